```python
import math
import jax, jax.numpy as jnp
from jax import lax
import numpy as np

D_MODEL = 2048
BATCH = 1
SEQ = 8192
DEPTH = 1
DEC_BATCH = 32
DEC_SEQ = 4
PAST_LEN = 8192
PAGE_SIZE = 128

N_META = 16
NORM_EPS = 1e-5
SSD_INNER = D_MODEL
SSD_HEAD_DIM = 64
SSD_HEADS = SSD_INNER // SSD_HEAD_DIM
SSD_GROUPS = 4
SSD_REP = SSD_HEADS // SSD_GROUPS
SSD_STATE = 128
SSD_CONV = 4
SSD_CHUNK = 128
CONV_DIM = SSD_INNER + 2 * SSD_GROUPS * SSD_STATE
ATT_HEADS = 8
ATT_KV_HEADS = 4
ATT_REP = ATT_HEADS // ATT_KV_HEADS
ATT_HEAD_DIM = D_MODEL // ATT_HEADS // 2
ATT_V_DIM = 2 * ATT_HEAD_DIM
ATT_SCALE = ATT_HEAD_DIM ** -0.5
ROT_DIM = ATT_HEAD_DIM // 4
ROPE_THETA = 500000.0
Q_BLOCK = 128
N_EXPERTS = 32
TOP_K = 4
D_FF = D_MODEL
SWIGLU_LIMIT = 7.0
SWIGLU_ALPHA = 1.702
MOE_BLOCK = 128
IN_SPLITS = (SSD_INNER, CONV_DIM, SSD_HEADS, ATT_HEADS * 2 * ATT_HEAD_DIM, ATT_KV_HEADS * 2 * ATT_HEAD_DIM, ATT_KV_HEADS * ATT_V_DIM, D_MODEL, D_MODEL)
IN_PROJ_DIM = sum(IN_SPLITS)
IN_OFFSETS = tuple(int(o) for o in np.cumsum(IN_SPLITS)[:-1])

kernel_name = 'hybrid_ssd_diffattn_moe_step'


def rmsnorm(x, w):
    xf = x.astype(jnp.float32)
    y = xf * lax.rsqrt(jnp.mean(xf * xf, axis=-1, keepdims=True) + NORM_EPS)
    return (y * w.astype(jnp.float32)).astype(x.dtype)


def rope_partial(x, pos):
    half = ROT_DIM // 2
    inv_freq = 1.0 / (ROPE_THETA ** (jnp.arange(half, dtype=jnp.float32) * (2.0 / ROT_DIM)))
    ang = pos.astype(jnp.float32)[:, None] * inv_freq[None, :]
    shape = (1, pos.shape[0]) + (1,) * (x.ndim - 3) + (half,)
    cos = jnp.cos(ang).reshape(shape).astype(x.dtype)
    sin = jnp.sin(ang).reshape(shape).astype(x.dtype)
    x1, x2, rest = x[..., :half], x[..., half:ROT_DIM], x[..., ROT_DIM:]
    return jnp.concatenate([x1 * cos - x2 * sin, x2 * cos + x1 * sin, rest], axis=-1)


def in_proj(h, norm_w, w):
    return jnp.split(rmsnorm(h, norm_w) @ w, IN_OFFSETS, axis=-1)


def causal_dwconv_silu(xpad, w, b, t_out):
    acc = b + sum(xpad[:, k:k + t_out] * w[k] for k in range(SSD_CONV))
    return jax.nn.silu(acc)


def split_ssd_inputs(xbc, dt_raw, dt_bias):
    lead = xbc.shape[:2]
    nb = SSD_GROUPS * SSD_STATE
    xs = xbc[..., :SSD_INNER].reshape(lead + (SSD_GROUPS, SSD_REP, SSD_HEAD_DIM))
    bm = xbc[..., SSD_INNER:SSD_INNER + nb].reshape(lead + (SSD_GROUPS, SSD_STATE))
    cm = xbc[..., SSD_INNER + nb:].reshape(lead + (SSD_GROUPS, SSD_STATE))
    dt = jax.nn.softplus(dt_raw + dt_bias).reshape(lead + (SSD_GROUPS, SSD_REP))
    return xs, dt, bm, cm


def segsum(a):
    t = a.shape[-1]
    cs = jnp.cumsum(a, axis=-1)
    diff = cs[..., :, None] - cs[..., None, :]
    return jnp.where(jnp.tril(jnp.ones((t, t), dtype=bool)), diff, -jnp.inf)


def ssd_chunked(xs, dt, a, bm, cm):
    bsz, t = xs.shape[:2]
    nc = t // SSD_CHUNK
    xdt = (xs * dt[..., None]).reshape(bsz, nc, SSD_CHUNK, SSD_GROUPS, SSD_REP, SSD_HEAD_DIM)
    da = jnp.moveaxis((dt * a).reshape(bsz, nc, SSD_CHUNK, SSD_GROUPS, SSD_REP), (3, 4), (1, 2))
    bc = bm.reshape(bsz, nc, SSD_CHUNK, SSD_GROUPS, SSD_STATE)
    cc = cm.reshape(bsz, nc, SSD_CHUNK, SSD_GROUPS, SSD_STATE)
    a_cs = jnp.cumsum(da, axis=-1)
    decay_in = jnp.exp(segsum(da))
    cb = jnp.einsum('bclgn,bcsgn->bcgls', cc, bc)
    y_diag = jnp.einsum('bcgls,bgrcls,bcsgrp->bclgrp', cb, decay_in, xdt)
    decay_to_end = jnp.exp(a_cs[..., -1:] - a_cs)
    chunk_states = jnp.einsum('bclgn,bgrcl,bclgrp->bcgrpn', bc, decay_to_end, xdt)
    chunk_states = jnp.concatenate([jnp.zeros_like(chunk_states[:, :1]), chunk_states], axis=1)
    chunk_decay = jnp.exp(segsum(jnp.pad(a_cs[..., -1], ((0, 0), (0, 0), (0, 0), (1, 0)))))
    states = jnp.einsum('bgrzc,bcgrpn->bzgrpn', chunk_decay, chunk_states)
    y_off = jnp.einsum('bclgn,bcgrpn,bgrcl->bclgrp', cc, states[:, :-1], jnp.exp(a_cs))
    y = (y_diag + y_off).reshape(bsz, t, SSD_GROUPS, SSD_REP, SSD_HEAD_DIM)
    return y, states[:, -1]


def ssd_scan(h0, xs, dt, a, bm, cm):
    def step(h, inp):
        x_t, dt_t, b_t, c_t = inp
        h = jnp.exp(dt_t * a)[..., None, None] * h + (x_t * dt_t[..., None])[..., None] * b_t[:, :, None, None, :]
        return h, jnp.einsum('bgrpn,bgn->bgrp', h, c_t)
    tm = lambda u: jnp.moveaxis(u, 1, 0)
    h, ys = lax.scan(step, h0, (tm(xs), tm(dt), tm(bm), tm(cm)))
    return jnp.moveaxis(ys, 0, 1), h


def ssd_branch_out(y, xs, z, d_skip, norm_w, w_out):
    lead = y.shape[:2]
    y = y + d_skip.reshape(SSD_GROUPS, SSD_REP)[..., None] * xs
    yz = y.reshape(lead + (SSD_INNER,)) * jax.nn.silu(z)
    gsz = SSD_INNER // SSD_GROUPS
    yz = rmsnorm(yz.reshape(lead + (SSD_GROUPS, gsz)), norm_w.reshape(SSD_GROUPS, gsz)).reshape(lead + (SSD_INNER,))
    return yz @ w_out


def split_heads(q, k, v):
    lead = q.shape[:2]
    return (q.reshape(lead + (ATT_KV_HEADS, ATT_REP, 2, ATT_HEAD_DIM)),
            k.reshape(lead + (ATT_KV_HEADS, 2, ATT_HEAD_DIM)),
            v.reshape(lead + (ATT_KV_HEADS, ATT_V_DIM)))


def diff_lambda(lq1, lk1, lq2, lk2, lam_init):
    f = lambda u, w: jnp.exp(jnp.sum(u.astype(jnp.float32) * w.astype(jnp.float32)))
    return f(lq1, lk1) - f(lq2, lk2) + lam_init


def diff_attend(q, k, v, q_pos, k_pos, lam):
    s = jnp.einsum('bqgrcd,bsgcd->bgrcqs', q, k).astype(jnp.float32) * ATT_SCALE
    s = jnp.where(k_pos[None, :] <= q_pos[:, None], s, -jnp.inf)
    p = jax.nn.softmax(s, axis=-1)
    w = (p[:, :, :, 0] - lam * p[:, :, :, 1]).astype(v.dtype)
    return jnp.einsum('bgrqs,bsge->bqgre', w, v)


def attn_branch_out(o, subln_w, lam_init, w_out):
    lead = o.shape[:2]
    o = rmsnorm(o, subln_w) * (1.0 - lam_init)
    return o.reshape(lead + (ATT_HEADS * ATT_V_DIM,)) @ w_out


def moe_ffn(x, router_w, router_b, w_gu, b_gu, w_dn, b_dn):
    t, d = x.shape
    logits = (x @ router_w).astype(jnp.float32) + router_b.astype(jnp.float32)
    top_logit, top_idx = lax.top_k(logits, TOP_K)
    gate = jax.nn.softmax(top_logit, axis=-1).astype(x.dtype)
    n_assign = t * TOP_K
    flat_e = top_idx.reshape(-1)
    flat_tok = jnp.repeat(jnp.arange(t, dtype=jnp.int32), TOP_K)
    flat_g = gate.reshape(-1)
    order = jnp.argsort(flat_e)
    sorted_e = flat_e[order]
    counts = jnp.bincount(flat_e, length=N_EXPERTS)
    padded = (counts + MOE_BLOCK - 1) // MOE_BLOCK * MOE_BLOCK
    pad_end = jnp.cumsum(padded)
    pad_start = pad_end - padded
    start = jnp.cumsum(counts) - counts
    dest = pad_start[sorted_e] + (jnp.arange(n_assign) - start[sorted_e])
    n_blocks = -(-n_assign // MOE_BLOCK) + N_EXPERTS
    n_rows = n_blocks * MOE_BLOCK
    row_tok = jnp.full((n_rows,), t, jnp.int32).at[dest].set(flat_tok[order])
    row_gate = jnp.zeros((n_rows,), x.dtype).at[dest].set(flat_g[order])
    block_e = jnp.minimum(jnp.searchsorted(pad_end, jnp.arange(n_blocks) * MOE_BLOCK, side='right'), N_EXPERTS - 1)
    x_ext = jnp.concatenate([x, jnp.zeros((1, d), x.dtype)], axis=0)
    xb = x_ext[row_tok].reshape(n_blocks, MOE_BLOCK, d)

    def expert_block(args):
        xblk, e = args
        hgu = xblk @ w_gu[e] + b_gu[e]
        g = jnp.minimum(hgu[:, :D_FF], SWIGLU_LIMIT)
        u = jnp.clip(hgu[:, D_FF:], -SWIGLU_LIMIT, SWIGLU_LIMIT)
        act = g * jax.nn.sigmoid(SWIGLU_ALPHA * g) * (u + 1.0)
        return act @ w_dn[e] + b_dn[e]

    yb = lax.map(expert_block, (xb, block_e)).reshape(n_rows, d)
    y = jnp.zeros((t + 1, d), x.dtype).at[row_tok].add(yb * row_gate[:, None])
    return y[:t]


def setup_inputs(seed: int = 0) -> dict:
    key = jax.random.key(seed)
    ks = jax.random.split(key, 32)
    f32 = jnp.float32
    n_pages = PAST_LEN // PAGE_SIZE
    n_used = DEC_BATCH * n_pages
    n_phys = n_used + max(1, n_used // 4)
    nrm = lambda k, shape, scale: jax.random.normal(k, shape, f32) * scale
    dt0 = jnp.exp(jax.random.uniform(ks[12], (DEPTH, SSD_HEADS), f32, math.log(1e-3), math.log(1e-1)))
    return {
        'x_prompt': nrm(ks[0], (BATCH, SEQ, D_MODEL), 1.0),
        'x_sample': nrm(ks[1], (DEC_BATCH, DEC_SEQ, D_MODEL), 1.0),
        'cache_k': nrm(ks[2], (DEPTH, n_phys, PAGE_SIZE, ATT_KV_HEADS, 2, ATT_HEAD_DIM), 1.0),
        'cache_v': nrm(ks[3], (DEPTH, n_phys, PAGE_SIZE, ATT_KV_HEADS, ATT_V_DIM), 1.0),
        'state_ssm': nrm(ks[4], (DEPTH, DEC_BATCH, SSD_HEADS, SSD_HEAD_DIM, SSD_STATE), 0.5),
        'state_conv': nrm(ks[5], (DEPTH, DEC_BATCH, SSD_CONV - 1, CONV_DIM), 1.0),
        'page_table': jax.random.permutation(ks[6], n_phys)[:n_used].reshape(DEC_BATCH, n_pages).astype(jnp.int32),
        'meta_tokens': nrm(ks[7], (N_META, D_MODEL), 1.0),
        'norm_mix_w': 1.0 + nrm(ks[8], (DEPTH, D_MODEL), 0.02),
        'w_in': nrm(ks[9], (DEPTH, D_MODEL, IN_PROJ_DIM), D_MODEL ** -0.5),
        'conv_w': nrm(ks[10], (DEPTH, SSD_CONV, CONV_DIM), SSD_CONV ** -0.5),
        'conv_b': nrm(ks[11], (DEPTH, CONV_DIM), 0.02),
        'dt_bias': dt0 + jnp.log(-jnp.expm1(-dt0)),
        'a_log': jnp.log(jax.random.uniform(ks[13], (DEPTH, SSD_HEADS), f32, 1.0, 16.0)),
        'd_skip': 1.0 + nrm(ks[14], (DEPTH, SSD_HEADS), 0.1),
        'ssd_norm_w': 1.0 + nrm(ks[15], (DEPTH, SSD_INNER), 0.02),
        'w_ssd_out': nrm(ks[16], (DEPTH, SSD_INNER, D_MODEL), SSD_INNER ** -0.5),
        'lambda_q1': nrm(ks[17], (DEPTH, ATT_HEAD_DIM), 0.1),
        'lambda_k1': nrm(ks[18], (DEPTH, ATT_HEAD_DIM), 0.1),
        'lambda_q2': nrm(ks[19], (DEPTH, ATT_HEAD_DIM), 0.1),
        'lambda_k2': nrm(ks[20], (DEPTH, ATT_HEAD_DIM), 0.1),
        'subln_w': 1.0 + nrm(ks[21], (DEPTH, ATT_V_DIM), 0.02),
        'w_attn_out': nrm(ks[22], (DEPTH, ATT_HEADS * ATT_V_DIM, D_MODEL), (ATT_HEADS * ATT_V_DIM) ** -0.5),
        'w_o': nrm(ks[23], (DEPTH, D_MODEL, D_MODEL), D_MODEL ** -0.5),
        'norm_ffn_w': 1.0 + nrm(ks[24], (DEPTH, D_MODEL), 0.02),
        'router_w': nrm(ks[25], (DEPTH, D_MODEL, N_EXPERTS), D_MODEL ** -0.5),
        'router_b': nrm(ks[26], (DEPTH, N_EXPERTS), 0.01),
        'w_gate_up': nrm(ks[27], (DEPTH, N_EXPERTS, D_MODEL, 2 * D_FF), D_MODEL ** -0.5),
        'b_gate_up': nrm(ks[28], (DEPTH, N_EXPERTS, 2 * D_FF), 0.01),
        'w_down': nrm(ks[29], (DEPTH, N_EXPERTS, D_FF, D_MODEL), D_FF ** -0.5),
        'b_down': nrm(ks[30], (DEPTH, N_EXPERTS, D_MODEL), 0.01),
        'final_norm_w': 1.0 + nrm(ks[31], (D_MODEL,), 0.02),
    }


def reference(x_prompt, x_sample, cache_k, cache_v, state_ssm, state_conv, page_table, meta_tokens,
              norm_mix_w, w_in, conv_w, conv_b, dt_bias, a_log, d_skip, ssd_norm_w, w_ssd_out,
              lambda_q1, lambda_k1, lambda_q2, lambda_k2, subln_w, w_attn_out, w_o, norm_ffn_w,
              router_w, router_b, w_gate_up, b_gate_up, w_down, b_down, final_norm_w):
    bp, seq = x_prompt.shape[:2]
    bd, ds = x_sample.shape[:2]
    past = page_table.shape[1] * PAGE_SIZE
    lp = seq + N_META
    hp = jnp.concatenate([jnp.broadcast_to(meta_tokens[None], (bp, N_META, D_MODEL)).astype(x_prompt.dtype), x_prompt], axis=1)
    hs = x_sample
    pos_p = jnp.arange(lp)
    pos_s = past + jnp.arange(ds)
    lead_pad = (-N_META) % SSD_CHUNK
    q_len = -(-lp // Q_BLOCK) * Q_BLOCK
    n_qblk = q_len // Q_BLOCK
    front = lambda u: jnp.pad(u, ((0, 0), (lead_pad, 0)) + ((0, 0),) * (u.ndim - 2))
    back = lambda u: jnp.pad(u, ((0, 0), (0, q_len - lp)) + ((0, 0),) * (u.ndim - 2))
    kp_rows, vp_rows, ks_rows, vs_rows = [], [], [], []
    ssm_p, ssm_s, conv_p, conv_s = [], [], [], []
    for l in range(DEPTH):
        lam_init = 0.8 - 0.6 * math.exp(-0.3 * l)
        lam = diff_lambda(lambda_q1[l], lambda_k1[l], lambda_q2[l], lambda_k2[l], lam_init)
        a = (-jnp.exp(a_log[l].astype(jnp.float32))).astype(hp.dtype).reshape(SSD_GROUPS, SSD_REP)

        z, xbc, dt_raw, q, k, v, g_a, g_b = in_proj(hp, norm_mix_w[l], w_in[l])
        xpad = jnp.concatenate([jnp.zeros((bp, SSD_CONV - 1, CONV_DIM), hp.dtype), xbc], axis=1)
        conv_p.append(xpad[:, -(SSD_CONV - 1):])
        xs, dt, bm, cm = split_ssd_inputs(causal_dwconv_silu(xpad, conv_w[l], conv_b[l], lp), dt_raw, dt_bias[l])
        y, h_fin = ssd_chunked(front(xs), front(dt), a, front(bm), front(cm))
        ssm_p.append(h_fin.reshape(bp, SSD_HEADS, SSD_HEAD_DIM, SSD_STATE))
        out_a = ssd_branch_out(y[:, lead_pad:], xs, z, d_skip[l], ssd_norm_w[l], w_ssd_out[l])

        q, k, v = split_heads(q, k, v)
        q = rope_partial(q, pos_p)
        k = rope_partial(k, pos_p)
        kp_rows.append(k)
        vp_rows.append(v)
        k_all, v_all = back(k), back(v)
        k_pos = jnp.arange(q_len)
        q_blocks = jnp.moveaxis(back(q).reshape((bp, n_qblk, Q_BLOCK) + q.shape[2:]), 1, 0)
        o = lax.map(lambda args: diff_attend(args[0], k_all, v_all, args[1] + jnp.arange(Q_BLOCK), k_pos, lam),
                    (q_blocks, jnp.arange(n_qblk) * Q_BLOCK))
        o = jnp.moveaxis(o, 0, 1).reshape((bp, q_len) + o.shape[3:])[:, :lp]
        out_b = attn_branch_out(o, subln_w[l], lam_init, w_attn_out[l])
        hp = hp + (jax.nn.sigmoid(g_a) * out_a + jax.nn.sigmoid(g_b) * out_b) @ w_o[l]

        z, xbc, dt_raw, q, k, v, g_a, g_b = in_proj(hs, norm_mix_w[l], w_in[l])
        xpad = jnp.concatenate([state_conv[l].astype(hs.dtype), xbc], axis=1)
        conv_s.append(xpad[:, -(SSD_CONV - 1):])
        xs, dt, bm, cm = split_ssd_inputs(causal_dwconv_silu(xpad, conv_w[l], conv_b[l], ds), dt_raw, dt_bias[l])
        h0 = state_ssm[l].astype(hs.dtype).reshape(bd, SSD_GROUPS, SSD_REP, SSD_HEAD_DIM, SSD_STATE)
        y, h_fin = ssd_scan(h0, xs, dt, a, bm, cm)
        ssm_s.append(h_fin.reshape(bd, SSD_HEADS, SSD_HEAD_DIM, SSD_STATE))
        out_a = ssd_branch_out(y, xs, z, d_skip[l], ssd_norm_w[l], w_ssd_out[l])

        q, k, v = split_heads(q, k, v)
        q = rope_partial(q, pos_s)
        k = rope_partial(k, pos_s)
        ks_rows.append(k)
        vs_rows.append(v)
        k_past = cache_k[l][page_table].reshape((bd, past) + cache_k.shape[3:]).astype(k.dtype)
        v_past = cache_v[l][page_table].reshape((bd, past) + cache_v.shape[3:]).astype(v.dtype)
        o = diff_attend(q, jnp.concatenate([k_past, k], axis=1), jnp.concatenate([v_past, v], axis=1),
                        pos_s, jnp.arange(past + ds), lam)
        out_b = attn_branch_out(o, subln_w[l], lam_init, w_attn_out[l])
        hs = hs + (jax.nn.sigmoid(g_a) * out_a + jax.nn.sigmoid(g_b) * out_b) @ w_o[l]

        n_all = jnp.concatenate([rmsnorm(hp, norm_ffn_w[l]).reshape(-1, D_MODEL),
                                 rmsnorm(hs, norm_ffn_w[l]).reshape(-1, D_MODEL)], axis=0)
        f = moe_ffn(n_all, router_w[l], router_b[l], w_gate_up[l], b_gate_up[l], w_down[l], b_down[l])
        hp = hp + f[:bp * lp].reshape(hp.shape)
        hs = hs + f[bp * lp:].reshape(hs.shape)

    y_prompt = rmsnorm(hp[:, N_META:], final_norm_w)
    y_sample = rmsnorm(hs, final_norm_w)
    return (y_prompt, y_sample, jnp.stack(kp_rows), jnp.stack(vp_rows), jnp.stack(ks_rows), jnp.stack(vs_rows),
            jnp.stack(ssm_p), jnp.stack(ssm_s), jnp.stack(conv_p), jnp.stack(conv_s))
```

```python
import functools
import math

import jax
import jax.numpy as jnp
import numpy as np
from jax import lax
from jax.experimental import pallas as pl
from jax.experimental.pallas import tpu as pltpu

F32 = jnp.float32
BF16 = jnp.bfloat16
I32 = jnp.int32
HI = lax.Precision.HIGHEST

D_MODEL = 2048
SEQ = 8192
DEC_BATCH = 32
DEC_SEQ = 4
PAGE_SIZE = 128
N_META = 16
NORM_EPS = 1e-5
SSD_INNER = D_MODEL
SSD_HEAD_DIM = 64
SSD_HEADS = SSD_INNER // SSD_HEAD_DIM
SSD_GROUPS = 4
SSD_REP = SSD_HEADS // SSD_GROUPS
SSD_STATE = 128
SSD_CONV = 4
SSD_CHUNK = 128
CONV_DIM = SSD_INNER + 2 * SSD_GROUPS * SSD_STATE
ATT_HEADS = 8
ATT_KV_HEADS = 4
ATT_REP = ATT_HEADS // ATT_KV_HEADS
ATT_HEAD_DIM = D_MODEL // ATT_HEADS // 2
ATT_V_DIM = 2 * ATT_HEAD_DIM
ATT_SCALE = ATT_HEAD_DIM ** -0.5
ROT_DIM = ATT_HEAD_DIM // 4
ROPE_THETA = 500000.0
N_EXPERTS = 32
TOP_K = 4
D_FF = D_MODEL
SWIGLU_LIMIT = 7.0
SWIGLU_ALPHA = 1.702
LAM_INIT = 0.8 - 0.6 * math.exp(-0.3 * 0)

LEAD = (-N_META) % SSD_CHUNK
P_ROWS = LEAD + N_META + SEQ
S_ROWS = DEC_BATCH * DEC_SEQ
R = P_ROWS + S_ROWS
N_CHUNKS = P_ROWS // SSD_CHUNK

LANES = 128
SUBLANES = 8
VMEM_LIMIT = 56 * 1024 * 1024

PCOL_Z, PCOL_GA, PCOL_GB, PCOL_XBC, PCOL_Q, PCOL_K, PCOL_V = 0, 2048, 4096, 6144, 9216, 11264, 12288
P_COLS = 13312
PROJ_TN = 1024
ROPE_BLOCKS = (PCOL_Q // PROJ_TN, PCOL_V // PROJ_TN)

MOE_TM = 256
MOE_NB = -(-(R * TOP_K) // MOE_TM) + N_EXPERTS
MOE_DUMP_BLOCKS = -(-(LEAD * TOP_K) // MOE_TM)
MOE_ROWS = (MOE_NB + MOE_DUMP_BLOCKS) * MOE_TM
MOE_TN = 512
NEG = -1e30


def _cparams(sem, vmem=VMEM_LIMIT):
    return pltpu.CompilerParams(dimension_semantics=sem, vmem_limit_bytes=vmem)


def _sigmoid(x):
    return 1.0 / (1.0 + jnp.exp(-x))


def _rmsnorm_kernel(x_ref, w_ref, o_ref):
    x = x_ref[...]
    ms = jnp.mean(x * x, axis=-1, keepdims=True)
    o_ref[...] = (x * lax.rsqrt(ms + NORM_EPS) * w_ref[...]).astype(o_ref.dtype)


def rmsnorm_rows(x, w, tm=768):
    rows, d = x.shape
    return pl.pallas_call(
        _rmsnorm_kernel,
        grid=(rows // tm,),
        in_specs=[pl.BlockSpec((tm, d), lambda i: (i, 0)), pl.BlockSpec((1, d), lambda i: (0, 0))],
        out_specs=pl.BlockSpec((tm, d), lambda i: (i, 0)),
        out_shape=jax.ShapeDtypeStruct((rows, d), BF16),
        compiler_params=_cparams(("parallel",)),
        name="rmsnorm_rows",
    )(x, w.reshape(1, d))


def _rope(blk, c, sa, sb):
    return blk * c + pltpu.roll(blk, ROT_DIM // 2, axis=1) * sa + pltpu.roll(blk, LANES - ROT_DIM // 2, axis=1) * sb


def _proj_kernel(x_ref, w_ref, c_ref, sa_ref, sb_ref, o_ref):
    j = pl.program_id(1)
    acc = jnp.dot(x_ref[...], w_ref[...], preferred_element_type=F32)
    is_rope = (j >= ROPE_BLOCKS[0]) & (j < ROPE_BLOCKS[1])

    @pl.when(is_rope)
    def _():
        c, sa, sb = c_ref[...], sa_ref[...], sb_ref[...]
        for g in range(acc.shape[1] // LANES):
            sl = slice(g * LANES, (g + 1) * LANES)
            o_ref[:, sl] = _rope(acc[:, sl], c, sa, sb)

    @pl.when(jnp.logical_not(is_rope))
    def _():
        o_ref[...] = acc


def in_projection(xn, w_bf, cos_t, sin_a, sin_b, tm=1056):
    rows, d = xn.shape
    tn = PROJ_TN
    tab = pl.BlockSpec((tm, LANES), lambda i, j: (i, 0))
    return pl.pallas_call(
        _proj_kernel,
        grid=(rows // tm, P_COLS // tn),
        in_specs=[pl.BlockSpec((tm, d), lambda i, j: (i, 0)), pl.BlockSpec((d, tn), lambda i, j: (0, j)), tab, tab, tab],
        out_specs=pl.BlockSpec((tm, tn), lambda i, j: (i, j)),
        out_shape=jax.ShapeDtypeStruct((rows, P_COLS), F32),
        compiler_params=_cparams(("parallel", "arbitrary")),
        name="in_projection",
    )(xn, w_bf, cos_t, sin_a, sin_b)


def _mm_kernel(x_ref, w_ref, o_ref):
    o_ref[...] = jnp.dot(x_ref[...], w_ref[...], preferred_element_type=F32).astype(o_ref.dtype)


def dt_projection(xn, w_dt, tm=1056):
    rows, d = xn.shape
    return pl.pallas_call(
        _mm_kernel,
        grid=(rows // tm,),
        in_specs=[pl.BlockSpec((tm, d), lambda i: (i, 0)), pl.BlockSpec((d, LANES), lambda i: (0, 0))],
        out_specs=pl.BlockSpec((tm, LANES), lambda i: (i, 0)),
        out_shape=jax.ShapeDtypeStruct((rows, LANES), F32),
        compiler_params=_cparams(("parallel",)),
        name="dt_projection",
    )(xn, w_dt)


def _ssd_chunk(xprev, xcur, dtraw, z, valid, s_ref, cw_ref, cb_ref, dtb_ref, alog_ref, dsk_ref, nw_ref, e_ref,
               state_dot_precision):
    L = xcur.shape[0]
    ext = jnp.concatenate([xprev, xcur], axis=0)
    off = SUBLANES - (SSD_CONV - 1)
    acc = cb_ref[...]
    for k in range(SSD_CONV):
        acc = acc + ext[off + k:off + k + L] * cw_ref[k:k + 1, :]
    act = acc * _sigmoid(acc)
    xs = act[:, :SSD_INNER]
    nb = SSD_GROUPS * SSD_STATE
    bm = act[:, SSD_INNER:SSD_INNER + nb].astype(BF16)
    cm = act[:, SSD_INNER + nb:].astype(BF16)

    dpre = dtraw + dtb_ref[...]
    dt = jnp.maximum(dpre, 0.0) + jnp.log1p(jnp.exp(-jnp.abs(dpre)))
    dt = jnp.where(valid, dt, 0.0)
    da = dt * (-jnp.exp(alog_ref[...]))

    li = lax.broadcasted_iota(I32, (L, L), 0)
    si = lax.broadcasted_iota(I32, (L, L), 1)
    tril = li >= si
    eye = li == si
    a_cs = jnp.dot(tril.astype(F32), da, precision=HI, preferred_element_type=F32)
    a_tot = a_cs[L - 1:L, :]
    e_mat = e_ref[...]
    dt_x = jnp.dot(dt, e_mat, precision=HI, preferred_element_type=F32)
    eacs_x = jnp.dot(jnp.exp(a_cs), e_mat, precision=HI, preferred_element_type=F32)
    dte_x = jnp.dot(jnp.exp(a_tot - a_cs), e_mat, precision=HI, preferred_element_type=F32)
    etot = jnp.exp(a_tot)

    xdt = xs * dt_x
    xdt_b = xdt.astype(BF16)
    xde = xdt * dte_x
    gw = SSD_REP * SSD_HEAD_DIM
    lane = lax.broadcasted_iota(I32, (L, LANES), 1)
    nt = (((1,), (1,)), ((), ()))
    tn = (((0,), (0,)), ((), ()))
    y_groups = []
    for g in range(SSD_GROUPS):
        bg = bm[:, g * SSD_STATE:(g + 1) * SSD_STATE]
        cg = cm[:, g * SSD_STATE:(g + 1) * SSD_STATE]
        cb = lax.dot_general(cg, bg, nt, preferred_element_type=F32)
        s_g = s_ref[g * gw:(g + 1) * gw, :]
        y_off = lax.dot_general(cg, s_g.astype(BF16), nt, preferred_element_type=F32)
        if state_dot_precision is None:
            upd = lax.dot_general(xde[:, g * gw:(g + 1) * gw].astype(BF16), bg, tn, preferred_element_type=F32)
        else:
            upd = lax.dot_general(xde[:, g * gw:(g + 1) * gw], act[:, SSD_INNER + g * SSD_STATE:SSD_INNER + (g + 1) * SSD_STATE],
                                  tn, precision=state_dot_precision, preferred_element_type=F32)
        pairs = []
        decs = []
        for q in range(SSD_REP // 2):
            yd = []
            for r in range(2):
                h = g * SSD_REP + 2 * q + r
                colb = jnp.broadcast_to(a_cs[:, h:h + 1], (L, L))
                rowb = jnp.sum(jnp.where(eye, colb, 0.0), axis=0, keepdims=True)
                lm = jnp.where(tril, jnp.exp(jnp.minimum(colb - rowb, 0.0)), 0.0)
                m = (cb * lm).astype(BF16)
                c0 = (g * SSD_REP + 2 * q) * SSD_HEAD_DIM
                yd.append(jnp.dot(m, xdt_b[:, c0:c0 + LANES], preferred_element_type=F32))
                decs.append(jnp.broadcast_to(etot[:, h:h + 1], (SSD_HEAD_DIM, SSD_STATE)))
            pairs.append(jnp.where(lane < SSD_HEAD_DIM, yd[0], yd[1]))
        y_g = jnp.concatenate(pairs, axis=1) + y_off * eacs_x[:, g * gw:(g + 1) * gw]
        y_groups.append(y_g)
        s_ref[g * gw:(g + 1) * gw, :] = jnp.concatenate(decs, axis=0) * s_g + upd
    y = jnp.concatenate(y_groups, axis=1) + dsk_ref[...] * xs
    yz = y * (z * _sigmoid(z))
    outs = []
    for g in range(SSD_GROUPS):
        yg = yz[:, g * gw:(g + 1) * gw]
        ms = jnp.mean(yg * yg, axis=-1, keepdims=True)
        outs.append(yg * lax.rsqrt(ms + NORM_EPS) * nw_ref[:, g * gw:(g + 1) * gw])
    return jnp.concatenate(outs, axis=1)


def _ssd_prompt_kernel(xprev_ref, xcur_ref, dt_ref, z_ref, cw_ref, cb_ref, dtb_ref, alog_ref, dsk_ref, nw_ref, e_ref,
                       y_ref, s_ref):
    c = pl.program_id(0)

    @pl.when(c == 0)
    def _():
        s_ref[...] = jnp.zeros_like(s_ref)

    xprev = jnp.where(c == 0, 0.0, xprev_ref[...])
    row = c * SSD_CHUNK + lax.broadcasted_iota(I32, (SSD_CHUNK, 1), 0)
    y = _ssd_chunk(xprev, xcur_ref[...], dt_ref[...], z_ref[...], row >= LEAD, s_ref,
                   cw_ref, cb_ref, dtb_ref, alog_ref, dsk_ref, nw_ref, e_ref, None)
    y_ref[...] = y.astype(y_ref.dtype)


def _full(shape):
    return pl.BlockSpec(shape, lambda *_: (0,) * len(shape))


def ssd_prompt(proj, dtraw, params):
    L = SSD_CHUNK
    xb = PCOL_XBC // CONV_DIM
    in_specs = [
        pl.BlockSpec((SUBLANES, CONV_DIM), lambda c: (jnp.maximum(c * (L // SUBLANES) - 1, 0), xb)),
        pl.BlockSpec((L, CONV_DIM), lambda c: (c, xb)),
        pl.BlockSpec((L, LANES), lambda c: (c, 0)),
        pl.BlockSpec((L, SSD_INNER), lambda c: (c, PCOL_Z // SSD_INNER)),
    ] + [_full(p.shape) for p in params]
    return pl.pallas_call(
        _ssd_prompt_kernel,
        grid=(N_CHUNKS,),
        in_specs=in_specs,
        out_specs=[pl.BlockSpec((L, SSD_INNER), lambda c: (c, 0)), _full((SSD_INNER, SSD_STATE))],
        out_shape=[jax.ShapeDtypeStruct((P_ROWS, SSD_INNER), BF16), jax.ShapeDtypeStruct((SSD_INNER, SSD_STATE), F32)],
        compiler_params=_cparams(("arbitrary",)),
        name="ssd_prompt",
    )(proj, proj, dtraw, proj, *params)


SSD_S_NB = 4


def _ssd_sample_kernel(xprev_ref, xcur_ref, dt_ref, z_ref, sin_ref, cw_ref, cb_ref, dtb_ref, alog_ref, dsk_ref, nw_ref,
                       e_ref, y_ref, sout_ref):
    L = SUBLANES
    valid = lax.broadcasted_iota(I32, (L, 1), 0) < DEC_SEQ
    ys = []
    for b in range(SSD_S_NB):
        rows = slice(b * DEC_SEQ, (b + 1) * DEC_SEQ)
        pad = lambda u: jnp.concatenate([u, jnp.zeros((L - DEC_SEQ, u.shape[1]), u.dtype)], axis=0)
        sout_ref[b] = sin_ref[b]
        y = _ssd_chunk(xprev_ref[b], pad(xcur_ref[rows, :]), pad(dt_ref[rows, :]), pad(z_ref[rows, :]), valid,
                       sout_ref.at[b], cw_ref, cb_ref, dtb_ref, alog_ref, dsk_ref, nw_ref, e_ref, HI)
        ys.append(y[:DEC_SEQ])
    y_ref[...] = jnp.concatenate(ys, axis=0).astype(y_ref.dtype)


def ssd_sample(proj, dtraw, conv_prev, state, params):
    nb = SSD_S_NB
    rb = nb * DEC_SEQ
    r0 = P_ROWS // rb
    in_specs = [
        pl.BlockSpec((nb, SUBLANES, CONV_DIM), lambda i: (i, 0, 0)),
        pl.BlockSpec((rb, CONV_DIM), lambda i: (r0 + i, PCOL_XBC // CONV_DIM)),
        pl.BlockSpec((rb, LANES), lambda i: (r0 + i, 0)),
        pl.BlockSpec((rb, SSD_INNER), lambda i: (r0 + i, PCOL_Z // SSD_INNER)),
        pl.BlockSpec((nb, SSD_INNER, SSD_STATE), lambda i: (i, 0, 0)),
    ] + [_full(p.shape) for p in params]
    return pl.pallas_call(
        _ssd_sample_kernel,
        grid=(DEC_BATCH // nb,),
        in_specs=in_specs,
        out_specs=[pl.BlockSpec((rb, SSD_INNER), lambda i: (i, 0)),
                   pl.BlockSpec((nb, SSD_INNER, SSD_STATE), lambda i: (i, 0, 0))],
        out_shape=[jax.ShapeDtypeStruct((S_ROWS, SSD_INNER), BF16),
                   jax.ShapeDtypeStruct((DEC_BATCH, SSD_INNER, SSD_STATE), F32)],
        compiler_params=_cparams(("parallel",)),
        name="ssd_sample",
    )(conv_prev, proj, dtraw, proj, state, *params)


def _lambda(lq1_ref, lk1_ref, lq2_ref, lk2_ref):
    s1 = jnp.sum(lq1_ref[...] * lk1_ref[...], axis=-1, keepdims=True)
    s2 = jnp.sum(lq2_ref[...] * lk2_ref[...], axis=-1, keepdims=True)
    return jnp.exp(s1) - jnp.exp(s2) + LAM_INIT


def _subnorm(o, subw_ref):
    ms = jnp.mean(o * o, axis=-1, keepdims=True)
    return (o * lax.rsqrt(ms + NORM_EPS) * subw_ref[...]) * (1.0 - LAM_INIT)


ATT_TQ = 256
ATT_TK = 256


def _attn_prompt_kernel(q_ref, k_ref, v_ref, lq1_ref, lk1_ref, lq2_ref, lk2_ref, subw_ref, o_ref, m_s, l_s, acc_s):
    i = pl.program_id(1)
    tq, tk, hd = ATT_TQ, ATT_TK, ATT_HEAD_DIM
    q = q_ref[...]
    qc = [jnp.concatenate([q[:, r * 2 * hd + c * hd: r * 2 * hd + (c + 1) * hd] for r in range(ATT_REP)], axis=0).astype(BF16)
          for c in range(2)]
    m_s[...] = jnp.full_like(m_s, NEG)
    l_s[...] = jnp.zeros_like(l_s)
    acc_s[...] = jnp.zeros_like(acc_s)
    nt = (((1,), (1,)), ((), ()))

    def step(j, masked):
        k0 = pl.multiple_of(j * tk, tk)
        kblk = k_ref[pl.ds(k0, tk), :]
        vblk = v_ref[pl.ds(k0, tk), :]
        for c in range(2):
            s = lax.dot_general(qc[c], kblk[:, c * hd:(c + 1) * hd], nt, preferred_element_type=F32) * ATT_SCALE
            if masked:
                qidx = i * tq + lax.broadcasted_iota(I32, s.shape, 0) % tq
                kidx = j * tk + lax.broadcasted_iota(I32, s.shape, 1)
                ok = (kidx <= qidx) & ((kidx >= LEAD) | (qidx < LEAD))
                s = jnp.where(ok, s, NEG)
            m_old = m_s[c]
            m_new = jnp.maximum(m_old, jnp.max(s, axis=1, keepdims=True))
            alpha = jnp.exp(m_old - m_new)
            p = jnp.exp(s - m_new[:, :1])
            l_s[c] = alpha * l_s[c] + jnp.sum(p, axis=1, keepdims=True)
            acc_s[c] = alpha[:, :1] * acc_s[c] + jnp.dot(p.astype(BF16), vblk, preferred_element_type=F32)
            m_s[c] = m_new

    step(0, True)

    def body(j, carry):
        step(j, False)
        return carry

    lax.fori_loop(1, i, body, 0)

    @pl.when(i > 0)
    def _():
        step(i, True)

    lam = _lambda(lq1_ref, lk1_ref, lq2_ref, lk2_ref)
    for r in range(ATT_REP):
        rows = slice(r * tq, (r + 1) * tq)
        o0 = acc_s[0, rows, :] / l_s[0, rows, :1]
        o1 = acc_s[1, rows, :] / l_s[1, rows, :1]
        o = _subnorm(o0 - lam * o1, subw_ref)
        o_ref[:, r * ATT_V_DIM:(r + 1) * ATT_V_DIM] = o.astype(o_ref.dtype)


def attn_prompt(proj, kb, vb, lam_params, subw):
    tq = ATT_TQ
    gq = ATT_REP * 2 * ATT_HEAD_DIM
    vec = _full((1, ATT_HEAD_DIM))
    return pl.pallas_call(
        _attn_prompt_kernel,
        grid=(ATT_KV_HEADS, R // tq),
        in_specs=[pl.BlockSpec((tq, gq), lambda g, i: (i, PCOL_Q // gq + g)),
                  pl.BlockSpec((R, ATT_V_DIM), lambda g, i: (0, g)),
                  pl.BlockSpec((R, ATT_V_DIM), lambda g, i: (0, g)),
                  vec, vec, vec, vec, _full((1, ATT_V_DIM))],
        out_specs=pl.BlockSpec((tq, gq), lambda g, i: (i, g)),
        out_shape=jax.ShapeDtypeStruct((R, ATT_HEADS * ATT_V_DIM), BF16),
        scratch_shapes=[pltpu.VMEM((2, ATT_REP * tq, LANES), F32), pltpu.VMEM((2, ATT_REP * tq, LANES), F32),
                        pltpu.VMEM((2, ATT_REP * tq, ATT_V_DIM), F32)],
        compiler_params=_cparams(("parallel", "arbitrary")),
        name="attn_prompt",
    )(proj, kb, vb, *lam_params, subw)


N_PAGES = SEQ // PAGE_SIZE
ATT_S_ROWS = ATT_KV_HEADS * 2 * DEC_SEQ * ATT_REP
ATT_S_GROUP = 2 * DEC_SEQ * ATT_REP


def _attn_sample_kernel(pt_ref, q_ref, kc_ref, vc_ref, kn_ref, vn_ref, lq1_ref, lk1_ref, lq2_ref, lk2_ref, subw_ref,
                        o_ref, m_s, l_s, acc_s):
    j = pl.program_id(1)
    nt = (((1,), (1,)), ((), ()))

    @pl.when(j == 0)
    def _():
        m_s[...] = jnp.full_like(m_s, NEG)
        l_s[...] = jnp.zeros_like(l_s)
        acc_s[...] = jnp.zeros_like(acc_s)

    def step(k_src, v_src, masked):
        kb = k_src[0].astype(BF16)
        vb = v_src[0].astype(BF16)
        s = lax.dot_general(q_ref[0], kb, nt, preferred_element_type=F32) * ATT_SCALE
        if masked:
            t_q = (lax.broadcasted_iota(I32, s.shape, 0) % (DEC_SEQ * ATT_REP)) // ATT_REP
            t_k = lax.broadcasted_iota(I32, s.shape, 1)
            s = jnp.where(t_k <= t_q, s, NEG)
        m_old = m_s[...]
        m_new = jnp.maximum(m_old, jnp.max(s, axis=1, keepdims=True))
        alpha = jnp.exp(m_old - m_new)
        p = jnp.exp(s - m_new[:, :1])
        l_s[...] = alpha * l_s[...] + jnp.sum(p, axis=1, keepdims=True)
        pb = p.astype(BF16)
        for g in range(ATT_KV_HEADS):
            rows = slice(g * ATT_S_GROUP, (g + 1) * ATT_S_GROUP)
            pv = jnp.dot(pb[rows, :], vb[:, g * ATT_V_DIM:(g + 1) * ATT_V_DIM], preferred_element_type=F32)
            acc_s[rows, :] = alpha[rows, :1] * acc_s[rows, :] + pv
        m_s[...] = m_new

    @pl.when(j < N_PAGES)
    def _():
        step(kc_ref, vc_ref, False)

    @pl.when(j == N_PAGES)
    def _():
        step(kn_ref, vn_ref, True)
        lam = _lambda(lq1_ref, lk1_ref, lq2_ref, lk2_ref)
        half = DEC_SEQ * ATT_REP
        for g in range(ATT_KV_HEADS):
            r0 = g * ATT_S_GROUP
            o0 = acc_s[r0:r0 + half, :] / l_s[r0:r0 + half, :1]
            o1 = acc_s[r0 + half:r0 + 2 * half, :] / l_s[r0 + half:r0 + 2 * half, :1]
            o_ref[0, g] = _subnorm(o0 - lam * o1, subw_ref)


def attn_sample(page_table, qbd, kc, vc, knew, vnew, lam_params, subw):
    n_tok = PAGE_SIZE
    kw = kc.shape[-1]
    vec = _full((1, ATT_HEAD_DIM))

    def page(b, j, pt):
        return (pt[b * N_PAGES + jnp.minimum(j, N_PAGES - 1)], 0, 0)

    grid_spec = pltpu.PrefetchScalarGridSpec(
        num_scalar_prefetch=1,
        grid=(DEC_BATCH, N_PAGES + 1),
        in_specs=[pl.BlockSpec((1, ATT_S_ROWS, kw), lambda b, j, pt: (b, 0, 0)),
                  pl.BlockSpec((1, n_tok, kw), page),
                  pl.BlockSpec((1, n_tok, kw), page),
                  pl.BlockSpec((1, n_tok, kw), lambda b, j, pt: (b, 0, 0)),
                  pl.BlockSpec((1, n_tok, kw), lambda b, j, pt: (b, 0, 0)),
                  vec, vec, vec, vec, _full((1, ATT_V_DIM))],
        out_specs=pl.BlockSpec((1, ATT_KV_HEADS, DEC_SEQ * ATT_REP, ATT_V_DIM), lambda b, j, pt: (b, 0, 0, 0)),
        scratch_shapes=[pltpu.VMEM((ATT_S_ROWS, LANES), F32), pltpu.VMEM((ATT_S_ROWS, LANES), F32),
                        pltpu.VMEM((ATT_S_ROWS, ATT_V_DIM), F32)],
    )
    return pl.pallas_call(
        _attn_sample_kernel,
        grid_spec=grid_spec,
        out_shape=jax.ShapeDtypeStruct((DEC_BATCH, ATT_KV_HEADS, DEC_SEQ * ATT_REP, ATT_V_DIM), F32),
        compiler_params=_cparams(("parallel", "arbitrary")),
        name="attn_sample",
    )(page_table, qbd, kc, vc, knew, vnew, *lam_params, subw)


def _mix_kernel(ya_ref, yb_ref, ga_ref, gb_ref, wa_ref, wb_ref, o_ref):
    oa = jnp.dot(ya_ref[...], wa_ref[...], preferred_element_type=F32)
    ob = jnp.dot(yb_ref[...], wb_ref[...], preferred_element_type=F32)
    o_ref[...] = (_sigmoid(ga_ref[...]) * oa + _sigmoid(gb_ref[...]) * ob).astype(o_ref.dtype)


def mix_branches(yzn, on, proj, wa, wb, tm=256):
    d = D_MODEL
    row = lambda i: (i, 0)
    return pl.pallas_call(
        _mix_kernel,
        grid=(R // tm,),
        in_specs=[pl.BlockSpec((tm, d), row), pl.BlockSpec((tm, d), row),
                  pl.BlockSpec((tm, d), lambda i: (i, PCOL_GA // d)), pl.BlockSpec((tm, d), lambda i: (i, PCOL_GB // d)),
                  _full((d, d)), _full((d, d))],
        out_specs=pl.BlockSpec((tm, d), row),
        out_shape=jax.ShapeDtypeStruct((R, d), BF16),
        compiler_params=_cparams(("parallel",)),
        name="mix_branches",
    )(yzn, on, proj, proj, wa, wb)


def _post_kernel(mix_ref, h_ref, wo_ref, nw_ref, rw_ref, rb_ref, h2_ref, n_ref, ti_ref, tg_ref):
    h2 = h_ref[...] + jnp.dot(mix_ref[...], wo_ref[...], preferred_element_type=F32)
    h2_ref[...] = h2
    ms = jnp.mean(h2 * h2, axis=-1, keepdims=True)
    n = h2 * lax.rsqrt(ms + NORM_EPS) * nw_ref[...]
    n_ref[...] = n
    logits = jnp.dot(n, rw_ref[...], precision=HI, preferred_element_type=F32) + rb_ref[...]
    lane = lax.broadcasted_iota(I32, logits.shape, 1)
    work = logits
    vals, idxs = [], []
    for _ in range(TOP_K):
        mx = jnp.max(work, axis=1, keepdims=True)
        am = jnp.min(jnp.where(work == mx, lane, LANES), axis=1, keepdims=True)
        vals.append(mx)
        idxs.append(am)
        work = jnp.where(lane == am, -jnp.inf, work)
    es = [jnp.exp(v - vals[0]) for v in vals]
    den = es[0] + es[1] + es[2] + es[3]
    ti = jnp.zeros(logits.shape, I32)
    tg = jnp.zeros(logits.shape, F32)
    for k in range(TOP_K):
        ti = jnp.where(lane == k, idxs[k], ti)
        tg = jnp.where(lane == k, es[k] / den, tg)
    ti_ref[...] = ti
    tg_ref[...] = tg


def post_mixer(mix, h, wo, nw, rw, rb, tm=256):
    d = D_MODEL
    row = lambda i: (i, 0)
    return pl.pallas_call(
        _post_kernel,
        grid=(R // tm,),
        in_specs=[pl.BlockSpec((tm, d), row), pl.BlockSpec((tm, d), row), _full((d, d)), _full((1, d)),
                  _full((d, LANES)), _full((1, LANES))],
        out_specs=[pl.BlockSpec((tm, d), row), pl.BlockSpec((tm, d), row),
                   pl.BlockSpec((tm, LANES), row), pl.BlockSpec((tm, LANES), row)],
        out_shape=[jax.ShapeDtypeStruct((R, d), F32), jax.ShapeDtypeStruct((R, d), F32),
                   jax.ShapeDtypeStruct((R, LANES), I32), jax.ShapeDtypeStruct((R, LANES), F32)],
        compiler_params=_cparams(("parallel",)),
        name="post_mixer",
    )(mix, h, wo, nw, rw, rb)


def _rank_kernel(ti_ref, rank_ref, cnt_ref, carry_s):
    i = pl.program_id(0)
    tm = ti_ref.shape[0]

    @pl.when(i == 0)
    def _():
        carry_s[...] = jnp.zeros_like(carry_s)

    ti = ti_ref[...]
    lane = lax.broadcasted_iota(I32, ti.shape, 1)
    row = i * tm + lax.broadcasted_iota(I32, (tm, 1), 0)
    valid = row >= LEAD
    ohs = [jnp.where((lane == ti[:, k:k + 1]) & valid, 1.0, 0.0) for k in range(TOP_K)]
    osum = ohs[0] + ohs[1] + ohs[2] + ohs[3]
    li = lax.broadcasted_iota(I32, (tm, tm), 0)
    si = lax.broadcasted_iota(I32, (tm, tm), 1)
    before = jnp.dot((li > si).astype(BF16), osum.astype(BF16), preferred_element_type=F32) + carry_s[...]
    rank = jnp.zeros(ti.shape, I32)
    for k in range(TOP_K):
        rk = jnp.sum(ohs[k] * before, axis=1, keepdims=True)
        rank = jnp.where(lane == k, rk.astype(I32), rank)
    rank_ref[...] = rank
    carry_s[...] = carry_s[...] + jnp.sum(osum, axis=0, keepdims=True)
    cnt_ref[...] = carry_s[...].astype(I32)


def expert_ranks(ti, tm=256):
    return pl.pallas_call(
        _rank_kernel,
        grid=(R // tm,),
        in_specs=[pl.BlockSpec((tm, LANES), lambda i: (i, 0))],
        out_specs=[pl.BlockSpec((tm, LANES), lambda i: (i, 0)), _full((1, LANES))],
        out_shape=[jax.ShapeDtypeStruct((R, LANES), I32), jax.ShapeDtypeStruct((1, LANES), I32)],
        scratch_shapes=[pltpu.VMEM((1, LANES), F32)],
        compiler_params=_cparams(("arbitrary",)),
        name="expert_ranks",
    )(ti)


DISPATCH_TM = 256


def _dispatch_kernel(dest_ref, n_ref, xs_in_ref, xs_ref, sem):
    del xs_in_ref
    i = pl.program_id(0)
    tm = DISPATCH_TM

    def body(r, carry):
        for k in range(TOP_K):
            d = dest_ref[(i * tm + r) * TOP_K + k]
            pltpu.make_async_copy(n_ref.at[pl.ds(r, 1), :], xs_ref.at[pl.ds(d, 1), :], sem).start()
        return carry

    lax.fori_loop(0, tm, body, 0)
    for k in range(TOP_K):
        pltpu.make_async_copy(n_ref, xs_ref.at[pl.ds(0, tm), :], sem).wait()


def moe_dispatch(dest_flat, n, xs_init):
    tm = DISPATCH_TM
    grid_spec = pltpu.PrefetchScalarGridSpec(
        num_scalar_prefetch=1,
        grid=(R // tm,),
        in_specs=[pl.BlockSpec((tm, D_MODEL), lambda i, d: (i, 0)), pl.BlockSpec(memory_space=pl.ANY)],
        out_specs=pl.BlockSpec(memory_space=pl.ANY),
        scratch_shapes=[pltpu.SemaphoreType.DMA],
    )
    return pl.pallas_call(
        _dispatch_kernel,
        grid_spec=grid_spec,
        out_shape=jax.ShapeDtypeStruct(xs_init.shape, xs_init.dtype),
        input_output_aliases={2: 0},
        compiler_params=_cparams(("arbitrary",)),
        name="moe_dispatch",
    )(dest_flat, n, xs_init)


def _new_expert(be_ref, i):
    return (i == 0) | (be_ref[i] != be_ref[jnp.maximum(i - 1, 0)])


def _gmm1_kernel(be_ref, nu_ref, x_ref, wg_ref, wu_ref, bg_ref, bu_ref, h_ref, wg_s, wu_s):
    i = pl.program_id(1)

    @pl.when(i < nu_ref[0])
    def _():
        @pl.when(_new_expert(be_ref, i))
        def _():
            wg_s[...] = wg_ref[0].astype(BF16)
            wu_s[...] = wu_ref[0].astype(BF16)

        x = x_ref[...].astype(BF16)
        hg = jnp.dot(x, wg_s[...], preferred_element_type=F32) + bg_ref[0]
        hu = jnp.dot(x, wu_s[...], preferred_element_type=F32) + bu_ref[0]
        g = jnp.minimum(hg, SWIGLU_LIMIT)
        u = jnp.clip(hu, -SWIGLU_LIMIT, SWIGLU_LIMIT)
        h_ref[...] = (g * _sigmoid(SWIGLU_ALPHA * g) * (u + 1.0)).astype(h_ref.dtype)

    @pl.when(i >= nu_ref[0])
    def _():
        h_ref[...] = jnp.zeros_like(h_ref)


def moe_gate_up(block_e, n_used, xs, w_gu, b_gu):
    tm, tn = MOE_TM, MOE_TN
    nj = D_FF // tn
    blk = lambda i, nu: jnp.minimum(i, nu[0] - 1)
    grid_spec = pltpu.PrefetchScalarGridSpec(
        num_scalar_prefetch=2,
        grid=(nj, MOE_NB),
        in_specs=[pl.BlockSpec((tm, D_MODEL), lambda j, i, be, nu: (blk(i, nu), 0)),
                  pl.BlockSpec((1, D_MODEL, tn), lambda j, i, be, nu: (be[blk(i, nu)], 0, j)),
                  pl.BlockSpec((1, D_MODEL, tn), lambda j, i, be, nu: (be[blk(i, nu)], 0, nj + j)),
                  pl.BlockSpec((1, 1, tn), lambda j, i, be, nu: (be[blk(i, nu)], 0, j)),
                  pl.BlockSpec((1, 1, tn), lambda j, i, be, nu: (be[blk(i, nu)], 0, nj + j))],
        out_specs=pl.BlockSpec((tm, tn), lambda j, i, be, nu: (i, j)),
        scratch_shapes=[pltpu.VMEM((D_MODEL, tn), BF16), pltpu.VMEM((D_MODEL, tn), BF16)],
    )
    return pl.pallas_call(
        _gmm1_kernel,
        grid_spec=grid_spec,
        out_shape=jax.ShapeDtypeStruct((MOE_NB * tm, D_FF), BF16),
        compiler_params=_cparams(("arbitrary", "arbitrary")),
        name="moe_gate_up",
    )(block_e, n_used, xs, w_gu, w_gu, b_gu, b_gu)


def _gmm2_kernel(be_ref, nu_ref, h_ref, wd_ref, bd_ref, y_ref, wd_s):
    i = pl.program_id(1)

    @pl.when(i < nu_ref[0])
    def _():
        @pl.when(_new_expert(be_ref, i))
        def _():
            wd_s[...] = wd_ref[0].astype(BF16)

        y_ref[...] = jnp.dot(h_ref[...], wd_s[...], preferred_element_type=F32) + bd_ref[0]

    @pl.when(i >= nu_ref[0])
    def _():
        y_ref[...] = jnp.zeros_like(y_ref)


def moe_down(block_e, n_used, hidden, w_dn, b_dn):
    tm, tn = MOE_TM, MOE_TN
    nj = D_MODEL // tn
    blk = lambda i, nu: jnp.minimum(i, nu[0] - 1)
    grid_spec = pltpu.PrefetchScalarGridSpec(
        num_scalar_prefetch=2,
        grid=(nj, MOE_NB),
        in_specs=[pl.BlockSpec((tm, D_FF), lambda j, i, be, nu: (blk(i, nu), 0)),
                  pl.BlockSpec((1, D_FF, tn), lambda j, i, be, nu: (be[blk(i, nu)], 0, j)),
                  pl.BlockSpec((1, 1, tn), lambda j, i, be, nu: (be[blk(i, nu)], 0, j))],
        out_specs=pl.BlockSpec((tm, tn), lambda j, i, be, nu: (i, j)),
        scratch_shapes=[pltpu.VMEM((D_FF, tn), BF16)],
    )
    return pl.pallas_call(
        _gmm2_kernel,
        grid_spec=grid_spec,
        out_shape=jax.ShapeDtypeStruct((MOE_NB * tm, D_MODEL), F32),
        compiler_params=_cparams(("arbitrary", "arbitrary")),
        name="moe_down",
    )(block_e, n_used, hidden, w_dn, b_dn)


COMBINE_TM = 256


def _combine_kernel(src_ref, h2_ref, tg_ref, fw_ref, yb_ref, y_ref, buf, sem):
    i = pl.program_id(0)
    tm = COMBINE_TM

    def body(r, carry):
        for k in range(TOP_K):
            s = src_ref[(i * tm + r) * TOP_K + k]
            pltpu.make_async_copy(yb_ref.at[pl.ds(s, 1), :], buf.at[k, pl.ds(r, 1), :], sem).start()
        return carry

    lax.fori_loop(0, tm, body, 0)
    for k in range(TOP_K):
        pltpu.make_async_copy(yb_ref.at[pl.ds(0, tm), :], buf.at[k], sem).wait()
    tg = tg_ref[...]
    out = h2_ref[...]
    f = tg[:, 0:1] * buf[0]
    for k in range(1, TOP_K):
        f = f + tg[:, k:k + 1] * buf[k]
    out = out + f
    ms = jnp.mean(out * out, axis=-1, keepdims=True)
    y_ref[...] = out * lax.rsqrt(ms + NORM_EPS) * fw_ref[...]


def moe_combine(src_flat, h2, tg, fw, yb):
    tm = COMBINE_TM
    d = D_MODEL
    grid_spec = pltpu.PrefetchScalarGridSpec(
        num_scalar_prefetch=1,
        grid=(R // tm,),
        in_specs=[pl.BlockSpec((tm, d), lambda i, s: (i, 0)), pl.BlockSpec((tm, LANES), lambda i, s: (i, 0)),
                  pl.BlockSpec((1, d), lambda i, s: (0, 0)), pl.BlockSpec(memory_space=pl.ANY)],
        out_specs=pl.BlockSpec((tm, d), lambda i, s: (i, 0)),
        scratch_shapes=[pltpu.VMEM((TOP_K, tm, d), F32), pltpu.SemaphoreType.DMA],
    )
    return pl.pallas_call(
        _combine_kernel,
        grid_spec=grid_spec,
        out_shape=jax.ShapeDtypeStruct((R, d), F32),
        compiler_params=_cparams(("arbitrary",)),
        name="moe_combine",
    )(src_flat, h2, tg, fw, yb)


def _rope_tables():
    half = ROT_DIM // 2
    pos_p = jnp.maximum(jnp.arange(P_ROWS) - LEAD, 0)
    pos_s = SEQ + jnp.arange(S_ROWS) % DEC_SEQ
    pos = jnp.concatenate([pos_p, pos_s])
    inv_freq = 1.0 / (ROPE_THETA ** (jnp.arange(half, dtype=F32) * (2.0 / ROT_DIM)))
    ang = pos.astype(F32)[:, None] * inv_freq[None, :]
    cos, sin = jnp.cos(ang), jnp.sin(ang)
    zeros = jnp.zeros((R, LANES - ROT_DIM), F32)
    cos_t = jnp.concatenate([cos, cos, jnp.ones((R, LANES - ROT_DIM), F32)], axis=1)
    sin_a = jnp.concatenate([jnp.zeros_like(sin), sin, zeros], axis=1)
    sin_b = jnp.concatenate([-sin, jnp.zeros_like(sin), zeros], axis=1)
    return cos_t, sin_a, sin_b


def _pad_lanes(v, fill=0.0):
    v = v.reshape(1, -1)
    return jnp.concatenate([v, jnp.full((1, LANES - v.shape[1]), fill, v.dtype)], axis=1)


def kernel(x_prompt, x_sample, cache_k, cache_v, state_ssm, state_conv, page_table, meta_tokens,
           norm_mix_w, w_in, conv_w, conv_b, dt_bias, a_log, d_skip, ssd_norm_w, w_ssd_out,
           lambda_q1, lambda_k1, lambda_q2, lambda_k2, subln_w, w_attn_out, w_o, norm_ffn_w,
           router_w, router_b, w_gate_up, b_gate_up, w_down, b_down, final_norm_w):
    l = 0
    d = D_MODEL
    x_all = jnp.concatenate([jnp.zeros((LEAD, d), F32), meta_tokens, x_prompt[0], x_sample.reshape(S_ROWS, d)], axis=0)
    xn = rmsnorm_rows(x_all, norm_mix_w[l])
    w = w_in[l]
    o_z, o_xbc, o_dt, o_q, o_k, o_v, o_ga, o_gb = (int(o) for o in np.cumsum(
        (0, SSD_INNER, CONV_DIM, SSD_HEADS, 2048, 1024, 1024, d)))
    w_perm = jnp.concatenate([w[:, o_z:o_xbc], w[:, o_ga:o_gb + d], w[:, o_xbc:o_dt], w[:, o_q:o_ga]], axis=1).astype(BF16)
    w_dt = jnp.concatenate([w[:, o_dt:o_q], jnp.zeros((d, LANES - SSD_HEADS), F32)], axis=1).astype(BF16)
    cos_t, sin_a, sin_b = _rope_tables()
    proj = in_projection(xn, w_perm, cos_t, sin_a, sin_b)
    dtraw = dt_projection(xn, w_dt)

    e_mat = np.zeros((LANES, SSD_INNER), np.float32)
    for h in range(SSD_HEADS):
        e_mat[h, h * SSD_HEAD_DIM:(h + 1) * SSD_HEAD_DIM] = 1.0
    ssd_params = (conv_w[l], conv_b[l].reshape(1, CONV_DIM), _pad_lanes(dt_bias[l]), _pad_lanes(a_log[l]),
                  jnp.repeat(d_skip[l], SSD_HEAD_DIM).reshape(1, SSD_INNER), ssd_norm_w[l].reshape(1, SSD_INNER),
                  jnp.asarray(e_mat))
    yzn_p, ssm_p = ssd_prompt(proj, dtraw, ssd_params)
    conv_prev = jnp.concatenate([jnp.zeros((DEC_BATCH, SUBLANES - (SSD_CONV - 1), CONV_DIM), F32), state_conv[l]], axis=1)
    yzn_s, ssm_s = ssd_sample(proj, dtraw, conv_prev, state_ssm[l].reshape(DEC_BATCH, SSD_INNER, SSD_STATE), ssd_params)
    yzn = jnp.concatenate([yzn_p, yzn_s], axis=0)

    k_rows = proj[:, PCOL_K:PCOL_V]
    v_rows = proj[:, PCOL_V:P_COLS]
    lam_params = (lambda_q1[l].reshape(1, -1), lambda_k1[l].reshape(1, -1), lambda_q2[l].reshape(1, -1),
                  lambda_k2[l].reshape(1, -1))
    subw = subln_w[l].reshape(1, ATT_V_DIM)
    on_p = attn_prompt(proj, k_rows.astype(BF16), v_rows.astype(BF16), lam_params, subw)

    q_s = proj[P_ROWS:, PCOL_Q:PCOL_K].reshape(DEC_BATCH, DEC_SEQ, ATT_KV_HEADS, ATT_REP, 2, ATT_HEAD_DIM)
    q_s = jnp.transpose(q_s, (0, 2, 4, 1, 3, 5)).reshape(DEC_BATCH, ATT_KV_HEADS * 2, DEC_SEQ * ATT_REP, ATT_HEAD_DIM)
    eye = jnp.eye(ATT_KV_HEADS * 2, dtype=F32)
    qbd = (q_s[:, :, :, None, :] * eye[None, :, None, :, None]).reshape(DEC_BATCH, ATT_S_ROWS, ATT_KV_HEADS * 2 * ATT_HEAD_DIM)
    qbd = qbd.astype(BF16)
    n_phys = cache_k.shape[1]
    kc = cache_k[l].reshape(n_phys, PAGE_SIZE, ATT_KV_HEADS * 2 * ATT_HEAD_DIM)
    vc = cache_v[l].reshape(n_phys, PAGE_SIZE, ATT_KV_HEADS * ATT_V_DIM)
    pad_new = lambda u: jnp.concatenate([u.reshape(DEC_BATCH, DEC_SEQ, -1),
                                         jnp.zeros((DEC_BATCH, PAGE_SIZE - DEC_SEQ, u.shape[-1]), F32)], axis=1)
    o_s = attn_sample(page_table.reshape(-1), qbd, kc, vc, pad_new(k_rows[P_ROWS:]), pad_new(v_rows[P_ROWS:]),
                      lam_params, subw)
    o_s = o_s.reshape(DEC_BATCH, ATT_KV_HEADS, DEC_SEQ, ATT_REP, ATT_V_DIM)
    o_s = jnp.transpose(o_s, (0, 2, 1, 3, 4)).reshape(S_ROWS, ATT_HEADS * ATT_V_DIM).astype(BF16)
    on = lax.dynamic_update_slice(on_p, o_s, (P_ROWS, 0))

    mix = mix_branches(yzn, on, proj, w_ssd_out[l].astype(BF16), w_attn_out[l].astype(BF16))
    rw = jnp.concatenate([router_w[l], jnp.zeros((d, LANES - N_EXPERTS), F32)], axis=1)
    rb = _pad_lanes(router_b[l], NEG)
    h2, n_rows, top_i, top_g = post_mixer(mix, x_all, w_o[l].astype(BF16), norm_ffn_w[l].reshape(1, d), rw, rb)

    rank, counts = expert_ranks(top_i)
    counts = counts[0, :N_EXPERTS]
    padded = (counts + MOE_TM - 1) // MOE_TM * MOE_TM
    pad_end = jnp.cumsum(padded)
    pad_start = pad_end - padded
    dest = pad_start[top_i[:, :TOP_K]] + rank[:, :TOP_K]
    is_tok = (jnp.arange(R) >= LEAD)[:, None]
    dump = MOE_NB * MOE_TM + jnp.arange(R * TOP_K).reshape(R, TOP_K)
    dest_scatter = jnp.where(is_tok, dest, dump).astype(I32).reshape(-1)
    dest_gather = jnp.where(is_tok, dest, 0).astype(I32).reshape(-1)
    block_e = jnp.minimum(jnp.searchsorted(pad_end, jnp.arange(MOE_NB) * MOE_TM, side="right"), N_EXPERTS - 1).astype(I32)
    n_used = (pad_end[-1:] // MOE_TM).astype(I32)
    xs = moe_dispatch(dest_scatter, n_rows, jnp.zeros((MOE_ROWS, d), F32))
    hidden = moe_gate_up(block_e, n_used, xs, w_gate_up[l], b_gate_up[l].reshape(N_EXPERTS, 1, 2 * D_FF))
    yb = moe_down(block_e, n_used, hidden, w_down[l], b_down[l].reshape(N_EXPERTS, 1, d))
    y = moe_combine(dest_gather, h2, top_g, final_norm_w.reshape(1, d), yb)

    tok0 = LEAD
    y_prompt = y[LEAD + N_META:P_ROWS][None]
    y_sample = y[P_ROWS:].reshape(DEC_BATCH, DEC_SEQ, d)
    k_prompt = k_rows[tok0:P_ROWS].reshape(1, 1, SEQ + N_META, ATT_KV_HEADS, 2, ATT_HEAD_DIM)
    v_prompt = v_rows[tok0:P_ROWS].reshape(1, 1, SEQ + N_META, ATT_KV_HEADS, ATT_V_DIM)
    k_sample = k_rows[P_ROWS:].reshape(1, DEC_BATCH, DEC_SEQ, ATT_KV_HEADS, 2, ATT_HEAD_DIM)
    v_sample = v_rows[P_ROWS:].reshape(1, DEC_BATCH, DEC_SEQ, ATT_KV_HEADS, ATT_V_DIM)
    ssm_prompt = ssm_p.reshape(1, 1, SSD_HEADS, SSD_HEAD_DIM, SSD_STATE)
    ssm_sample = ssm_s.reshape(1, DEC_BATCH, SSD_HEADS, SSD_HEAD_DIM, SSD_STATE)
    xbc = proj[:, PCOL_XBC:PCOL_XBC + CONV_DIM]
    conv_prompt = xbc[P_ROWS - (SSD_CONV - 1):P_ROWS][None, None]
    xbc_s = xbc[P_ROWS:].reshape(DEC_BATCH, DEC_SEQ, CONV_DIM)
    conv_sample = jnp.concatenate([state_conv[l], xbc_s], axis=1)[:, -(SSD_CONV - 1):][None]
    return (y_prompt, y_sample, k_prompt, v_prompt, k_sample, v_sample, ssm_prompt, ssm_sample, conv_prompt, conv_sample)
```

```python
import functools
import math

import jax
import jax.numpy as jnp
import numpy as np
from jax import lax
from jax.experimental import pallas as pl
from jax.experimental.pallas import tpu as pltpu

F32 = jnp.float32
BF16 = jnp.bfloat16
I32 = jnp.int32
HI = lax.Precision.HIGHEST

D_MODEL = 2048
SEQ = 8192
DEC_BATCH = 32
DEC_SEQ = 4
PAGE_SIZE = 128
N_META = 16
NORM_EPS = 1e-5
SSD_INNER = D_MODEL
SSD_HEAD_DIM = 64
SSD_HEADS = SSD_INNER // SSD_HEAD_DIM
SSD_GROUPS = 4
SSD_REP = SSD_HEADS // SSD_GROUPS
SSD_STATE = 128
SSD_CONV = 4
SSD_CHUNK = 128
CONV_DIM = SSD_INNER + 2 * SSD_GROUPS * SSD_STATE
ATT_HEADS = 8
ATT_KV_HEADS = 4
ATT_REP = ATT_HEADS // ATT_KV_HEADS
ATT_HEAD_DIM = D_MODEL // ATT_HEADS // 2
ATT_V_DIM = 2 * ATT_HEAD_DIM
ATT_SCALE = ATT_HEAD_DIM ** -0.5
ROT_DIM = ATT_HEAD_DIM // 4
ROPE_THETA = 500000.0
N_EXPERTS = 32
TOP_K = 4
D_FF = D_MODEL
SWIGLU_LIMIT = 7.0
SWIGLU_ALPHA = 1.702
LAM_INIT = 0.8 - 0.6 * math.exp(-0.3 * 0)

LEAD = (-N_META) % SSD_CHUNK
P_ROWS = LEAD + N_META + SEQ
S_ROWS = DEC_BATCH * DEC_SEQ
R = P_ROWS + S_ROWS
N_CHUNKS = P_ROWS // SSD_CHUNK

LANES = 128
SUBLANES = 8
VMEM_LIMIT = 56 * 1024 * 1024

PCOL_Z, PCOL_GA, PCOL_GB, PCOL_XBC, PCOL_Q, PCOL_K, PCOL_V = 0, 2048, 4096, 6144, 9216, 11264, 12288
P_COLS = 13312
PROJ_TN = 1024
ROPE_BLOCKS = (PCOL_Q // PROJ_TN, PCOL_V // PROJ_TN)

MOE_TM = 256
MOE_NB = -(-(R * TOP_K) // MOE_TM) + N_EXPERTS
MOE_DUMP_BLOCKS = -(-(LEAD * TOP_K) // MOE_TM)
MOE_ROWS = (MOE_NB + MOE_DUMP_BLOCKS) * MOE_TM
MOE_TN_GATE_UP = 1024
NEG = -1e30


def _cparams(sem, vmem=VMEM_LIMIT):
    return pltpu.CompilerParams(dimension_semantics=sem, vmem_limit_bytes=vmem)


def _sigmoid(x):
    return 1.0 / (1.0 + jnp.exp(-x))


def _rmsnorm_kernel(x_ref, w_ref, o_ref):
    x = x_ref[...]
    ms = jnp.mean(x * x, axis=-1, keepdims=True)
    o_ref[...] = (x * lax.rsqrt(ms + NORM_EPS) * w_ref[...]).astype(o_ref.dtype)


def rmsnorm_rows(x, w, tm=768):
    rows, d = x.shape
    return pl.pallas_call(
        _rmsnorm_kernel,
        grid=(rows // tm,),
        in_specs=[pl.BlockSpec((tm, d), lambda i: (i, 0)), pl.BlockSpec((1, d), lambda i: (0, 0))],
        out_specs=pl.BlockSpec((tm, d), lambda i: (i, 0)),
        out_shape=jax.ShapeDtypeStruct((rows, d), BF16),
        compiler_params=_cparams(("parallel",)),
        name="rmsnorm_rows",
    )(x, w.reshape(1, d))


def _rope(blk, c, sa, sb):
    return blk * c + pltpu.roll(blk, ROT_DIM // 2, axis=1) * sa + pltpu.roll(blk, LANES - ROT_DIM // 2, axis=1) * sb


def _proj_kernel(x_ref, w_ref, c_ref, sa_ref, sb_ref, o_ref, kb_ref, vt_ref):
    j = pl.program_id(1)
    acc = jnp.dot(x_ref[...], w_ref[...], preferred_element_type=F32)
    is_rope = (j >= ROPE_BLOCKS[0]) & (j < ROPE_BLOCKS[1])

    @pl.when(is_rope)
    def _():
        c, sa, sb = c_ref[...], sa_ref[...], sb_ref[...]
        for g in range(acc.shape[1] // LANES):
            sl = slice(g * LANES, (g + 1) * LANES)
            o_ref[:, sl] = _rope(acc[:, sl], c, sa, sb)

    @pl.when(jnp.logical_not(is_rope))
    def _():
        o_ref[...] = acc

    @pl.when(j == PCOL_K // PROJ_TN)
    def _():
        kb_ref[...] = o_ref[...].astype(kb_ref.dtype)

    @pl.when(j == PCOL_V // PROJ_TN)
    def _():
        vt_ref[...] = acc.T.astype(vt_ref.dtype)


def in_projection(xn, w_bf, cos_t, sin_a, sin_b, tm=768):
    rows, d = xn.shape
    tn = PROJ_TN
    tab = pl.BlockSpec((tm, LANES), lambda i, j: (i, 0))
    return pl.pallas_call(
        _proj_kernel,
        grid=(rows // tm, P_COLS // tn),
        in_specs=[pl.BlockSpec((tm, d), lambda i, j: (i, 0)), pl.BlockSpec((d, tn), lambda i, j: (0, j)), tab, tab, tab],
        out_specs=[pl.BlockSpec((tm, tn), lambda i, j: (i, j)),
                   pl.BlockSpec((tm, tn), lambda i, j: (i, 0)),
                   pl.BlockSpec((tn, tm), lambda i, j: (0, i))],
        out_shape=[jax.ShapeDtypeStruct((rows, P_COLS), F32),
                   jax.ShapeDtypeStruct((rows, tn), BF16),
                   jax.ShapeDtypeStruct((tn, rows), BF16)],
        compiler_params=_cparams(("parallel", "arbitrary")),
        name="in_projection",
    )(xn, w_bf, cos_t, sin_a, sin_b)


def _mm_kernel(x_ref, w_ref, o_ref):
    o_ref[...] = jnp.dot(x_ref[...], w_ref[...], preferred_element_type=F32).astype(o_ref.dtype)


def dt_projection(xn, w_dt, tm=1056):
    rows, d = xn.shape
    return pl.pallas_call(
        _mm_kernel,
        grid=(rows // tm,),
        in_specs=[pl.BlockSpec((tm, d), lambda i: (i, 0)), pl.BlockSpec((d, LANES), lambda i: (0, 0))],
        out_specs=pl.BlockSpec((tm, LANES), lambda i: (i, 0)),
        out_shape=jax.ShapeDtypeStruct((rows, LANES), F32),
        compiler_params=_cparams(("parallel",)),
        name="dt_projection",
    )(xn, w_dt)


def _ssd_chunk(xprev, xcur, dtraw, z, valid, s_ref, cw_ref, cb_ref, dtb_ref, alog_ref, dsk_ref, nw_ref, e_ref,
               state_dot_precision):
    L = xcur.shape[0]
    ext = jnp.concatenate([xprev, xcur], axis=0)
    off = SUBLANES - (SSD_CONV - 1)
    acc = cb_ref[...]
    for k in range(SSD_CONV):
        acc = acc + ext[off + k:off + k + L] * cw_ref[k:k + 1, :]
    act = acc * _sigmoid(acc)
    xs = act[:, :SSD_INNER]
    nb = SSD_GROUPS * SSD_STATE
    bm = act[:, SSD_INNER:SSD_INNER + nb].astype(BF16)
    cm = act[:, SSD_INNER + nb:].astype(BF16)

    dpre = dtraw + dtb_ref[...]
    dt = jnp.maximum(dpre, 0.0) + jnp.log1p(jnp.exp(-jnp.abs(dpre)))
    dt = jnp.where(valid, dt, 0.0)
    da = dt * (-jnp.exp(alog_ref[...]))

    li = lax.broadcasted_iota(I32, (L, L), 0)
    si = lax.broadcasted_iota(I32, (L, L), 1)
    tril = li >= si
    eye = li == si
    a_cs = jnp.dot(tril.astype(F32), da, precision=HI, preferred_element_type=F32)
    a_tot = a_cs[L - 1:L, :]
    e_mat = e_ref[...]
    dt_x = jnp.dot(dt, e_mat, precision=HI, preferred_element_type=F32)
    eacs_x = jnp.dot(jnp.exp(a_cs), e_mat, precision=HI, preferred_element_type=F32)
    dte_x = jnp.dot(jnp.exp(a_tot - a_cs), e_mat, precision=HI, preferred_element_type=F32)
    etot = jnp.exp(a_tot)

    xdt = xs * dt_x
    xdt_b = xdt.astype(BF16)
    xde = xdt * dte_x
    gw = SSD_REP * SSD_HEAD_DIM
    lane = lax.broadcasted_iota(I32, (L, LANES), 1)
    nt = (((1,), (1,)), ((), ()))
    tn = (((0,), (0,)), ((), ()))
    y_groups = []
    for g in range(SSD_GROUPS):
        bg = bm[:, g * SSD_STATE:(g + 1) * SSD_STATE]
        cg = cm[:, g * SSD_STATE:(g + 1) * SSD_STATE]
        cb = lax.dot_general(cg, bg, nt, preferred_element_type=F32)
        s_g = s_ref[g * gw:(g + 1) * gw, :]
        y_off = lax.dot_general(cg, s_g.astype(BF16), nt, preferred_element_type=F32)
        if state_dot_precision is None:
            upd = lax.dot_general(xde[:, g * gw:(g + 1) * gw].astype(BF16), bg, tn, preferred_element_type=F32)
        else:
            upd = lax.dot_general(xde[:, g * gw:(g + 1) * gw], act[:, SSD_INNER + g * SSD_STATE:SSD_INNER + (g + 1) * SSD_STATE],
                                  tn, precision=state_dot_precision, preferred_element_type=F32)
        pairs = []
        decs = []
        for q in range(SSD_REP // 2):
            yd = []
            for r in range(2):
                h = g * SSD_REP + 2 * q + r
                colb = jnp.broadcast_to(a_cs[:, h:h + 1], (L, L))
                rowb = jnp.sum(jnp.where(eye, colb, 0.0), axis=0, keepdims=True)
                lm = jnp.where(tril, jnp.exp(jnp.minimum(colb - rowb, 0.0)), 0.0)
                m = (cb * lm).astype(BF16)
                c0 = (g * SSD_REP + 2 * q) * SSD_HEAD_DIM
                yd.append(jnp.dot(m, xdt_b[:, c0:c0 + LANES], preferred_element_type=F32))
                decs.append(jnp.broadcast_to(etot[:, h:h + 1], (SSD_HEAD_DIM, SSD_STATE)))
            pairs.append(jnp.where(lane < SSD_HEAD_DIM, yd[0], yd[1]))
        y_g = jnp.concatenate(pairs, axis=1) + y_off * eacs_x[:, g * gw:(g + 1) * gw]
        y_groups.append(y_g)
        s_ref[g * gw:(g + 1) * gw, :] = jnp.concatenate(decs, axis=0) * s_g + upd
    y = jnp.concatenate(y_groups, axis=1) + dsk_ref[...] * xs
    yz = y * (z * _sigmoid(z))
    outs = []
    for g in range(SSD_GROUPS):
        yg = yz[:, g * gw:(g + 1) * gw]
        ms = jnp.mean(yg * yg, axis=-1, keepdims=True)
        outs.append(yg * lax.rsqrt(ms + NORM_EPS) * nw_ref[:, g * gw:(g + 1) * gw])
    return jnp.concatenate(outs, axis=1)


def _ssd_prompt_kernel(xprev_ref, xcur_ref, dt_ref, z_ref, cw_ref, cb_ref, dtb_ref, alog_ref, dsk_ref, nw_ref, e_ref,
                       y_ref, s_ref):
    c = pl.program_id(0)

    @pl.when(c == 0)
    def _():
        s_ref[...] = jnp.zeros_like(s_ref)

    xprev = jnp.where(c == 0, 0.0, xprev_ref[...])
    row = c * SSD_CHUNK + lax.broadcasted_iota(I32, (SSD_CHUNK, 1), 0)
    y = _ssd_chunk(xprev, xcur_ref[...], dt_ref[...], z_ref[...], row >= LEAD, s_ref,
                   cw_ref, cb_ref, dtb_ref, alog_ref, dsk_ref, nw_ref, e_ref, None)
    y_ref[...] = y.astype(y_ref.dtype)


def _full(shape):
    return pl.BlockSpec(shape, lambda *_: (0,) * len(shape))


def ssd_prompt(proj, dtraw, params):
    L = SSD_CHUNK
    xb = PCOL_XBC // CONV_DIM
    in_specs = [
        pl.BlockSpec((SUBLANES, CONV_DIM), lambda c: (jnp.maximum(c * (L // SUBLANES) - 1, 0), xb)),
        pl.BlockSpec((L, CONV_DIM), lambda c: (c, xb)),
        pl.BlockSpec((L, LANES), lambda c: (c, 0)),
        pl.BlockSpec((L, SSD_INNER), lambda c: (c, PCOL_Z // SSD_INNER)),
    ] + [_full(p.shape) for p in params]
    return pl.pallas_call(
        _ssd_prompt_kernel,
        grid=(N_CHUNKS,),
        in_specs=in_specs,
        out_specs=[pl.BlockSpec((L, SSD_INNER), lambda c: (c, 0)), _full((SSD_INNER, SSD_STATE))],
        out_shape=[jax.ShapeDtypeStruct((P_ROWS, SSD_INNER), BF16), jax.ShapeDtypeStruct((SSD_INNER, SSD_STATE), F32)],
        compiler_params=_cparams(("arbitrary",)),
        name="ssd_prompt",
    )(proj, proj, dtraw, proj, *params)


SSD_S_NB = 4


def _ssd_sample_kernel(xprev_ref, xcur_ref, dt_ref, z_ref, sin_ref, cw_ref, cb_ref, dtb_ref, alog_ref, dsk_ref, nw_ref,
                       e_ref, y_ref, sout_ref):
    L = SUBLANES
    valid = lax.broadcasted_iota(I32, (L, 1), 0) < DEC_SEQ
    ys = []
    for b in range(SSD_S_NB):
        rows = slice(b * DEC_SEQ, (b + 1) * DEC_SEQ)
        pad = lambda u: jnp.concatenate([u, jnp.zeros((L - DEC_SEQ, u.shape[1]), u.dtype)], axis=0)
        sout_ref[b] = sin_ref[b]
        y = _ssd_chunk(xprev_ref[b], pad(xcur_ref[rows, :]), pad(dt_ref[rows, :]), pad(z_ref[rows, :]), valid,
                       sout_ref.at[b], cw_ref, cb_ref, dtb_ref, alog_ref, dsk_ref, nw_ref, e_ref, HI)
        ys.append(y[:DEC_SEQ])
    y_ref[...] = jnp.concatenate(ys, axis=0).astype(y_ref.dtype)


def ssd_sample(proj, dtraw, conv_prev, state, params):
    nb = SSD_S_NB
    rb = nb * DEC_SEQ
    r0 = P_ROWS // rb
    in_specs = [
        pl.BlockSpec((nb, SUBLANES, CONV_DIM), lambda i: (i, 0, 0)),
        pl.BlockSpec((rb, CONV_DIM), lambda i: (r0 + i, PCOL_XBC // CONV_DIM)),
        pl.BlockSpec((rb, LANES), lambda i: (r0 + i, 0)),
        pl.BlockSpec((rb, SSD_INNER), lambda i: (r0 + i, PCOL_Z // SSD_INNER)),
        pl.BlockSpec((nb, SSD_INNER, SSD_STATE), lambda i: (i, 0, 0)),
    ] + [_full(p.shape) for p in params]
    return pl.pallas_call(
        _ssd_sample_kernel,
        grid=(DEC_BATCH // nb,),
        in_specs=in_specs,
        out_specs=[pl.BlockSpec((rb, SSD_INNER), lambda i: (i, 0)),
                   pl.BlockSpec((nb, SSD_INNER, SSD_STATE), lambda i: (i, 0, 0))],
        out_shape=[jax.ShapeDtypeStruct((S_ROWS, SSD_INNER), BF16),
                   jax.ShapeDtypeStruct((DEC_BATCH, SSD_INNER, SSD_STATE), F32)],
        compiler_params=_cparams(("parallel",)),
        name="ssd_sample",
    )(conv_prev, proj, dtraw, proj, state, *params)


def _lambda(lq1_ref, lk1_ref, lq2_ref, lk2_ref):
    s1 = jnp.sum(lq1_ref[...] * lk1_ref[...], axis=-1, keepdims=True)
    s2 = jnp.sum(lq2_ref[...] * lk2_ref[...], axis=-1, keepdims=True)
    return jnp.exp(s1) - jnp.exp(s2) + LAM_INIT


def _subnorm(o, subw_ref):
    ms = jnp.mean(o * o, axis=-1, keepdims=True)
    return (o * lax.rsqrt(ms + NORM_EPS) * subw_ref[...]) * (1.0 - LAM_INIT)


ATT_TQ = 256
ATT_TK = 768
BF16_SUBLANES = 16
ATT_VT_ROWS = ATT_V_DIM + BF16_SUBLANES
EXP2_SCALE = ATT_SCALE * math.log2(math.e)


def _attn_prompt_kernel(q_ref, k_ref, vt_ref, lq1_ref, lk1_ref, lq2_ref, lk2_ref, subw_ref, o_ref, m_s, acc_s):
    i = pl.program_id(1)
    tq, tk, hd = ATT_TQ, ATT_TK, ATT_HEAD_DIM
    q = q_ref[...]
    qc = [jnp.concatenate([q[:, r * 2 * hd + c * hd: r * 2 * hd + (c + 1) * hd] for r in range(ATT_REP)], axis=0).astype(BF16)
          for c in range(2)]
    m_s[...] = jnp.full_like(m_s, NEG)
    acc_s[...] = jnp.zeros_like(acc_s)
    nt = (((1,), (1,)), ((), ()))
    n_blocks = (i * tq + tq - 1) // tk + 1

    def step(j, mask):
        k0 = pl.multiple_of(j * tk, tk)
        kblk = k_ref[pl.ds(k0, tk), :]
        vt = vt_ref[:, pl.ds(k0, tk)]
        for c in range(2):
            st = lax.dot_general(kblk[:, c * hd:(c + 1) * hd], qc[c], nt, preferred_element_type=F32)
            if mask == "causal":
                kidx = j * tk + lax.broadcasted_iota(I32, st.shape, 0)
                qidx = i * tq + lax.broadcasted_iota(I32, st.shape, 1) % tq
                ok = (kidx <= qidx) & ((kidx >= LEAD) | (qidx < LEAD))
                st = jnp.where(ok, st, NEG)
            elif mask == "lead":
                st = st + jnp.where(lax.broadcasted_iota(I32, (tk, 1), 0) < LEAD, NEG, 0.0)
            m_old = m_s[c]
            m_new = jnp.maximum(m_old, jnp.max(st, axis=0, keepdims=True))
            alpha = jnp.exp2((m_old - m_new) * EXP2_SCALE)
            p = jnp.exp2((st - m_new) * EXP2_SCALE)
            acc_s[c] = alpha * acc_s[c] + jnp.dot(vt, p.astype(BF16), preferred_element_type=F32)
            m_s[c] = m_new

    @pl.when(n_blocks == 1)
    def _():
        step(0, "causal")

    @pl.when(n_blocks > 1)
    def _():
        step(0, "lead")

        def body(j, carry):
            step(j, None)
            return carry

        lax.fori_loop(1, n_blocks - 1, body, 0)
        step(n_blocks - 1, "causal")

    lam = _lambda(lq1_ref, lk1_ref, lq2_ref, lk2_ref)
    for r in range(ATT_REP):
        cols = slice(r * tq, (r + 1) * tq)
        o0 = acc_s[0, :ATT_V_DIM, cols] / acc_s[0, ATT_V_DIM:ATT_V_DIM + 1, cols]
        o1 = acc_s[1, :ATT_V_DIM, cols] / acc_s[1, ATT_V_DIM:ATT_V_DIM + 1, cols]
        ot = o0 - lam * o1
        ms = jnp.mean(ot * ot, axis=0, keepdims=True)
        ot = (ot * lax.rsqrt(ms + NORM_EPS) * subw_ref[...]) * (1.0 - LAM_INIT)
        o_ref[:, r * ATT_V_DIM:(r + 1) * ATT_V_DIM] = ot.T.astype(o_ref.dtype)


def attn_prompt(proj, kb, vt_ext, lam_params, subw_col):
    tq = ATT_TQ
    gq = ATT_REP * 2 * ATT_HEAD_DIM
    vec = _full((1, ATT_HEAD_DIM))
    return pl.pallas_call(
        _attn_prompt_kernel,
        grid=(ATT_KV_HEADS, R // tq),
        in_specs=[pl.BlockSpec((tq, gq), lambda g, i: (i, PCOL_Q // gq + g)),
                  pl.BlockSpec((R, ATT_V_DIM), lambda g, i: (0, g)),
                  pl.BlockSpec((ATT_VT_ROWS, R), lambda g, i: (g, 0)),
                  vec, vec, vec, vec, _full((ATT_V_DIM, 1))],
        out_specs=pl.BlockSpec((tq, gq), lambda g, i: (i, g)),
        out_shape=jax.ShapeDtypeStruct((R, ATT_HEADS * ATT_V_DIM), BF16),
        scratch_shapes=[pltpu.VMEM((2, 1, ATT_REP * tq), F32),
                        pltpu.VMEM((2, ATT_VT_ROWS, ATT_REP * tq), F32)],
        compiler_params=_cparams(("parallel", "arbitrary")),
        name="attn_prompt",
    )(proj, kb, vt_ext, *lam_params, subw_col)


N_PAGES = SEQ // PAGE_SIZE
ATT_S_ROWS = ATT_KV_HEADS * 2 * DEC_SEQ * ATT_REP
ATT_S_GROUP = 2 * DEC_SEQ * ATT_REP
ATT_S_PPS = 8
ATT_S_STEPS = N_PAGES // ATT_S_PPS
KROWS = PAGE_SIZE * ATT_KV_HEADS * 2
V_LANE_TILES = ATT_V_DIM // LANES


def _attn_sample_kernel(pt_ref, q_ref, *refs):
    npg = ATT_S_PPS
    kc_refs = refs[:npg]
    vc_refs = [refs[npg + V_LANE_TILES * p: npg + V_LANE_TILES * (p + 1)] for p in range(npg)]
    base = npg * (1 + V_LANE_TILES)
    kn_ref = refs[base]
    vn_refs = refs[base + 1: base + 1 + V_LANE_TILES]
    x_ref, lq1_ref, lk1_ref, lq2_ref, lk2_ref, subw_ref, o_ref, m_s, l_s, acc_s = refs[base + 1 + V_LANE_TILES:]
    j = pl.program_id(1)
    nt = (((1,), (1,)), ((), ()))
    nhc = ATT_KV_HEADS * 2
    rows_hc = DEC_SEQ * ATT_REP

    @pl.when(j == 0)
    def _():
        m_s[...] = jnp.full_like(m_s, NEG)
        l_s[...] = jnp.zeros_like(l_s)
        acc_s[...] = jnp.zeros_like(acc_s)

    def step(k_pages, v_pages, masked):
        q = q_ref[0]
        cols = []
        for k_src in k_pages:
            ss = []
            for hc in range(nhc):
                khc = k_src[pl.ds(hc, PAGE_SIZE, stride=nhc), :].astype(BF16)
                ss.append(lax.dot_general(q[hc * rows_hc:(hc + 1) * rows_hc, :], khc, nt, preferred_element_type=F32))
            cols.append(jnp.concatenate(ss, axis=0))
        s = jnp.concatenate(cols, axis=1) if len(cols) > 1 else cols[0]
        if masked:
            t_q = (lax.broadcasted_iota(I32, s.shape, 0) % rows_hc) // ATT_REP
            t_k = lax.broadcasted_iota(I32, s.shape, 1)
            s = jnp.where(t_k <= t_q, s, NEG)
        m_old = m_s[...]
        m_new = jnp.maximum(m_old, jnp.max(s, axis=1, keepdims=True))
        alpha = jnp.exp2((m_old - m_new) * EXP2_SCALE)
        p = jnp.exp2((s - m_new[:, :1]) * EXP2_SCALE)
        l_s[...] = alpha * l_s[...] + jnp.sum(p, axis=1, keepdims=True)
        pb = p.astype(BF16)
        head_of_col = lax.broadcasted_iota(I32, (ATT_S_ROWS, PAGE_SIZE * ATT_KV_HEADS), 1) % ATT_KV_HEADS
        head_of_row = lax.broadcasted_iota(I32, (ATT_S_ROWS, PAGE_SIZE * ATT_KV_HEADS), 0) // ATT_S_GROUP
        own_head = head_of_col == head_of_row
        pv = None
        for pi, v_tiles in enumerate(v_pages):
            v2 = jnp.concatenate([v[...].reshape(PAGE_SIZE * ATT_KV_HEADS, LANES) for v in v_tiles], axis=1).astype(BF16)
            spread = jnp.dot(pb[:, pi * PAGE_SIZE:(pi + 1) * PAGE_SIZE], x_ref[...], preferred_element_type=F32)
            pe = jnp.where(own_head, spread, 0.0).astype(BF16)
            d = jnp.dot(pe, v2, preferred_element_type=F32)
            pv = d if pv is None else pv + d
        acc_s[...] = alpha[:, :1] * acc_s[...] + pv
        m_s[...] = m_new

    @pl.when(j < ATT_S_STEPS)
    def _():
        step(kc_refs, vc_refs, False)

    @pl.when(j == ATT_S_STEPS)
    def _():
        step([kn_ref], [vn_refs], True)
        lam = _lambda(lq1_ref, lk1_ref, lq2_ref, lk2_ref)
        for g in range(ATT_KV_HEADS):
            r0 = g * ATT_S_GROUP
            o0 = acc_s[r0:r0 + rows_hc, :] / l_s[r0:r0 + rows_hc, :1]
            o1 = acc_s[r0 + rows_hc:r0 + 2 * rows_hc, :] / l_s[r0 + rows_hc:r0 + 2 * rows_hc, :1]
            o_ref[0, g] = _subnorm(o0 - lam * o1, subw_ref)


def attn_sample(page_table, q, kc, vc, knew, vnew, lam_params, subw):
    vec = _full((1, ATT_HEAD_DIM))
    kblk = (None, KROWS, ATT_HEAD_DIM)
    vblk = (None, None, PAGE_SIZE, ATT_KV_HEADS, LANES)

    def page(b, j, pt, p):
        return pt[b * N_PAGES + jnp.minimum(j, ATT_S_STEPS - 1) * ATT_S_PPS + p]

    k_specs = [pl.BlockSpec(kblk, functools.partial(lambda b, j, pt, p: (page(b, j, pt, p), 0, 0), p=p))
               for p in range(ATT_S_PPS)]
    v_specs = [pl.BlockSpec(vblk, functools.partial(lambda b, j, pt, p, e: (0, page(b, j, pt, p), 0, 0, e), p=p, e=e))
               for p in range(ATT_S_PPS) for e in range(V_LANE_TILES)]
    spread = np.zeros((PAGE_SIZE, PAGE_SIZE * ATT_KV_HEADS), np.float32)
    for t in range(PAGE_SIZE):
        spread[t, t * ATT_KV_HEADS:(t + 1) * ATT_KV_HEADS] = 1.0
    kn_spec = pl.BlockSpec(kblk, lambda b, j, pt: (b, 0, 0))
    vn_specs = [pl.BlockSpec(vblk, functools.partial(lambda b, j, pt, e: (0, b, 0, 0, e), e=e)) for e in range(V_LANE_TILES)]
    grid_spec = pltpu.PrefetchScalarGridSpec(
        num_scalar_prefetch=1,
        grid=(DEC_BATCH, ATT_S_STEPS + 1),
        in_specs=[pl.BlockSpec((1, ATT_S_ROWS, ATT_HEAD_DIM), lambda b, j, pt: (b, 0, 0)),
                  *k_specs, *v_specs, kn_spec, *vn_specs, _full((PAGE_SIZE, PAGE_SIZE * ATT_KV_HEADS)),
                  vec, vec, vec, vec, _full((1, ATT_V_DIM))],
        out_specs=pl.BlockSpec((1, ATT_KV_HEADS, DEC_SEQ * ATT_REP, ATT_V_DIM), lambda b, j, pt: (b, 0, 0, 0)),
        scratch_shapes=[pltpu.VMEM((ATT_S_ROWS, LANES), F32), pltpu.VMEM((ATT_S_ROWS, LANES), F32),
                        pltpu.VMEM((ATT_S_ROWS, ATT_V_DIM), F32)],
    )
    return pl.pallas_call(
        _attn_sample_kernel,
        grid_spec=grid_spec,
        out_shape=jax.ShapeDtypeStruct((DEC_BATCH, ATT_KV_HEADS, DEC_SEQ * ATT_REP, ATT_V_DIM), F32),
        compiler_params=_cparams(("parallel", "arbitrary")),
        name="attn_sample",
    )(page_table, q, *([kc] * ATT_S_PPS), *([vc] * (ATT_S_PPS * V_LANE_TILES)), knew, *([vnew] * V_LANE_TILES),
      jnp.asarray(spread, BF16), *lam_params, subw)


def _mix_kernel(ya_ref, yb_ref, ga_ref, gb_ref, wa_ref, wb_ref, o_ref):
    oa = jnp.dot(ya_ref[...], wa_ref[...], preferred_element_type=F32)
    ob = jnp.dot(yb_ref[...], wb_ref[...], preferred_element_type=F32)
    o_ref[...] = (_sigmoid(ga_ref[...]) * oa + _sigmoid(gb_ref[...]) * ob).astype(o_ref.dtype)


def mix_branches(yzn, on, proj, wa, wb, tm=256):
    d = D_MODEL
    row = lambda i: (i, 0)
    return pl.pallas_call(
        _mix_kernel,
        grid=(R // tm,),
        in_specs=[pl.BlockSpec((tm, d), row), pl.BlockSpec((tm, d), row),
                  pl.BlockSpec((tm, d), lambda i: (i, PCOL_GA // d)), pl.BlockSpec((tm, d), lambda i: (i, PCOL_GB // d)),
                  _full((d, d)), _full((d, d))],
        out_specs=pl.BlockSpec((tm, d), row),
        out_shape=jax.ShapeDtypeStruct((R, d), BF16),
        compiler_params=_cparams(("parallel",)),
        name="mix_branches",
    )(yzn, on, proj, proj, wa, wb)


def _post_kernel(mix_ref, h_ref, wo_ref, nw_ref, rw_ref, rb_ref, h2_ref, n_ref, ti_ref, tg_ref):
    h2 = h_ref[...] + jnp.dot(mix_ref[...], wo_ref[...], preferred_element_type=F32)
    h2_ref[...] = h2
    ms = jnp.mean(h2 * h2, axis=-1, keepdims=True)
    n = h2 * lax.rsqrt(ms + NORM_EPS) * nw_ref[...]
    n_ref[...] = n
    logits = jnp.dot(n, rw_ref[...], precision=HI, preferred_element_type=F32) + rb_ref[...]
    lane = lax.broadcasted_iota(I32, logits.shape, 1)
    work = logits
    vals, idxs = [], []
    for _ in range(TOP_K):
        mx = jnp.max(work, axis=1, keepdims=True)
        am = jnp.min(jnp.where(work == mx, lane, LANES), axis=1, keepdims=True)
        vals.append(mx)
        idxs.append(am)
        work = jnp.where(lane == am, -jnp.inf, work)
    es = [jnp.exp(v - vals[0]) for v in vals]
    den = es[0] + es[1] + es[2] + es[3]
    ti = jnp.zeros(logits.shape, I32)
    tg = jnp.zeros(logits.shape, F32)
    for k in range(TOP_K):
        ti = jnp.where(lane == k, idxs[k], ti)
        tg = jnp.where(lane == k, es[k] / den, tg)
    ti_ref[...] = ti
    tg_ref[...] = tg


def post_mixer(mix, h, wo, nw, rw, rb, tm=256):
    d = D_MODEL
    row = lambda i: (i, 0)
    return pl.pallas_call(
        _post_kernel,
        grid=(R // tm,),
        in_specs=[pl.BlockSpec((tm, d), row), pl.BlockSpec((tm, d), row), _full((d, d)), _full((1, d)),
                  _full((d, LANES)), _full((1, LANES))],
        out_specs=[pl.BlockSpec((tm, d), row), pl.BlockSpec((tm, d), row),
                   pl.BlockSpec((tm, LANES), row), pl.BlockSpec((tm, LANES), row)],
        out_shape=[jax.ShapeDtypeStruct((R, d), F32), jax.ShapeDtypeStruct((R, d), F32),
                   jax.ShapeDtypeStruct((R, LANES), I32), jax.ShapeDtypeStruct((R, LANES), F32)],
        compiler_params=_cparams(("parallel",)),
        name="post_mixer",
    )(mix, h, wo, nw, rw, rb)


def _rank_kernel(ti_ref, rank_ref, cnt_ref, carry_s):
    i = pl.program_id(0)
    tm = ti_ref.shape[0]

    @pl.when(i == 0)
    def _():
        carry_s[...] = jnp.zeros_like(carry_s)

    ti = ti_ref[...]
    lane = lax.broadcasted_iota(I32, ti.shape, 1)
    row = i * tm + lax.broadcasted_iota(I32, (tm, 1), 0)
    valid = row >= LEAD
    ohs = [jnp.where((lane == ti[:, k:k + 1]) & valid, 1.0, 0.0) for k in range(TOP_K)]
    osum = ohs[0] + ohs[1] + ohs[2] + ohs[3]
    li = lax.broadcasted_iota(I32, (tm, tm), 0)
    si = lax.broadcasted_iota(I32, (tm, tm), 1)
    before = jnp.dot((li > si).astype(BF16), osum.astype(BF16), preferred_element_type=F32) + carry_s[...]
    rank = jnp.zeros(ti.shape, I32)
    for k in range(TOP_K):
        rk = jnp.sum(ohs[k] * before, axis=1, keepdims=True)
        rank = jnp.where(lane == k, rk.astype(I32), rank)
    rank_ref[...] = rank
    carry_s[...] = carry_s[...] + jnp.sum(osum, axis=0, keepdims=True)
    cnt_ref[...] = carry_s[...].astype(I32)


def expert_ranks(ti, tm=256):
    return pl.pallas_call(
        _rank_kernel,
        grid=(R // tm,),
        in_specs=[pl.BlockSpec((tm, LANES), lambda i: (i, 0))],
        out_specs=[pl.BlockSpec((tm, LANES), lambda i: (i, 0)), _full((1, LANES))],
        out_shape=[jax.ShapeDtypeStruct((R, LANES), I32), jax.ShapeDtypeStruct((1, LANES), I32)],
        scratch_shapes=[pltpu.VMEM((1, LANES), F32)],
        compiler_params=_cparams(("arbitrary",)),
        name="expert_ranks",
    )(ti)


DISPATCH_TM = 256


def _dispatch_kernel(dest_ref, n_ref, xs_in_ref, xs_ref, sem):
    del xs_in_ref
    i = pl.program_id(0)
    tm = DISPATCH_TM

    def body(r, carry):
        for k in range(TOP_K):
            d = dest_ref[(i * tm + r) * TOP_K + k]
            pltpu.make_async_copy(n_ref.at[pl.ds(r, 1), :], xs_ref.at[pl.ds(d, 1), :], sem).start()
        return carry

    lax.fori_loop(0, tm, body, 0)
    for k in range(TOP_K):
        pltpu.make_async_copy(n_ref, xs_ref.at[pl.ds(0, tm), :], sem).wait()


def moe_dispatch(dest_flat, n, xs_init):
    tm = DISPATCH_TM
    grid_spec = pltpu.PrefetchScalarGridSpec(
        num_scalar_prefetch=1,
        grid=(R // tm,),
        in_specs=[pl.BlockSpec((tm, n.shape[1]), lambda i, d: (i, 0)), pl.BlockSpec(memory_space=pl.ANY)],
        out_specs=pl.BlockSpec(memory_space=pl.ANY),
        scratch_shapes=[pltpu.SemaphoreType.DMA],
    )
    return pl.pallas_call(
        _dispatch_kernel,
        grid_spec=grid_spec,
        out_shape=jax.ShapeDtypeStruct(xs_init.shape, xs_init.dtype),
        input_output_aliases={2: 0},
        compiler_params=_cparams(("arbitrary",)),
        name="moe_dispatch",
    )(dest_flat, n, xs_init)


def _new_expert(be_ref, i):
    return (i == 0) | (be_ref[i] != be_ref[jnp.maximum(i - 1, 0)])


def _stream_expert_weights(i, be_ref, run_ref, nxt_ref, copies, wbuf, wbf):
    slot = run_ref[i] % 2

    @pl.when(i == 0)
    def _():
        for c in copies(be_ref[0], 0):
            c.start()

    @pl.when(_new_expert(be_ref, i))
    def _():
        for c in copies(be_ref[i], slot):
            c.wait()

        @pl.when(nxt_ref[i] >= 0)
        def _():
            for c in copies(nxt_ref[i], 1 - slot):
                c.start()

        for t in range(wbf.shape[0]):
            wbf[t] = wbuf[slot, t].astype(BF16)


def _gmm1_kernel(be_ref, nu_ref, run_ref, nxt_ref, x_ref, w_hbm, bg_ref, bu_ref, h_ref, wbuf, wbf, sem):
    j = pl.program_id(0)
    i = pl.program_id(1)
    tn = MOE_TN_GATE_UP

    def copies(e, slot):
        cols = (pl.multiple_of(j * tn, tn), pl.multiple_of(D_FF + j * tn, tn))
        return [pltpu.make_async_copy(w_hbm.at[e, :, pl.ds(c0, tn)], wbuf.at[slot, t], sem.at[slot, t])
                for t, c0 in enumerate(cols)]

    @pl.when(i < nu_ref[0])
    def _():
        _stream_expert_weights(i, be_ref, run_ref, nxt_ref, copies, wbuf, wbf)
        x = x_ref[...].astype(BF16)
        hg = jnp.dot(x, wbf[0], preferred_element_type=F32) + bg_ref[0]
        hu = jnp.dot(x, wbf[1], preferred_element_type=F32) + bu_ref[0]
        g = jnp.minimum(hg, SWIGLU_LIMIT)
        u = jnp.clip(hu, -SWIGLU_LIMIT, SWIGLU_LIMIT)
        h_ref[...] = (g * _sigmoid(SWIGLU_ALPHA * g) * (u + 1.0)).astype(h_ref.dtype)

    @pl.when(i >= nu_ref[0])
    def _():
        h_ref[...] = jnp.zeros_like(h_ref)


def moe_gate_up(sched, xs, w_gu, b_gu):
    tm, tn = MOE_TM, MOE_TN_GATE_UP
    nj = D_FF // tn
    blk = lambda i, nu: jnp.minimum(i, nu[0] - 1)
    grid_spec = pltpu.PrefetchScalarGridSpec(
        num_scalar_prefetch=4,
        grid=(nj, MOE_NB),
        in_specs=[pl.BlockSpec((tm, D_MODEL), lambda j, i, be, nu, rn, nx: (blk(i, nu), 0)),
                  pl.BlockSpec(memory_space=pl.ANY),
                  pl.BlockSpec((1, 1, tn), lambda j, i, be, nu, rn, nx: (be[blk(i, nu)], 0, j)),
                  pl.BlockSpec((1, 1, tn), lambda j, i, be, nu, rn, nx: (be[blk(i, nu)], 0, nj + j))],
        out_specs=pl.BlockSpec((tm, tn), lambda j, i, be, nu, rn, nx: (i, j)),
        scratch_shapes=[pltpu.VMEM((2, 2, D_MODEL, tn), F32), pltpu.VMEM((2, D_MODEL, tn), BF16),
                        pltpu.SemaphoreType.DMA((2, 2))],
    )
    return pl.pallas_call(
        _gmm1_kernel,
        grid_spec=grid_spec,
        out_shape=jax.ShapeDtypeStruct((MOE_NB * tm, D_FF), BF16),
        compiler_params=_cparams(("arbitrary", "arbitrary")),
        name="moe_gate_up",
    )(*sched, xs, w_gu, b_gu, b_gu)


def _gmm2_kernel(be_ref, nu_ref, run_ref, nxt_ref, h_ref, w_hbm, bd_ref, y_ref, wbuf, wbf, sem):
    i = pl.program_id(0)

    def copies(e, slot):
        return [pltpu.make_async_copy(w_hbm.at[e], wbuf.at[slot, 0], sem.at[slot])]

    @pl.when(i < nu_ref[0])
    def _():
        _stream_expert_weights(i, be_ref, run_ref, nxt_ref, copies, wbuf, wbf)
        y_ref[...] = jnp.dot(h_ref[...], wbf[0], preferred_element_type=F32) + bd_ref[0]

    @pl.when(i >= nu_ref[0])
    def _():
        y_ref[...] = jnp.zeros_like(y_ref)


def moe_down(sched, hidden, w_dn, b_dn):
    tm = MOE_TM
    blk = lambda i, nu: jnp.minimum(i, nu[0] - 1)
    grid_spec = pltpu.PrefetchScalarGridSpec(
        num_scalar_prefetch=4,
        grid=(MOE_NB,),
        in_specs=[pl.BlockSpec((tm, D_FF), lambda i, be, nu, rn, nx: (blk(i, nu), 0)),
                  pl.BlockSpec(memory_space=pl.ANY),
                  pl.BlockSpec((1, 1, D_MODEL), lambda i, be, nu, rn, nx: (be[blk(i, nu)], 0, 0))],
        out_specs=pl.BlockSpec((tm, D_MODEL), lambda i, be, nu, rn, nx: (i, 0)),
        scratch_shapes=[pltpu.VMEM((2, 1, D_FF, D_MODEL), F32), pltpu.VMEM((1, D_FF, D_MODEL), BF16),
                        pltpu.SemaphoreType.DMA((2,))],
    )
    return pl.pallas_call(
        _gmm2_kernel,
        grid_spec=grid_spec,
        out_shape=jax.ShapeDtypeStruct((MOE_NB * tm, D_MODEL), F32),
        compiler_params=_cparams(("arbitrary",)),
        name="moe_down",
    )(*sched, hidden, w_dn, b_dn)


COMBINE_TM = 256


def _combine_kernel(src_ref, h2_ref, tg_ref, fw_ref, yb_ref, y_ref, buf, sem):
    i = pl.program_id(0)
    tm = COMBINE_TM

    def body(r, carry):
        for k in range(TOP_K):
            s = src_ref[(i * tm + r) * TOP_K + k]
            pltpu.make_async_copy(yb_ref.at[pl.ds(s, 1), :], buf.at[k, pl.ds(r, 1), :], sem).start()
        return carry

    lax.fori_loop(0, tm, body, 0)
    for k in range(TOP_K):
        pltpu.make_async_copy(yb_ref.at[pl.ds(0, tm), :], buf.at[k], sem).wait()
    tg = tg_ref[...]
    out = h2_ref[...]
    f = tg[:, 0:1] * buf[0]
    for k in range(1, TOP_K):
        f = f + tg[:, k:k + 1] * buf[k]
    out = out + f
    ms = jnp.mean(out * out, axis=-1, keepdims=True)
    y_ref[...] = out * lax.rsqrt(ms + NORM_EPS) * fw_ref[...]


def moe_combine(src_flat, h2, tg, fw, yb):
    tm = COMBINE_TM
    d = D_MODEL
    grid_spec = pltpu.PrefetchScalarGridSpec(
        num_scalar_prefetch=1,
        grid=(R // tm,),
        in_specs=[pl.BlockSpec((tm, d), lambda i, s: (i, 0)), pl.BlockSpec((tm, LANES), lambda i, s: (i, 0)),
                  pl.BlockSpec((1, d), lambda i, s: (0, 0)), pl.BlockSpec(memory_space=pl.ANY)],
        out_specs=pl.BlockSpec((tm, d), lambda i, s: (i, 0)),
        scratch_shapes=[pltpu.VMEM((TOP_K, tm, d), F32), pltpu.SemaphoreType.DMA],
    )
    return pl.pallas_call(
        _combine_kernel,
        grid_spec=grid_spec,
        out_shape=jax.ShapeDtypeStruct((R, d), F32),
        compiler_params=_cparams(("arbitrary",)),
        name="moe_combine",
    )(src_flat, h2, tg, fw, yb)


def _rope_tables():
    half = ROT_DIM // 2
    pos_p = jnp.maximum(jnp.arange(P_ROWS) - LEAD, 0)
    pos_s = SEQ + jnp.arange(S_ROWS) % DEC_SEQ
    pos = jnp.concatenate([pos_p, pos_s])
    inv_freq = 1.0 / (ROPE_THETA ** (jnp.arange(half, dtype=F32) * (2.0 / ROT_DIM)))
    ang = pos.astype(F32)[:, None] * inv_freq[None, :]
    cos, sin = jnp.cos(ang), jnp.sin(ang)
    zeros = jnp.zeros((R, LANES - ROT_DIM), F32)
    cos_t = jnp.concatenate([cos, cos, jnp.ones((R, LANES - ROT_DIM), F32)], axis=1)
    sin_a = jnp.concatenate([jnp.zeros_like(sin), sin, zeros], axis=1)
    sin_b = jnp.concatenate([-sin, jnp.zeros_like(sin), zeros], axis=1)
    return cos_t, sin_a, sin_b


def _pad_lanes(v, fill=0.0):
    v = v.reshape(1, -1)
    return jnp.concatenate([v, jnp.full((1, LANES - v.shape[1]), fill, v.dtype)], axis=1)


def kernel(x_prompt, x_sample, cache_k, cache_v, state_ssm, state_conv, page_table, meta_tokens,
           norm_mix_w, w_in, conv_w, conv_b, dt_bias, a_log, d_skip, ssd_norm_w, w_ssd_out,
           lambda_q1, lambda_k1, lambda_q2, lambda_k2, subln_w, w_attn_out, w_o, norm_ffn_w,
           router_w, router_b, w_gate_up, b_gate_up, w_down, b_down, final_norm_w):
    l = 0
    d = D_MODEL
    x_all = jnp.concatenate([jnp.zeros((LEAD, d), F32), meta_tokens, x_prompt[0], x_sample.reshape(S_ROWS, d)], axis=0)
    xn = rmsnorm_rows(x_all, norm_mix_w[l])
    w = w_in[l]
    o_z, o_xbc, o_dt, o_q, o_k, o_v, o_ga, o_gb = (int(o) for o in np.cumsum(
        (0, SSD_INNER, CONV_DIM, SSD_HEADS, 2048, 1024, 1024, d)))
    w_perm = jnp.concatenate([w[:, o_z:o_xbc], w[:, o_ga:o_gb + d], w[:, o_xbc:o_dt], w[:, o_q:o_ga]], axis=1).astype(BF16)
    w_dt = jnp.concatenate([w[:, o_dt:o_q], jnp.zeros((d, LANES - SSD_HEADS), F32)], axis=1).astype(BF16)
    cos_t, sin_a, sin_b = _rope_tables()
    proj, kb, vt = in_projection(xn, w_perm, cos_t, sin_a, sin_b)
    dtraw = dt_projection(xn, w_dt)

    e_mat = np.zeros((LANES, SSD_INNER), np.float32)
    for h in range(SSD_HEADS):
        e_mat[h, h * SSD_HEAD_DIM:(h + 1) * SSD_HEAD_DIM] = 1.0
    ssd_params = (conv_w[l], conv_b[l].reshape(1, CONV_DIM), _pad_lanes(dt_bias[l]), _pad_lanes(a_log[l]),
                  jnp.repeat(d_skip[l], SSD_HEAD_DIM).reshape(1, SSD_INNER), ssd_norm_w[l].reshape(1, SSD_INNER),
                  jnp.asarray(e_mat))
    yzn_p, ssm_p = ssd_prompt(proj, dtraw, ssd_params)
    conv_prev = jnp.concatenate([jnp.zeros((DEC_BATCH, SUBLANES - (SSD_CONV - 1), CONV_DIM), F32), state_conv[l]], axis=1)
    yzn_s, ssm_s = ssd_sample(proj, dtraw, conv_prev, state_ssm[l].reshape(DEC_BATCH, SSD_INNER, SSD_STATE), ssd_params)
    yzn = jnp.concatenate([yzn_p, yzn_s], axis=0)

    k_rows = proj[:, PCOL_K:PCOL_V]
    v_rows = proj[:, PCOL_V:P_COLS]
    lam_params = (lambda_q1[l].reshape(1, -1), lambda_k1[l].reshape(1, -1), lambda_q2[l].reshape(1, -1),
                  lambda_k2[l].reshape(1, -1))
    subw = subln_w[l].reshape(1, ATT_V_DIM)
    vt_ext = jnp.concatenate([vt.reshape(ATT_KV_HEADS, ATT_V_DIM, R),
                              jnp.ones((ATT_KV_HEADS, ATT_VT_ROWS - ATT_V_DIM, R), BF16)], axis=1)
    on_p = attn_prompt(proj, kb, vt_ext.reshape(ATT_KV_HEADS * ATT_VT_ROWS, R), lam_params,
                       subln_w[l].reshape(ATT_V_DIM, 1))

    q_s = proj[P_ROWS:, PCOL_Q:PCOL_K].reshape(DEC_BATCH, DEC_SEQ, ATT_KV_HEADS, ATT_REP, 2, ATT_HEAD_DIM)
    q_s = jnp.transpose(q_s, (0, 2, 4, 1, 3, 5)).reshape(DEC_BATCH, ATT_S_ROWS, ATT_HEAD_DIM).astype(BF16)
    n_phys = cache_k.shape[1]
    kc = cache_k[l].reshape(n_phys, KROWS, ATT_HEAD_DIM)
    pad_new = lambda u: jnp.concatenate([u.reshape(DEC_BATCH, DEC_SEQ, -1),
                                         jnp.zeros((DEC_BATCH, PAGE_SIZE - DEC_SEQ, u.shape[-1]), F32)], axis=1)
    k_new = pad_new(k_rows[P_ROWS:]).reshape(DEC_BATCH, KROWS, ATT_HEAD_DIM)
    v_new = pad_new(v_rows[P_ROWS:]).reshape(1, DEC_BATCH, PAGE_SIZE, ATT_KV_HEADS, ATT_V_DIM)
    o_s = attn_sample(page_table.reshape(-1), q_s, kc, cache_v[l:l + 1], k_new, v_new, lam_params, subw)
    o_s = o_s.reshape(DEC_BATCH, ATT_KV_HEADS, DEC_SEQ, ATT_REP, ATT_V_DIM)
    o_s = jnp.transpose(o_s, (0, 2, 1, 3, 4)).reshape(S_ROWS, ATT_HEADS * ATT_V_DIM).astype(BF16)
    on = lax.dynamic_update_slice(on_p, o_s, (P_ROWS, 0))

    mix = mix_branches(yzn, on, proj, w_ssd_out[l].astype(BF16), w_attn_out[l].astype(BF16))
    rw = jnp.concatenate([router_w[l], jnp.zeros((d, LANES - N_EXPERTS), F32)], axis=1)
    rb = _pad_lanes(router_b[l], NEG)
    h2, n_rows, top_i, top_g = post_mixer(mix, x_all, w_o[l].astype(BF16), norm_ffn_w[l].reshape(1, d), rw, rb)

    rank, counts = expert_ranks(top_i)
    counts = counts[0, :N_EXPERTS]
    padded = (counts + MOE_TM - 1) // MOE_TM * MOE_TM
    pad_end = jnp.cumsum(padded)
    pad_start = pad_end - padded
    dest = pad_start[top_i[:, :TOP_K]] + rank[:, :TOP_K]
    is_tok = (jnp.arange(R) >= LEAD)[:, None]
    dump = MOE_NB * MOE_TM + jnp.arange(R * TOP_K).reshape(R, TOP_K)
    dest_scatter = jnp.where(is_tok, dest, dump).astype(I32).reshape(-1)
    dest_gather = jnp.where(is_tok, dest, 0).astype(I32).reshape(-1)
    block_start = jnp.arange(MOE_NB) * MOE_TM
    block_e = jnp.minimum(jnp.sum(pad_end[None, :] <= block_start[:, None], axis=1), N_EXPERTS - 1).astype(I32)
    n_used = (pad_end[-1:] // MOE_TM).astype(I32)
    ex = jnp.arange(N_EXPERTS)
    nonempty = padded > 0
    run_of_expert = jnp.cumsum(nonempty) - 1
    later = (ex[None, :] > ex[:, None]) & nonempty[None, :]
    next_expert = jnp.min(jnp.where(later, ex[None, :], N_EXPERTS), axis=1)
    next_expert = jnp.where(next_expert < N_EXPERTS, next_expert, -1)
    sched = (block_e, n_used, run_of_expert[block_e].astype(I32), next_expert[block_e].astype(I32))
    xs = moe_dispatch(dest_scatter, n_rows, jnp.zeros((MOE_ROWS, d), F32))
    hidden = moe_gate_up(sched, xs, w_gate_up[l], b_gate_up[l].reshape(N_EXPERTS, 1, 2 * D_FF))
    yb = moe_down(sched, hidden, w_down[l], b_down[l].reshape(N_EXPERTS, 1, d))
    y = moe_combine(dest_gather, h2, top_g, final_norm_w.reshape(1, d), yb)

    tok0 = LEAD
    y_prompt = y[LEAD + N_META:P_ROWS][None]
    y_sample = y[P_ROWS:].reshape(DEC_BATCH, DEC_SEQ, d)
    k_prompt = k_rows[tok0:P_ROWS].reshape(1, 1, SEQ + N_META, ATT_KV_HEADS, 2, ATT_HEAD_DIM)
    v_prompt = v_rows[tok0:P_ROWS].reshape(1, 1, SEQ + N_META, ATT_KV_HEADS, ATT_V_DIM)
    k_sample = k_rows[P_ROWS:].reshape(1, DEC_BATCH, DEC_SEQ, ATT_KV_HEADS, 2, ATT_HEAD_DIM)
    v_sample = v_rows[P_ROWS:].reshape(1, DEC_BATCH, DEC_SEQ, ATT_KV_HEADS, ATT_V_DIM)
    ssm_prompt = ssm_p.reshape(1, 1, SSD_HEADS, SSD_HEAD_DIM, SSD_STATE)
    ssm_sample = ssm_s.reshape(1, DEC_BATCH, SSD_HEADS, SSD_HEAD_DIM, SSD_STATE)
    xbc = proj[:, PCOL_XBC:PCOL_XBC + CONV_DIM]
    conv_prompt = xbc[P_ROWS - (SSD_CONV - 1):P_ROWS][None, None]
    xbc_s = xbc[P_ROWS:].reshape(DEC_BATCH, DEC_SEQ, CONV_DIM)
    conv_sample = jnp.concatenate([state_conv[l], xbc_s], axis=1)[:, -(SSD_CONV - 1):][None]
    return (y_prompt, y_sample, k_prompt, v_prompt, k_sample, v_sample, ssm_prompt, ssm_sample, conv_prompt, conv_sample)
```

```python
import functools
import math

import jax
import jax.numpy as jnp
import numpy as np
from jax import lax
from jax.experimental import pallas as pl
from jax.experimental.pallas import tpu as pltpu

F32 = jnp.float32
BF16 = jnp.bfloat16
I32 = jnp.int32
HI = lax.Precision.HIGHEST

D_MODEL = 2048
SEQ = 8192
DEC_BATCH = 32
DEC_SEQ = 4
PAGE_SIZE = 128
N_META = 16
NORM_EPS = 1e-5
SSD_INNER = D_MODEL
SSD_HEAD_DIM = 64
SSD_HEADS = SSD_INNER // SSD_HEAD_DIM
SSD_GROUPS = 4
SSD_REP = SSD_HEADS // SSD_GROUPS
SSD_STATE = 128
SSD_CONV = 4
SSD_CHUNK = 128
CONV_DIM = SSD_INNER + 2 * SSD_GROUPS * SSD_STATE
ATT_HEADS = 8
ATT_KV_HEADS = 4
ATT_REP = ATT_HEADS // ATT_KV_HEADS
ATT_HEAD_DIM = D_MODEL // ATT_HEADS // 2
ATT_V_DIM = 2 * ATT_HEAD_DIM
ATT_SCALE = ATT_HEAD_DIM ** -0.5
ROT_DIM = ATT_HEAD_DIM // 4
ROPE_THETA = 500000.0
N_EXPERTS = 32
TOP_K = 4
D_FF = D_MODEL
SWIGLU_LIMIT = 7.0
SWIGLU_ALPHA = 1.702
LAM_INIT = 0.8 - 0.6 * math.exp(-0.3 * 0)

LEAD = (-N_META) % SSD_CHUNK
P_ROWS = LEAD + N_META + SEQ
S_ROWS = DEC_BATCH * DEC_SEQ
R = P_ROWS + S_ROWS
N_CHUNKS = P_ROWS // SSD_CHUNK

LANES = 128
SUBLANES = 8
VMEM_LIMIT = 56 * 1024 * 1024

PCOL_Z, PCOL_GA, PCOL_GB, PCOL_XBC, PCOL_Q, PCOL_K, PCOL_V = 0, 2048, 4096, 6144, 9216, 11264, 12288
P_COLS = 13312
PROJ_TN = 1024
ROPE_BLOCKS = (PCOL_Q // PROJ_TN, PCOL_V // PROJ_TN)

MOE_TM = 256
MOE_NB = -(-(R * TOP_K) // MOE_TM) + N_EXPERTS
MOE_DUMP_BLOCKS = -(-(LEAD * TOP_K) // MOE_TM)
MOE_ROWS = (MOE_NB + MOE_DUMP_BLOCKS) * MOE_TM
MOE_TN_GATE_UP = 1024
NEG = -1e30


def _cparams(sem, vmem=VMEM_LIMIT):
    return pltpu.CompilerParams(dimension_semantics=sem, vmem_limit_bytes=vmem)


def _sigmoid(x):
    return 1.0 / (1.0 + jnp.exp(-x))


def _rmsnorm_kernel(x_ref, w_ref, o_ref):
    x = x_ref[...]
    ms = jnp.mean(x * x, axis=-1, keepdims=True)
    o_ref[...] = (x * lax.rsqrt(ms + NORM_EPS) * w_ref[...]).astype(o_ref.dtype)


def rmsnorm_rows(x, w, tm=768):
    rows, d = x.shape
    return pl.pallas_call(
        _rmsnorm_kernel,
        grid=(rows // tm,),
        in_specs=[pl.BlockSpec((tm, d), lambda i: (i, 0)), pl.BlockSpec((1, d), lambda i: (0, 0))],
        out_specs=pl.BlockSpec((tm, d), lambda i: (i, 0)),
        out_shape=jax.ShapeDtypeStruct((rows, d), BF16),
        compiler_params=_cparams(("parallel",)),
        name="rmsnorm_rows",
    )(x, w.reshape(1, d))


def _rope(blk, c, sa, sb):
    return blk * c + pltpu.roll(blk, ROT_DIM // 2, axis=1) * sa + pltpu.roll(blk, LANES - ROT_DIM // 2, axis=1) * sb


def _proj_kernel(x_ref, w_ref, c_ref, sa_ref, sb_ref, o_ref, kb_ref, vt_ref):
    j = pl.program_id(1)
    acc = jnp.dot(x_ref[...], w_ref[...], preferred_element_type=F32)
    is_rope = (j >= ROPE_BLOCKS[0]) & (j < ROPE_BLOCKS[1])

    @pl.when(is_rope)
    def _():
        c, sa, sb = c_ref[...], sa_ref[...], sb_ref[...]
        for g in range(acc.shape[1] // LANES):
            sl = slice(g * LANES, (g + 1) * LANES)
            o_ref[:, sl] = _rope(acc[:, sl], c, sa, sb)

    @pl.when(jnp.logical_not(is_rope))
    def _():
        o_ref[...] = acc

    @pl.when(j == PCOL_K // PROJ_TN)
    def _():
        kb_ref[...] = o_ref[...].astype(kb_ref.dtype)

    @pl.when(j == PCOL_V // PROJ_TN)
    def _():
        vt_ref[...] = acc.T.astype(vt_ref.dtype)


def in_projection(xn, w_bf, cos_t, sin_a, sin_b, tm=768):
    rows, d = xn.shape
    tn = PROJ_TN
    tab = pl.BlockSpec((tm, LANES), lambda i, j: (i, 0))
    return pl.pallas_call(
        _proj_kernel,
        grid=(rows // tm, P_COLS // tn),
        in_specs=[pl.BlockSpec((tm, d), lambda i, j: (i, 0)), pl.BlockSpec((d, tn), lambda i, j: (0, j)), tab, tab, tab],
        out_specs=[pl.BlockSpec((tm, tn), lambda i, j: (i, j)),
                   pl.BlockSpec((tm, tn), lambda i, j: (i, 0)),
                   pl.BlockSpec((tn, tm), lambda i, j: (0, i))],
        out_shape=[jax.ShapeDtypeStruct((rows, P_COLS), F32),
                   jax.ShapeDtypeStruct((rows, tn), BF16),
                   jax.ShapeDtypeStruct((tn, rows), BF16)],
        compiler_params=_cparams(("parallel", "arbitrary")),
        name="in_projection",
    )(xn, w_bf, cos_t, sin_a, sin_b)


def _mm_kernel(x_ref, w_ref, o_ref):
    o_ref[...] = jnp.dot(x_ref[...], w_ref[...], preferred_element_type=F32).astype(o_ref.dtype)


def dt_projection(xn, w_dt, tm=1056):
    rows, d = xn.shape
    return pl.pallas_call(
        _mm_kernel,
        grid=(rows // tm,),
        in_specs=[pl.BlockSpec((tm, d), lambda i: (i, 0)), pl.BlockSpec((d, LANES), lambda i: (0, 0))],
        out_specs=pl.BlockSpec((tm, LANES), lambda i: (i, 0)),
        out_shape=jax.ShapeDtypeStruct((rows, LANES), F32),
        compiler_params=_cparams(("parallel",)),
        name="dt_projection",
    )(xn, w_dt)


def _ssd_chunk(xprev, xcur, dtraw, z, valid, s_ref, cw_ref, cb_ref, dtb_ref, alog_ref, dsk_ref, nw_ref, e_ref,
               state_dot_precision):
    L = xcur.shape[0]
    ext = jnp.concatenate([xprev, xcur], axis=0)
    off = SUBLANES - (SSD_CONV - 1)
    acc = cb_ref[...]
    for k in range(SSD_CONV):
        acc = acc + ext[off + k:off + k + L] * cw_ref[k:k + 1, :]
    act = acc * _sigmoid(acc)
    xs = act[:, :SSD_INNER]
    nb = SSD_GROUPS * SSD_STATE
    bm = act[:, SSD_INNER:SSD_INNER + nb].astype(BF16)
    cm = act[:, SSD_INNER + nb:].astype(BF16)

    dpre = dtraw + dtb_ref[...]
    dt = jnp.maximum(dpre, 0.0) + jnp.log1p(jnp.exp(-jnp.abs(dpre)))
    dt = jnp.where(valid, dt, 0.0)
    da = dt * (-jnp.exp(alog_ref[...]))

    li = lax.broadcasted_iota(I32, (L, L), 0)
    si = lax.broadcasted_iota(I32, (L, L), 1)
    tril = li >= si
    eye = li == si
    a_cs = jnp.dot(tril.astype(F32), da, precision=HI, preferred_element_type=F32)
    a_tot = a_cs[L - 1:L, :]
    e_mat = e_ref[...]

    def expand(v):
        out = None
        for _ in range(3):
            piece = v.astype(BF16)
            term = jnp.dot(piece, e_mat, preferred_element_type=F32)
            out = term if out is None else out + term
            v = v - piece.astype(F32)
        return out

    dt_x = expand(dt)
    eacs_x = expand(jnp.exp(a_cs))
    dte_x = expand(jnp.exp(a_tot - a_cs))
    etot = jnp.exp(a_tot)

    xdt = xs * dt_x
    xdt_b = xdt.astype(BF16)
    xde = xdt * dte_x
    gw = SSD_REP * SSD_HEAD_DIM
    lane = lax.broadcasted_iota(I32, (L, LANES), 1)
    nt = (((1,), (1,)), ((), ()))
    tn = (((0,), (0,)), ((), ()))
    y_groups = []
    for g in range(SSD_GROUPS):
        bg = bm[:, g * SSD_STATE:(g + 1) * SSD_STATE]
        cg = cm[:, g * SSD_STATE:(g + 1) * SSD_STATE]
        cb = lax.dot_general(cg, bg, nt, preferred_element_type=F32)
        s_g = s_ref[g * gw:(g + 1) * gw, :]
        y_off = lax.dot_general(cg, s_g.astype(BF16), nt, preferred_element_type=F32)
        if state_dot_precision is None:
            upd = lax.dot_general(xde[:, g * gw:(g + 1) * gw].astype(BF16), bg, tn, preferred_element_type=F32)
        else:
            upd = lax.dot_general(xde[:, g * gw:(g + 1) * gw], act[:, SSD_INNER + g * SSD_STATE:SSD_INNER + (g + 1) * SSD_STATE],
                                  tn, precision=state_dot_precision, preferred_element_type=F32)
        pairs = []
        decs = []
        for q in range(SSD_REP // 2):
            yd = []
            for r in range(2):
                h = g * SSD_REP + 2 * q + r
                colb = jnp.broadcast_to(a_cs[:, h:h + 1], (L, L))
                rowb = jnp.sum(jnp.where(eye, colb, 0.0), axis=0, keepdims=True)
                lm = jnp.where(tril, jnp.exp(jnp.minimum(colb - rowb, 0.0)), 0.0)
                m = (cb * lm).astype(BF16)
                c0 = (g * SSD_REP + 2 * q) * SSD_HEAD_DIM
                yd.append(jnp.dot(m, xdt_b[:, c0:c0 + LANES], preferred_element_type=F32))
                decs.append(jnp.broadcast_to(etot[:, h:h + 1], (SSD_HEAD_DIM, SSD_STATE)))
            pairs.append(jnp.where(lane < SSD_HEAD_DIM, yd[0], yd[1]))
        y_g = jnp.concatenate(pairs, axis=1) + y_off * eacs_x[:, g * gw:(g + 1) * gw]
        y_groups.append(y_g)
        s_ref[g * gw:(g + 1) * gw, :] = jnp.concatenate(decs, axis=0) * s_g + upd
    y = jnp.concatenate(y_groups, axis=1) + dsk_ref[...] * xs
    yz = y * (z * _sigmoid(z))
    outs = []
    for g in range(SSD_GROUPS):
        yg = yz[:, g * gw:(g + 1) * gw]
        ms = jnp.mean(yg * yg, axis=-1, keepdims=True)
        outs.append(yg * lax.rsqrt(ms + NORM_EPS) * nw_ref[:, g * gw:(g + 1) * gw])
    return jnp.concatenate(outs, axis=1)


def _ssd_prompt_kernel(xprev_ref, xcur_ref, dt_ref, z_ref, cw_ref, cb_ref, dtb_ref, alog_ref, dsk_ref, nw_ref, e_ref,
                       y_ref, s_ref):
    c = pl.program_id(0)

    @pl.when(c == 0)
    def _():
        s_ref[...] = jnp.zeros_like(s_ref)

    xprev = jnp.where(c == 0, 0.0, xprev_ref[...])
    row = c * SSD_CHUNK + lax.broadcasted_iota(I32, (SSD_CHUNK, 1), 0)
    y = _ssd_chunk(xprev, xcur_ref[...], dt_ref[...], z_ref[...], row >= LEAD, s_ref,
                   cw_ref, cb_ref, dtb_ref, alog_ref, dsk_ref, nw_ref, e_ref, None)
    y_ref[...] = y.astype(y_ref.dtype)


def _full(shape):
    return pl.BlockSpec(shape, lambda *_: (0,) * len(shape))


def ssd_prompt(proj, dtraw, params):
    L = SSD_CHUNK
    xb = PCOL_XBC // CONV_DIM
    in_specs = [
        pl.BlockSpec((SUBLANES, CONV_DIM), lambda c: (jnp.maximum(c * (L // SUBLANES) - 1, 0), xb)),
        pl.BlockSpec((L, CONV_DIM), lambda c: (c, xb)),
        pl.BlockSpec((L, LANES), lambda c: (c, 0)),
        pl.BlockSpec((L, SSD_INNER), lambda c: (c, PCOL_Z // SSD_INNER)),
    ] + [_full(p.shape) for p in params]
    return pl.pallas_call(
        _ssd_prompt_kernel,
        grid=(N_CHUNKS,),
        in_specs=in_specs,
        out_specs=[pl.BlockSpec((L, SSD_INNER), lambda c: (c, 0)), _full((SSD_INNER, SSD_STATE))],
        out_shape=[jax.ShapeDtypeStruct((P_ROWS, SSD_INNER), BF16), jax.ShapeDtypeStruct((SSD_INNER, SSD_STATE), F32)],
        compiler_params=_cparams(("arbitrary",)),
        name="ssd_prompt",
    )(proj, proj, dtraw, proj, *params)


SSD_S_NB = 4


def _ssd_sample_kernel(xprev_ref, xcur_ref, dt_ref, z_ref, sin_ref, cw_ref, cb_ref, dtb_ref, alog_ref, dsk_ref, nw_ref,
                       e_ref, y_ref, sout_ref):
    L = SUBLANES
    valid = lax.broadcasted_iota(I32, (L, 1), 0) < DEC_SEQ
    ys = []
    for b in range(SSD_S_NB):
        rows = slice(b * DEC_SEQ, (b + 1) * DEC_SEQ)
        pad = lambda u: jnp.concatenate([u, jnp.zeros((L - DEC_SEQ, u.shape[1]), u.dtype)], axis=0)
        sout_ref[b] = sin_ref[b]
        y = _ssd_chunk(xprev_ref[b], pad(xcur_ref[rows, :]), pad(dt_ref[rows, :]), pad(z_ref[rows, :]), valid,
                       sout_ref.at[b], cw_ref, cb_ref, dtb_ref, alog_ref, dsk_ref, nw_ref, e_ref, HI)
        ys.append(y[:DEC_SEQ])
    y_ref[...] = jnp.concatenate(ys, axis=0).astype(y_ref.dtype)


def ssd_sample(proj, dtraw, conv_prev, state, params):
    nb = SSD_S_NB
    rb = nb * DEC_SEQ
    r0 = P_ROWS // rb
    in_specs = [
        pl.BlockSpec((nb, SUBLANES, CONV_DIM), lambda i: (i, 0, 0)),
        pl.BlockSpec((rb, CONV_DIM), lambda i: (r0 + i, PCOL_XBC // CONV_DIM)),
        pl.BlockSpec((rb, LANES), lambda i: (r0 + i, 0)),
        pl.BlockSpec((rb, SSD_INNER), lambda i: (r0 + i, PCOL_Z // SSD_INNER)),
        pl.BlockSpec((nb, SSD_INNER, SSD_STATE), lambda i: (i, 0, 0)),
    ] + [_full(p.shape) for p in params]
    return pl.pallas_call(
        _ssd_sample_kernel,
        grid=(DEC_BATCH // nb,),
        in_specs=in_specs,
        out_specs=[pl.BlockSpec((rb, SSD_INNER), lambda i: (i, 0)),
                   pl.BlockSpec((nb, SSD_INNER, SSD_STATE), lambda i: (i, 0, 0))],
        out_shape=[jax.ShapeDtypeStruct((S_ROWS, SSD_INNER), BF16),
                   jax.ShapeDtypeStruct((DEC_BATCH, SSD_INNER, SSD_STATE), F32)],
        compiler_params=_cparams(("parallel",)),
        name="ssd_sample",
    )(conv_prev, proj, dtraw, proj, state, *params)


def _lambda(lq1_ref, lk1_ref, lq2_ref, lk2_ref):
    s1 = jnp.sum(lq1_ref[...] * lk1_ref[...], axis=-1, keepdims=True)
    s2 = jnp.sum(lq2_ref[...] * lk2_ref[...], axis=-1, keepdims=True)
    return jnp.exp(s1) - jnp.exp(s2) + LAM_INIT


def _subnorm(o, subw_ref):
    ms = jnp.mean(o * o, axis=-1, keepdims=True)
    return (o * lax.rsqrt(ms + NORM_EPS) * subw_ref[...]) * (1.0 - LAM_INIT)


ATT_TQ = 256
ATT_TK = 768
BF16_SUBLANES = 16
ATT_VT_ROWS = ATT_V_DIM + BF16_SUBLANES
EXP2_SCALE = ATT_SCALE * math.log2(math.e)


def _attn_prompt_kernel(q_ref, k_ref, vt_ref, lq1_ref, lk1_ref, lq2_ref, lk2_ref, subw_ref, o_ref, m_s, acc_s):
    i = pl.program_id(1)
    tq, tk, hd = ATT_TQ, ATT_TK, ATT_HEAD_DIM
    q = q_ref[...]
    qc = [jnp.concatenate([q[:, r * 2 * hd + c * hd: r * 2 * hd + (c + 1) * hd] for r in range(ATT_REP)], axis=0).astype(BF16)
          for c in range(2)]
    m_s[...] = jnp.full_like(m_s, NEG)
    acc_s[...] = jnp.zeros_like(acc_s)
    nt = (((1,), (1,)), ((), ()))
    n_blocks = (i * tq + tq - 1) // tk + 1

    def step(j, mask):
        k0 = pl.multiple_of(j * tk, tk)
        kblk = k_ref[pl.ds(k0, tk), :]
        vt = vt_ref[:, pl.ds(k0, tk)]
        sts = [lax.dot_general(kblk[:, c * hd:(c + 1) * hd], qc[c], nt, preferred_element_type=F32) for c in range(2)]
        ps, alphas = [], []
        for c in range(2):
            st = sts[c]
            if mask == "causal":
                kidx = j * tk + lax.broadcasted_iota(I32, st.shape, 0)
                qidx = i * tq + lax.broadcasted_iota(I32, st.shape, 1) % tq
                ok = (kidx <= qidx) & ((kidx >= LEAD) | (qidx < LEAD))
                st = jnp.where(ok, st, NEG)
            elif mask == "lead":
                st = st + jnp.where(lax.broadcasted_iota(I32, (tk, 1), 0) < LEAD, NEG, 0.0)
            m_old = m_s[c]
            m_new = jnp.maximum(m_old, jnp.max(st, axis=0, keepdims=True))
            alphas.append(jnp.exp2((m_old - m_new) * EXP2_SCALE))
            ps.append(jnp.exp2((st - m_new) * EXP2_SCALE).astype(BF16))
            m_s[c] = m_new
        for c in range(2):
            acc_s[c] = alphas[c] * acc_s[c] + jnp.dot(vt, ps[c], preferred_element_type=F32)

    @pl.when(n_blocks == 1)
    def _():
        step(0, "causal")

    @pl.when(n_blocks > 1)
    def _():
        step(0, "lead")

        def body(j, carry):
            step(j, None)
            return carry

        lax.fori_loop(1, n_blocks - 1, body, 0)
        step(n_blocks - 1, "causal")

    lam = _lambda(lq1_ref, lk1_ref, lq2_ref, lk2_ref)
    for r in range(ATT_REP):
        cols = slice(r * tq, (r + 1) * tq)
        o0 = acc_s[0, :ATT_V_DIM, cols] / acc_s[0, ATT_V_DIM:ATT_V_DIM + 1, cols]
        o1 = acc_s[1, :ATT_V_DIM, cols] / acc_s[1, ATT_V_DIM:ATT_V_DIM + 1, cols]
        ot = o0 - lam * o1
        ms = jnp.mean(ot * ot, axis=0, keepdims=True)
        ot = (ot * lax.rsqrt(ms + NORM_EPS) * subw_ref[...]) * (1.0 - LAM_INIT)
        o_ref[:, r * ATT_V_DIM:(r + 1) * ATT_V_DIM] = ot.T.astype(o_ref.dtype)


def attn_prompt(proj, kb, vt_ext, lam_params, subw_col):
    tq = ATT_TQ
    gq = ATT_REP * 2 * ATT_HEAD_DIM
    vec = _full((1, ATT_HEAD_DIM))
    return pl.pallas_call(
        _attn_prompt_kernel,
        grid=(ATT_KV_HEADS, R // tq),
        in_specs=[pl.BlockSpec((tq, gq), lambda g, i: (i, PCOL_Q // gq + g)),
                  pl.BlockSpec((R, ATT_V_DIM), lambda g, i: (0, g)),
                  pl.BlockSpec((ATT_VT_ROWS, R), lambda g, i: (g, 0)),
                  vec, vec, vec, vec, _full((ATT_V_DIM, 1))],
        out_specs=pl.BlockSpec((tq, gq), lambda g, i: (i, g)),
        out_shape=jax.ShapeDtypeStruct((R, ATT_HEADS * ATT_V_DIM), BF16),
        scratch_shapes=[pltpu.VMEM((2, 1, ATT_REP * tq), F32),
                        pltpu.VMEM((2, ATT_VT_ROWS, ATT_REP * tq), F32)],
        compiler_params=_cparams(("parallel", "arbitrary")),
        name="attn_prompt",
    )(proj, kb, vt_ext, *lam_params, subw_col)


N_PAGES = SEQ // PAGE_SIZE
ATT_S_ROWS = ATT_KV_HEADS * 2 * DEC_SEQ * ATT_REP
ATT_S_GROUP = 2 * DEC_SEQ * ATT_REP
ATT_S_PPS = 8
ATT_S_STEPS = N_PAGES // ATT_S_PPS
KROWS = PAGE_SIZE * ATT_KV_HEADS * 2
V_LANE_TILES = ATT_V_DIM // LANES


def _attn_sample_kernel(pt_ref, q_ref, *refs):
    npg = ATT_S_PPS
    kc_refs = refs[:npg]
    vc_refs = [refs[npg + V_LANE_TILES * p: npg + V_LANE_TILES * (p + 1)] for p in range(npg)]
    base = npg * (1 + V_LANE_TILES)
    kn_ref = refs[base]
    vn_refs = refs[base + 1: base + 1 + V_LANE_TILES]
    x_ref, lq1_ref, lk1_ref, lq2_ref, lk2_ref, subw_ref, o_ref, m_s, l_s, acc_s = refs[base + 1 + V_LANE_TILES:]
    j = pl.program_id(1)
    nt = (((1,), (1,)), ((), ()))
    nhc = ATT_KV_HEADS * 2
    rows_hc = DEC_SEQ * ATT_REP

    @pl.when(j == 0)
    def _():
        m_s[...] = jnp.full_like(m_s, NEG)
        l_s[...] = jnp.zeros_like(l_s)
        acc_s[...] = jnp.zeros_like(acc_s)

    def step(k_pages, v_pages, masked):
        q = q_ref[0]
        cols = []
        for k_src in k_pages:
            ss = []
            for hc in range(nhc):
                khc = k_src[pl.ds(hc, PAGE_SIZE, stride=nhc), :].astype(BF16)
                ss.append(lax.dot_general(q[hc * rows_hc:(hc + 1) * rows_hc, :], khc, nt, preferred_element_type=F32))
            cols.append(jnp.concatenate(ss, axis=0))
        s = jnp.concatenate(cols, axis=1) if len(cols) > 1 else cols[0]
        if masked:
            t_q = (lax.broadcasted_iota(I32, s.shape, 0) % rows_hc) // ATT_REP
            t_k = lax.broadcasted_iota(I32, s.shape, 1)
            s = jnp.where(t_k <= t_q, s, NEG)
        m_old = m_s[...]
        m_new = jnp.maximum(m_old, jnp.max(s, axis=1, keepdims=True))
        alpha = jnp.exp2((m_old - m_new) * EXP2_SCALE)
        p = jnp.exp2((s - m_new[:, :1]) * EXP2_SCALE)
        l_s[...] = alpha * l_s[...] + jnp.sum(p, axis=1, keepdims=True)
        pb = p.astype(BF16)
        head_of_col = lax.broadcasted_iota(I32, (ATT_S_ROWS, PAGE_SIZE * ATT_KV_HEADS), 1) % ATT_KV_HEADS
        head_of_row = lax.broadcasted_iota(I32, (ATT_S_ROWS, PAGE_SIZE * ATT_KV_HEADS), 0) // ATT_S_GROUP
        own_head = head_of_col == head_of_row
        pv = None
        for pi, v_tiles in enumerate(v_pages):
            v2 = jnp.concatenate([v[...].reshape(PAGE_SIZE * ATT_KV_HEADS, LANES) for v in v_tiles], axis=1).astype(BF16)
            spread = jnp.dot(pb[:, pi * PAGE_SIZE:(pi + 1) * PAGE_SIZE], x_ref[...], preferred_element_type=F32)
            pe = jnp.where(own_head, spread, 0.0).astype(BF16)
            d = jnp.dot(pe, v2, preferred_element_type=F32)
            pv = d if pv is None else pv + d
        acc_s[...] = alpha[:, :1] * acc_s[...] + pv
        m_s[...] = m_new

    @pl.when(j < ATT_S_STEPS)
    def _():
        step(kc_refs, vc_refs, False)

    @pl.when(j == ATT_S_STEPS)
    def _():
        step([kn_ref], [vn_refs], True)
        lam = _lambda(lq1_ref, lk1_ref, lq2_ref, lk2_ref)
        for g in range(ATT_KV_HEADS):
            r0 = g * ATT_S_GROUP
            o0 = acc_s[r0:r0 + rows_hc, :] / l_s[r0:r0 + rows_hc, :1]
            o1 = acc_s[r0 + rows_hc:r0 + 2 * rows_hc, :] / l_s[r0 + rows_hc:r0 + 2 * rows_hc, :1]
            o_ref[0, g] = _subnorm(o0 - lam * o1, subw_ref)


def attn_sample(page_table, q, kc, vc, knew, vnew, lam_params, subw):
    vec = _full((1, ATT_HEAD_DIM))
    kblk = (None, KROWS, ATT_HEAD_DIM)
    vblk = (None, None, PAGE_SIZE, ATT_KV_HEADS, LANES)

    def page(b, j, pt, p):
        return pt[b * N_PAGES + jnp.minimum(j, ATT_S_STEPS - 1) * ATT_S_PPS + p]

    k_specs = [pl.BlockSpec(kblk, functools.partial(lambda b, j, pt, p: (page(b, j, pt, p), 0, 0), p=p))
               for p in range(ATT_S_PPS)]
    v_specs = [pl.BlockSpec(vblk, functools.partial(lambda b, j, pt, p, e: (0, page(b, j, pt, p), 0, 0, e), p=p, e=e))
               for p in range(ATT_S_PPS) for e in range(V_LANE_TILES)]
    spread = np.zeros((PAGE_SIZE, PAGE_SIZE * ATT_KV_HEADS), np.float32)
    for t in range(PAGE_SIZE):
        spread[t, t * ATT_KV_HEADS:(t + 1) * ATT_KV_HEADS] = 1.0
    kn_spec = pl.BlockSpec(kblk, lambda b, j, pt: (b, 0, 0))
    vn_specs = [pl.BlockSpec(vblk, functools.partial(lambda b, j, pt, e: (0, b, 0, 0, e), e=e)) for e in range(V_LANE_TILES)]
    grid_spec = pltpu.PrefetchScalarGridSpec(
        num_scalar_prefetch=1,
        grid=(DEC_BATCH, ATT_S_STEPS + 1),
        in_specs=[pl.BlockSpec((1, ATT_S_ROWS, ATT_HEAD_DIM), lambda b, j, pt: (b, 0, 0)),
                  *k_specs, *v_specs, kn_spec, *vn_specs, _full((PAGE_SIZE, PAGE_SIZE * ATT_KV_HEADS)),
                  vec, vec, vec, vec, _full((1, ATT_V_DIM))],
        out_specs=pl.BlockSpec((1, ATT_KV_HEADS, DEC_SEQ * ATT_REP, ATT_V_DIM), lambda b, j, pt: (b, 0, 0, 0)),
        scratch_shapes=[pltpu.VMEM((ATT_S_ROWS, LANES), F32), pltpu.VMEM((ATT_S_ROWS, LANES), F32),
                        pltpu.VMEM((ATT_S_ROWS, ATT_V_DIM), F32)],
    )
    return pl.pallas_call(
        _attn_sample_kernel,
        grid_spec=grid_spec,
        out_shape=jax.ShapeDtypeStruct((DEC_BATCH, ATT_KV_HEADS, DEC_SEQ * ATT_REP, ATT_V_DIM), F32),
        compiler_params=_cparams(("parallel", "arbitrary")),
        name="attn_sample",
    )(page_table, q, *([kc] * ATT_S_PPS), *([vc] * (ATT_S_PPS * V_LANE_TILES)), knew, *([vnew] * V_LANE_TILES),
      jnp.asarray(spread, BF16), *lam_params, subw)


def _mix_kernel(ya_ref, yb_ref, ga_ref, gb_ref, wa_ref, wb_ref, o_ref):
    oa = jnp.dot(ya_ref[...], wa_ref[...], preferred_element_type=F32)
    ob = jnp.dot(yb_ref[...], wb_ref[...], preferred_element_type=F32)
    o_ref[...] = (_sigmoid(ga_ref[...]) * oa + _sigmoid(gb_ref[...]) * ob).astype(o_ref.dtype)


def mix_branches(yzn, on, proj, wa, wb, tm=256):
    d = D_MODEL
    row = lambda i: (i, 0)
    return pl.pallas_call(
        _mix_kernel,
        grid=(R // tm,),
        in_specs=[pl.BlockSpec((tm, d), row), pl.BlockSpec((tm, d), row),
                  pl.BlockSpec((tm, d), lambda i: (i, PCOL_GA // d)), pl.BlockSpec((tm, d), lambda i: (i, PCOL_GB // d)),
                  _full((d, d)), _full((d, d))],
        out_specs=pl.BlockSpec((tm, d), row),
        out_shape=jax.ShapeDtypeStruct((R, d), BF16),
        compiler_params=_cparams(("parallel",)),
        name="mix_branches",
    )(yzn, on, proj, proj, wa, wb)


def _post_kernel(mix_ref, h_ref, wo_ref, nw_ref, rw_ref, rb_ref, h2_ref, n_ref, ti_ref, tg_ref):
    h2 = h_ref[...] + jnp.dot(mix_ref[...], wo_ref[...], preferred_element_type=F32)
    h2_ref[...] = h2
    ms = jnp.mean(h2 * h2, axis=-1, keepdims=True)
    n = h2 * lax.rsqrt(ms + NORM_EPS) * nw_ref[...]
    n_ref[...] = n
    logits = jnp.dot(n, rw_ref[...], precision=HI, preferred_element_type=F32) + rb_ref[...]
    lane = lax.broadcasted_iota(I32, logits.shape, 1)
    work = logits
    vals, idxs = [], []
    for _ in range(TOP_K):
        mx = jnp.max(work, axis=1, keepdims=True)
        am = jnp.min(jnp.where(work == mx, lane, LANES), axis=1, keepdims=True)
        vals.append(mx)
        idxs.append(am)
        work = jnp.where(lane == am, -jnp.inf, work)
    es = [jnp.exp(v - vals[0]) for v in vals]
    den = es[0] + es[1] + es[2] + es[3]
    ti = jnp.zeros(logits.shape, I32)
    tg = jnp.zeros(logits.shape, F32)
    for k in range(TOP_K):
        ti = jnp.where(lane == k, idxs[k], ti)
        tg = jnp.where(lane == k, es[k] / den, tg)
    ti_ref[...] = ti
    tg_ref[...] = tg


def post_mixer(mix, h, wo, nw, rw, rb, tm=256):
    d = D_MODEL
    row = lambda i: (i, 0)
    return pl.pallas_call(
        _post_kernel,
        grid=(R // tm,),
        in_specs=[pl.BlockSpec((tm, d), row), pl.BlockSpec((tm, d), row), _full((d, d)), _full((1, d)),
                  _full((d, LANES)), _full((1, LANES))],
        out_specs=[pl.BlockSpec((tm, d), row), pl.BlockSpec((tm, d), row),
                   pl.BlockSpec((tm, LANES), row), pl.BlockSpec((tm, LANES), row)],
        out_shape=[jax.ShapeDtypeStruct((R, d), F32), jax.ShapeDtypeStruct((R, d), F32),
                   jax.ShapeDtypeStruct((R, LANES), I32), jax.ShapeDtypeStruct((R, LANES), F32)],
        compiler_params=_cparams(("parallel",)),
        name="post_mixer",
    )(mix, h, wo, nw, rw, rb)


def _rank_kernel(ti_ref, rank_ref, cnt_ref, carry_s):
    i = pl.program_id(0)
    tm = ti_ref.shape[0]

    @pl.when(i == 0)
    def _():
        carry_s[...] = jnp.zeros_like(carry_s)

    ti = ti_ref[...]
    lane = lax.broadcasted_iota(I32, ti.shape, 1)
    row = i * tm + lax.broadcasted_iota(I32, (tm, 1), 0)
    valid = row >= LEAD
    ohs = [jnp.where((lane == ti[:, k:k + 1]) & valid, 1.0, 0.0) for k in range(TOP_K)]
    osum = ohs[0] + ohs[1] + ohs[2] + ohs[3]
    li = lax.broadcasted_iota(I32, (tm, tm), 0)
    si = lax.broadcasted_iota(I32, (tm, tm), 1)
    before = jnp.dot((li > si).astype(BF16), osum.astype(BF16), preferred_element_type=F32) + carry_s[...]
    rank = jnp.zeros(ti.shape, I32)
    for k in range(TOP_K):
        rk = jnp.sum(ohs[k] * before, axis=1, keepdims=True)
        rank = jnp.where(lane == k, rk.astype(I32), rank)
    rank_ref[...] = rank
    carry_s[...] = carry_s[...] + jnp.sum(osum, axis=0, keepdims=True)
    cnt_ref[...] = carry_s[...].astype(I32)


def expert_ranks(ti, tm=256):
    return pl.pallas_call(
        _rank_kernel,
        grid=(R // tm,),
        in_specs=[pl.BlockSpec((tm, LANES), lambda i: (i, 0))],
        out_specs=[pl.BlockSpec((tm, LANES), lambda i: (i, 0)), _full((1, LANES))],
        out_shape=[jax.ShapeDtypeStruct((R, LANES), I32), jax.ShapeDtypeStruct((1, LANES), I32)],
        scratch_shapes=[pltpu.VMEM((1, LANES), F32)],
        compiler_params=_cparams(("arbitrary",)),
        name="expert_ranks",
    )(ti)


DISPATCH_TM = 256


def _dispatch_kernel(dest_ref, n_ref, xs_in_ref, xs_ref, sem):
    del xs_in_ref
    i = pl.program_id(0)
    tm = DISPATCH_TM

    def body(r, carry):
        for k in range(TOP_K):
            d = dest_ref[(i * tm + r) * TOP_K + k]
            pltpu.make_async_copy(n_ref.at[pl.ds(r, 1), :], xs_ref.at[pl.ds(d, 1), :], sem).start()
        return carry

    lax.fori_loop(0, tm, body, 0)
    for k in range(TOP_K):
        pltpu.make_async_copy(n_ref, xs_ref.at[pl.ds(0, tm), :], sem).wait()


def moe_dispatch(dest_flat, n, xs_init):
    tm = DISPATCH_TM
    grid_spec = pltpu.PrefetchScalarGridSpec(
        num_scalar_prefetch=1,
        grid=(R // tm,),
        in_specs=[pl.BlockSpec((tm, n.shape[1]), lambda i, d: (i, 0)), pl.BlockSpec(memory_space=pl.ANY)],
        out_specs=pl.BlockSpec(memory_space=pl.ANY),
        scratch_shapes=[pltpu.SemaphoreType.DMA],
    )
    return pl.pallas_call(
        _dispatch_kernel,
        grid_spec=grid_spec,
        out_shape=jax.ShapeDtypeStruct(xs_init.shape, xs_init.dtype),
        input_output_aliases={2: 0},
        compiler_params=_cparams(("arbitrary",)),
        name="moe_dispatch",
    )(dest_flat, n, xs_init)


def _new_expert(be_ref, i):
    return (i == 0) | (be_ref[i] != be_ref[jnp.maximum(i - 1, 0)])


def _stream_expert_weights(i, be_ref, run_ref, nxt_ref, copies, wbuf, wbf):
    slot = run_ref[i] % 2

    @pl.when(i == 0)
    def _():
        for c in copies(be_ref[0], 0):
            c.start()

    @pl.when(_new_expert(be_ref, i))
    def _():
        for c in copies(be_ref[i], slot):
            c.wait()

        @pl.when(nxt_ref[i] >= 0)
        def _():
            for c in copies(nxt_ref[i], 1 - slot):
                c.start()

        for t in range(wbf.shape[0]):
            wbf[t] = wbuf[slot, t].astype(BF16)


def _gmm1_kernel(be_ref, nu_ref, run_ref, nxt_ref, x_ref, w_hbm, bg_ref, bu_ref, h_ref, wbuf, wbf, sem):
    j = pl.program_id(0)
    i = pl.program_id(1)
    tn = MOE_TN_GATE_UP

    def copies(e, slot):
        cols = (pl.multiple_of(j * tn, tn), pl.multiple_of(D_FF + j * tn, tn))
        return [pltpu.make_async_copy(w_hbm.at[e, :, pl.ds(c0, tn)], wbuf.at[slot, t], sem.at[slot, t])
                for t, c0 in enumerate(cols)]

    @pl.when(i < nu_ref[0])
    def _():
        _stream_expert_weights(i, be_ref, run_ref, nxt_ref, copies, wbuf, wbf)
        x = x_ref[...].astype(BF16)
        hg = jnp.dot(x, wbf[0], preferred_element_type=F32) + bg_ref[0]
        hu = jnp.dot(x, wbf[1], preferred_element_type=F32) + bu_ref[0]
        g = jnp.minimum(hg, SWIGLU_LIMIT)
        u = jnp.clip(hu, -SWIGLU_LIMIT, SWIGLU_LIMIT)
        h_ref[...] = (g * _sigmoid(SWIGLU_ALPHA * g) * (u + 1.0)).astype(h_ref.dtype)

    @pl.when(i >= nu_ref[0])
    def _():
        h_ref[...] = jnp.zeros_like(h_ref)


def moe_gate_up(sched, xs, w_gu, b_gu):
    tm, tn = MOE_TM, MOE_TN_GATE_UP
    nj = D_FF // tn
    blk = lambda i, nu: jnp.minimum(i, nu[0] - 1)
    grid_spec = pltpu.PrefetchScalarGridSpec(
        num_scalar_prefetch=4,
        grid=(nj, MOE_NB),
        in_specs=[pl.BlockSpec((tm, D_MODEL), lambda j, i, be, nu, rn, nx: (blk(i, nu), 0)),
                  pl.BlockSpec(memory_space=pl.ANY),
                  pl.BlockSpec((1, 1, tn), lambda j, i, be, nu, rn, nx: (be[blk(i, nu)], 0, j)),
                  pl.BlockSpec((1, 1, tn), lambda j, i, be, nu, rn, nx: (be[blk(i, nu)], 0, nj + j))],
        out_specs=pl.BlockSpec((tm, tn), lambda j, i, be, nu, rn, nx: (i, j)),
        scratch_shapes=[pltpu.VMEM((2, 2, D_MODEL, tn), F32), pltpu.VMEM((2, D_MODEL, tn), BF16),
                        pltpu.SemaphoreType.DMA((2, 2))],
    )
    return pl.pallas_call(
        _gmm1_kernel,
        grid_spec=grid_spec,
        out_shape=jax.ShapeDtypeStruct((MOE_NB * tm, D_FF), BF16),
        compiler_params=_cparams(("arbitrary", "arbitrary")),
        name="moe_gate_up",
    )(*sched, xs, w_gu, b_gu, b_gu)


def _gmm2_kernel(be_ref, nu_ref, run_ref, nxt_ref, h_ref, w_hbm, bd_ref, y_ref, wbuf, wbf, sem):
    i = pl.program_id(0)

    def copies(e, slot):
        return [pltpu.make_async_copy(w_hbm.at[e], wbuf.at[slot, 0], sem.at[slot])]

    @pl.when(i < nu_ref[0])
    def _():
        _stream_expert_weights(i, be_ref, run_ref, nxt_ref, copies, wbuf, wbf)
        y_ref[...] = jnp.dot(h_ref[...], wbf[0], preferred_element_type=F32) + bd_ref[0]

    @pl.when(i >= nu_ref[0])
    def _():
        y_ref[...] = jnp.zeros_like(y_ref)


def moe_down(sched, hidden, w_dn, b_dn):
    tm = MOE_TM
    blk = lambda i, nu: jnp.minimum(i, nu[0] - 1)
    grid_spec = pltpu.PrefetchScalarGridSpec(
        num_scalar_prefetch=4,
        grid=(MOE_NB,),
        in_specs=[pl.BlockSpec((tm, D_FF), lambda i, be, nu, rn, nx: (blk(i, nu), 0)),
                  pl.BlockSpec(memory_space=pl.ANY),
                  pl.BlockSpec((1, 1, D_MODEL), lambda i, be, nu, rn, nx: (be[blk(i, nu)], 0, 0))],
        out_specs=pl.BlockSpec((tm, D_MODEL), lambda i, be, nu, rn, nx: (i, 0)),
        scratch_shapes=[pltpu.VMEM((2, 1, D_FF, D_MODEL), F32), pltpu.VMEM((1, D_FF, D_MODEL), BF16),
                        pltpu.SemaphoreType.DMA((2,))],
    )
    return pl.pallas_call(
        _gmm2_kernel,
        grid_spec=grid_spec,
        out_shape=jax.ShapeDtypeStruct((MOE_NB * tm, D_MODEL), F32),
        compiler_params=_cparams(("arbitrary",)),
        name="moe_down",
    )(*sched, hidden, w_dn, b_dn)


COMBINE_TM = 128
COMBINE_TILES = R // COMBINE_TM
PROMPT_TILE0 = (LEAD + N_META) // COMBINE_TM
SAMPLE_TILE = P_ROWS // COMBINE_TM


def _combine_kernel(src_ref, h2_ref, tg_ref, fw_ref, yb_ref, yp_ref, ys_ref, buf, sem):
    i = pl.program_id(0)
    tm = COMBINE_TM

    def gather(tile, slot):
        def body(r, carry):
            for k in range(TOP_K):
                s = src_ref[(tile * tm + r) * TOP_K + k]
                pltpu.make_async_copy(yb_ref.at[pl.ds(s, 1), :], buf.at[slot, k, pl.ds(r, 1), :], sem.at[slot]).start()
            return carry

        lax.fori_loop(0, tm, body, 0)

    @pl.when(i == 0)
    def _():
        gather(0, 0)

    @pl.when(i + 1 < COMBINE_TILES)
    def _():
        gather(i + 1, (i + 1) % 2)

    slot = i % 2
    for k in range(TOP_K):
        pltpu.make_async_copy(yb_ref.at[pl.ds(0, tm), :], buf.at[slot, k], sem.at[slot]).wait()
    tg = tg_ref[...]
    f = tg[:, 0:1] * buf[slot, 0]
    for k in range(1, TOP_K):
        f = f + tg[:, k:k + 1] * buf[slot, k]
    out = h2_ref[...] + f
    ms = jnp.mean(out * out, axis=-1, keepdims=True)
    y = out * lax.rsqrt(ms + NORM_EPS) * fw_ref[...]

    @pl.when((i >= PROMPT_TILE0) & (i < SAMPLE_TILE))
    def _():
        yp_ref[...] = y

    @pl.when(i == SAMPLE_TILE)
    def _():
        ys_ref[...] = y


def moe_combine(src_flat, h2, tg, fw, yb):
    tm = COMBINE_TM
    d = D_MODEL
    grid_spec = pltpu.PrefetchScalarGridSpec(
        num_scalar_prefetch=1,
        grid=(COMBINE_TILES,),
        in_specs=[pl.BlockSpec((tm, d), lambda i, s: (i, 0)), pl.BlockSpec((tm, LANES), lambda i, s: (i, 0)),
                  pl.BlockSpec((1, d), lambda i, s: (0, 0)), pl.BlockSpec(memory_space=pl.ANY)],
        out_specs=[pl.BlockSpec((tm, d), lambda i, s: (jnp.clip(i - PROMPT_TILE0, 0, SEQ // tm - 1), 0)),
                   pl.BlockSpec((tm, d), lambda i, s: (0, 0))],
        scratch_shapes=[pltpu.VMEM((2, TOP_K, tm, d), F32), pltpu.SemaphoreType.DMA((2,))],
    )
    return pl.pallas_call(
        _combine_kernel,
        grid_spec=grid_spec,
        out_shape=[jax.ShapeDtypeStruct((SEQ, d), F32), jax.ShapeDtypeStruct((S_ROWS, d), F32)],
        compiler_params=_cparams(("arbitrary",)),
        name="moe_combine",
    )(src_flat, h2, tg, fw, yb)


def _rope_tables():
    half = ROT_DIM // 2
    pos_p = jnp.maximum(jnp.arange(P_ROWS) - LEAD, 0)
    pos_s = SEQ + jnp.arange(S_ROWS) % DEC_SEQ
    pos = jnp.concatenate([pos_p, pos_s])
    inv_freq = 1.0 / (ROPE_THETA ** (jnp.arange(half, dtype=F32) * (2.0 / ROT_DIM)))
    ang = pos.astype(F32)[:, None] * inv_freq[None, :]
    cos, sin = jnp.cos(ang), jnp.sin(ang)
    zeros = jnp.zeros((R, LANES - ROT_DIM), F32)
    cos_t = jnp.concatenate([cos, cos, jnp.ones((R, LANES - ROT_DIM), F32)], axis=1)
    sin_a = jnp.concatenate([jnp.zeros_like(sin), sin, zeros], axis=1)
    sin_b = jnp.concatenate([-sin, jnp.zeros_like(sin), zeros], axis=1)
    return cos_t, sin_a, sin_b


def _pad_lanes(v, fill=0.0):
    v = v.reshape(1, -1)
    return jnp.concatenate([v, jnp.full((1, LANES - v.shape[1]), fill, v.dtype)], axis=1)


def kernel(x_prompt, x_sample, cache_k, cache_v, state_ssm, state_conv, page_table, meta_tokens,
           norm_mix_w, w_in, conv_w, conv_b, dt_bias, a_log, d_skip, ssd_norm_w, w_ssd_out,
           lambda_q1, lambda_k1, lambda_q2, lambda_k2, subln_w, w_attn_out, w_o, norm_ffn_w,
           router_w, router_b, w_gate_up, b_gate_up, w_down, b_down, final_norm_w):
    l = 0
    d = D_MODEL
    x_all = jnp.concatenate([jnp.zeros((LEAD, d), F32), meta_tokens, x_prompt[0], x_sample.reshape(S_ROWS, d)], axis=0)
    xn = rmsnorm_rows(x_all, norm_mix_w[l])
    w = w_in[l]
    o_z, o_xbc, o_dt, o_q, o_k, o_v, o_ga, o_gb = (int(o) for o in np.cumsum(
        (0, SSD_INNER, CONV_DIM, SSD_HEADS, 2048, 1024, 1024, d)))
    w_perm = jnp.concatenate([w[:, o_z:o_xbc], w[:, o_ga:o_gb + d], w[:, o_xbc:o_dt], w[:, o_q:o_ga]], axis=1).astype(BF16)
    w_dt = jnp.concatenate([w[:, o_dt:o_q], jnp.zeros((d, LANES - SSD_HEADS), F32)], axis=1).astype(BF16)
    cos_t, sin_a, sin_b = _rope_tables()
    proj, kb, vt = in_projection(xn, w_perm, cos_t, sin_a, sin_b)
    dtraw = dt_projection(xn, w_dt)

    e_mat = np.zeros((LANES, SSD_INNER), np.float32)
    for h in range(SSD_HEADS):
        e_mat[h, h * SSD_HEAD_DIM:(h + 1) * SSD_HEAD_DIM] = 1.0
    ssd_params = (conv_w[l], conv_b[l].reshape(1, CONV_DIM), _pad_lanes(dt_bias[l]), _pad_lanes(a_log[l]),
                  jnp.repeat(d_skip[l], SSD_HEAD_DIM).reshape(1, SSD_INNER), ssd_norm_w[l].reshape(1, SSD_INNER),
                  jnp.asarray(e_mat, BF16))
    yzn_p, ssm_p = ssd_prompt(proj, dtraw, ssd_params)
    conv_prev = jnp.concatenate([jnp.zeros((DEC_BATCH, SUBLANES - (SSD_CONV - 1), CONV_DIM), F32), state_conv[l]], axis=1)
    yzn_s, ssm_s = ssd_sample(proj, dtraw, conv_prev, state_ssm[l].reshape(DEC_BATCH, SSD_INNER, SSD_STATE), ssd_params)
    yzn = jnp.concatenate([yzn_p, yzn_s], axis=0)

    k_rows = proj[:, PCOL_K:PCOL_V]
    v_rows = proj[:, PCOL_V:P_COLS]
    lam_params = (lambda_q1[l].reshape(1, -1), lambda_k1[l].reshape(1, -1), lambda_q2[l].reshape(1, -1),
                  lambda_k2[l].reshape(1, -1))
    subw = subln_w[l].reshape(1, ATT_V_DIM)
    vt_ext = jnp.concatenate([vt.reshape(ATT_KV_HEADS, ATT_V_DIM, R),
                              jnp.ones((ATT_KV_HEADS, ATT_VT_ROWS - ATT_V_DIM, R), BF16)], axis=1)
    on_p = attn_prompt(proj, kb, vt_ext.reshape(ATT_KV_HEADS * ATT_VT_ROWS, R), lam_params,
                       subln_w[l].reshape(ATT_V_DIM, 1))

    q_s = proj[P_ROWS:, PCOL_Q:PCOL_K].reshape(DEC_BATCH, DEC_SEQ, ATT_KV_HEADS, ATT_REP, 2, ATT_HEAD_DIM)
    q_s = jnp.transpose(q_s, (0, 2, 4, 1, 3, 5)).reshape(DEC_BATCH, ATT_S_ROWS, ATT_HEAD_DIM).astype(BF16)
    n_phys = cache_k.shape[1]
    kc = cache_k[l].reshape(n_phys, KROWS, ATT_HEAD_DIM)
    pad_new = lambda u: jnp.concatenate([u.reshape(DEC_BATCH, DEC_SEQ, -1),
                                         jnp.zeros((DEC_BATCH, PAGE_SIZE - DEC_SEQ, u.shape[-1]), F32)], axis=1)
    k_new = pad_new(k_rows[P_ROWS:]).reshape(DEC_BATCH, KROWS, ATT_HEAD_DIM)
    v_new = pad_new(v_rows[P_ROWS:]).reshape(1, DEC_BATCH, PAGE_SIZE, ATT_KV_HEADS, ATT_V_DIM)
    o_s = attn_sample(page_table.reshape(-1), q_s, kc, cache_v[l:l + 1], k_new, v_new, lam_params, subw)
    o_s = o_s.reshape(DEC_BATCH, ATT_KV_HEADS, DEC_SEQ, ATT_REP, ATT_V_DIM)
    o_s = jnp.transpose(o_s, (0, 2, 1, 3, 4)).reshape(S_ROWS, ATT_HEADS * ATT_V_DIM).astype(BF16)
    on = lax.dynamic_update_slice(on_p, o_s, (P_ROWS, 0))

    mix = mix_branches(yzn, on, proj, w_ssd_out[l].astype(BF16), w_attn_out[l].astype(BF16))
    rw = jnp.concatenate([router_w[l], jnp.zeros((d, LANES - N_EXPERTS), F32)], axis=1)
    rb = _pad_lanes(router_b[l], NEG)
    h2, n_rows, top_i, top_g = post_mixer(mix, x_all, w_o[l].astype(BF16), norm_ffn_w[l].reshape(1, d), rw, rb)

    rank, counts = expert_ranks(top_i)
    counts = counts[0, :N_EXPERTS]
    padded = (counts + MOE_TM - 1) // MOE_TM * MOE_TM
    pad_end = jnp.cumsum(padded)
    pad_start = pad_end - padded
    dest = pad_start[top_i[:, :TOP_K]] + rank[:, :TOP_K]
    is_tok = (jnp.arange(R) >= LEAD)[:, None]
    dump = MOE_NB * MOE_TM + jnp.arange(R * TOP_K).reshape(R, TOP_K)
    dest_scatter = jnp.where(is_tok, dest, dump).astype(I32).reshape(-1)
    dest_gather = jnp.where(is_tok, dest, 0).astype(I32).reshape(-1)
    block_start = jnp.arange(MOE_NB) * MOE_TM
    block_e = jnp.minimum(jnp.sum(pad_end[None, :] <= block_start[:, None], axis=1), N_EXPERTS - 1).astype(I32)
    n_used = (pad_end[-1:] // MOE_TM).astype(I32)
    ex = jnp.arange(N_EXPERTS)
    nonempty = padded > 0
    run_of_expert = jnp.cumsum(nonempty) - 1
    later = (ex[None, :] > ex[:, None]) & nonempty[None, :]
    next_expert = jnp.min(jnp.where(later, ex[None, :], N_EXPERTS), axis=1)
    next_expert = jnp.where(next_expert < N_EXPERTS, next_expert, -1)
    sched = (block_e, n_used, run_of_expert[block_e].astype(I32), next_expert[block_e].astype(I32))
    xs = moe_dispatch(dest_scatter, n_rows, jnp.zeros((MOE_ROWS, d), F32))
    hidden = moe_gate_up(sched, xs, w_gate_up[l], b_gate_up[l].reshape(N_EXPERTS, 1, 2 * D_FF))
    yb = moe_down(sched, hidden, w_down[l], b_down[l].reshape(N_EXPERTS, 1, d))
    y_p, y_s = moe_combine(dest_gather, h2, top_g, final_norm_w.reshape(1, d), yb)

    tok0 = LEAD
    y_prompt = y_p[None]
    y_sample = y_s.reshape(DEC_BATCH, DEC_SEQ, d)
    k_prompt = k_rows[tok0:P_ROWS].reshape(1, 1, SEQ + N_META, ATT_KV_HEADS, 2, ATT_HEAD_DIM)
    v_prompt = v_rows[tok0:P_ROWS].reshape(1, 1, SEQ + N_META, ATT_KV_HEADS, ATT_V_DIM)
    k_sample = k_rows[P_ROWS:].reshape(1, DEC_BATCH, DEC_SEQ, ATT_KV_HEADS, 2, ATT_HEAD_DIM)
    v_sample = v_rows[P_ROWS:].reshape(1, DEC_BATCH, DEC_SEQ, ATT_KV_HEADS, ATT_V_DIM)
    ssm_prompt = ssm_p.reshape(1, 1, SSD_HEADS, SSD_HEAD_DIM, SSD_STATE)
    ssm_sample = ssm_s.reshape(1, DEC_BATCH, SSD_HEADS, SSD_HEAD_DIM, SSD_STATE)
    xbc = proj[:, PCOL_XBC:PCOL_XBC + CONV_DIM]
    conv_prompt = xbc[P_ROWS - (SSD_CONV - 1):P_ROWS][None, None]
    xbc_s = xbc[P_ROWS:].reshape(DEC_BATCH, DEC_SEQ, CONV_DIM)
    conv_sample = jnp.concatenate([state_conv[l], xbc_s], axis=1)[:, -(SSD_CONV - 1):][None]
    return (y_prompt, y_sample, k_prompt, v_prompt, k_sample, v_sample, ssm_prompt, ssm_sample, conv_prompt, conv_sample)
```

```python
import functools
import math

import jax
import jax.numpy as jnp
import numpy as np
from jax import lax
from jax.experimental import pallas as pl
from jax.experimental.pallas import tpu as pltpu

F32 = jnp.float32
BF16 = jnp.bfloat16
I32 = jnp.int32
HI = lax.Precision.HIGHEST

D_MODEL = 2048
SEQ = 8192
DEC_BATCH = 32
DEC_SEQ = 4
PAGE_SIZE = 128
N_META = 16
NORM_EPS = 1e-5
SSD_INNER = D_MODEL
SSD_HEAD_DIM = 64
SSD_HEADS = SSD_INNER // SSD_HEAD_DIM
SSD_GROUPS = 4
SSD_REP = SSD_HEADS // SSD_GROUPS
SSD_STATE = 128
SSD_CONV = 4
SSD_CHUNK = 128
CONV_DIM = SSD_INNER + 2 * SSD_GROUPS * SSD_STATE
ATT_HEADS = 8
ATT_KV_HEADS = 4
ATT_REP = ATT_HEADS // ATT_KV_HEADS
ATT_HEAD_DIM = D_MODEL // ATT_HEADS // 2
ATT_V_DIM = 2 * ATT_HEAD_DIM
ATT_SCALE = ATT_HEAD_DIM ** -0.5
ROT_DIM = ATT_HEAD_DIM // 4
ROPE_THETA = 500000.0
N_EXPERTS = 32
TOP_K = 4
D_FF = D_MODEL
SWIGLU_LIMIT = 7.0
SWIGLU_ALPHA = 1.702
LAM_INIT = 0.8 - 0.6 * math.exp(-0.3 * 0)

LEAD = (-N_META) % SSD_CHUNK
P_ROWS = LEAD + N_META + SEQ
S_ROWS = DEC_BATCH * DEC_SEQ
R = P_ROWS + S_ROWS
N_CHUNKS = P_ROWS // SSD_CHUNK

LANES = 128
SUBLANES = 8
VMEM_LIMIT = 56 * 1024 * 1024

PCOL_Z, PCOL_GA, PCOL_GB, PCOL_XBC, PCOL_Q, PCOL_K, PCOL_V = 0, 2048, 4096, 6144, 9216, 11264, 12288
P_COLS = 13312
PROJ_TN = 1024
ROPE_BLOCKS = (PCOL_Q // PROJ_TN, PCOL_V // PROJ_TN)

MOE_TM = 256
MOE_NB = -(-(R * TOP_K) // MOE_TM) + N_EXPERTS
MOE_DUMP_BLOCKS = -(-(LEAD * TOP_K) // MOE_TM)
MOE_ROWS = (MOE_NB + MOE_DUMP_BLOCKS) * MOE_TM
MOE_TN_GATE_UP = 1024
NEG = -1e30


def _cparams(sem, vmem=VMEM_LIMIT):
    return pltpu.CompilerParams(dimension_semantics=sem, vmem_limit_bytes=vmem)


def _sigmoid(x):
    return 1.0 / (1.0 + jnp.exp(-x))


def _rmsnorm_kernel(x_ref, w_ref, o_ref):
    x = x_ref[...]
    ms = jnp.mean(x * x, axis=-1, keepdims=True)
    o_ref[...] = (x * lax.rsqrt(ms + NORM_EPS) * w_ref[...]).astype(o_ref.dtype)


def rmsnorm_rows(x, w, tm=768):
    rows, d = x.shape
    return pl.pallas_call(
        _rmsnorm_kernel,
        grid=(rows // tm,),
        in_specs=[pl.BlockSpec((tm, d), lambda i: (i, 0)), pl.BlockSpec((1, d), lambda i: (0, 0))],
        out_specs=pl.BlockSpec((tm, d), lambda i: (i, 0)),
        out_shape=jax.ShapeDtypeStruct((rows, d), BF16),
        compiler_params=_cparams(("parallel",)),
        name="rmsnorm_rows",
    )(x, w.reshape(1, d))


def _rope(blk, c, sa, sb):
    return blk * c + pltpu.roll(blk, ROT_DIM // 2, axis=1) * sa + pltpu.roll(blk, LANES - ROT_DIM // 2, axis=1) * sb


def _proj_kernel(x_ref, w_ref, c_ref, sa_ref, sb_ref, o_ref, kb_ref, vt_ref):
    j = pl.program_id(1)
    acc = jnp.dot(x_ref[...], w_ref[...], preferred_element_type=F32)
    is_rope = (j >= ROPE_BLOCKS[0]) & (j < ROPE_BLOCKS[1])

    @pl.when(is_rope)
    def _():
        c, sa, sb = c_ref[...], sa_ref[...], sb_ref[...]
        for g in range(acc.shape[1] // LANES):
            sl = slice(g * LANES, (g + 1) * LANES)
            o_ref[:, sl] = _rope(acc[:, sl], c, sa, sb)

    @pl.when(jnp.logical_not(is_rope))
    def _():
        o_ref[...] = acc

    @pl.when(j == PCOL_K // PROJ_TN)
    def _():
        kb_ref[...] = o_ref[...].astype(kb_ref.dtype)

    @pl.when(j == PCOL_V // PROJ_TN)
    def _():
        vt_ref[...] = acc.T.astype(vt_ref.dtype)


def in_projection(xn, w_bf, cos_t, sin_a, sin_b, tm=1408):
    rows, d = xn.shape
    tn = PROJ_TN
    tab = pl.BlockSpec((tm, LANES), lambda i, j: (i, 0))
    return pl.pallas_call(
        _proj_kernel,
        grid=(rows // tm, P_COLS // tn),
        in_specs=[pl.BlockSpec((tm, d), lambda i, j: (i, 0)), pl.BlockSpec((d, tn), lambda i, j: (0, j)), tab, tab, tab],
        out_specs=[pl.BlockSpec((tm, tn), lambda i, j: (i, j)),
                   pl.BlockSpec((tm, tn), lambda i, j: (i, 0)),
                   pl.BlockSpec((tn, tm), lambda i, j: (0, i))],
        out_shape=[jax.ShapeDtypeStruct((rows, P_COLS), F32),
                   jax.ShapeDtypeStruct((rows, tn), BF16),
                   jax.ShapeDtypeStruct((tn, rows), BF16)],
        compiler_params=_cparams(("parallel", "arbitrary")),
        name="in_projection",
    )(xn, w_bf, cos_t, sin_a, sin_b)


def _mm_kernel(x_ref, w_ref, o_ref):
    o_ref[...] = jnp.dot(x_ref[...], w_ref[...], preferred_element_type=F32).astype(o_ref.dtype)


def dt_projection(xn, w_dt, tm=1056):
    rows, d = xn.shape
    return pl.pallas_call(
        _mm_kernel,
        grid=(rows // tm,),
        in_specs=[pl.BlockSpec((tm, d), lambda i: (i, 0)), pl.BlockSpec((d, LANES), lambda i: (0, 0))],
        out_specs=pl.BlockSpec((tm, LANES), lambda i: (i, 0)),
        out_shape=jax.ShapeDtypeStruct((rows, LANES), F32),
        compiler_params=_cparams(("parallel",)),
        name="dt_projection",
    )(xn, w_dt)


def _ssd_chunk(xprev, xcur, dtraw, z, valid, s_ref, cw_ref, cb_ref, dtb_ref, alog_ref, dsk_ref, nw_ref, e_ref,
               state_dot_precision):
    L = xcur.shape[0]
    ext = jnp.concatenate([xprev, xcur], axis=0)
    off = SUBLANES - (SSD_CONV - 1)
    acc = cb_ref[...]
    for k in range(SSD_CONV):
        acc = acc + ext[off + k:off + k + L] * cw_ref[k:k + 1, :]
    act = acc * _sigmoid(acc)
    xs = act[:, :SSD_INNER]
    nb = SSD_GROUPS * SSD_STATE
    bm = act[:, SSD_INNER:SSD_INNER + nb].astype(BF16)
    cm = act[:, SSD_INNER + nb:].astype(BF16)

    dpre = dtraw + dtb_ref[...]
    dt = jnp.maximum(dpre, 0.0) + jnp.log1p(jnp.exp(-jnp.abs(dpre)))
    dt = jnp.where(valid, dt, 0.0)
    da = dt * (-jnp.exp(alog_ref[...]))

    li = lax.broadcasted_iota(I32, (L, L), 0)
    si = lax.broadcasted_iota(I32, (L, L), 1)
    tril = li >= si
    eye = li == si
    a_cs = jnp.dot(tril.astype(F32), da, precision=HI, preferred_element_type=F32)
    a_tot = a_cs[L - 1:L, :]
    e_mat = e_ref[...]

    def expand(v):
        out = None
        for _ in range(3):
            piece = v.astype(BF16)
            term = jnp.dot(piece, e_mat, preferred_element_type=F32)
            out = term if out is None else out + term
            v = v - piece.astype(F32)
        return out

    dt_x = expand(dt)
    eacs_x = expand(jnp.exp(a_cs))
    dte_x = expand(jnp.exp(a_tot - a_cs))
    etot = jnp.exp(a_tot)

    xdt = xs * dt_x
    xdt_b = xdt.astype(BF16)
    xde = xdt * dte_x
    gw = SSD_REP * SSD_HEAD_DIM
    lane = lax.broadcasted_iota(I32, (L, LANES), 1)
    nt = (((1,), (1,)), ((), ()))
    tn = (((0,), (0,)), ((), ()))
    y_groups = []
    for g in range(SSD_GROUPS):
        bg = bm[:, g * SSD_STATE:(g + 1) * SSD_STATE]
        cg = cm[:, g * SSD_STATE:(g + 1) * SSD_STATE]
        cb = lax.dot_general(cg, bg, nt, preferred_element_type=F32)
        s_g = s_ref[g * gw:(g + 1) * gw, :]
        y_off = lax.dot_general(cg, s_g.astype(BF16), nt, preferred_element_type=F32)
        if state_dot_precision is None:
            upd = lax.dot_general(xde[:, g * gw:(g + 1) * gw].astype(BF16), bg, tn, preferred_element_type=F32)
        else:
            upd = lax.dot_general(xde[:, g * gw:(g + 1) * gw], act[:, SSD_INNER + g * SSD_STATE:SSD_INNER + (g + 1) * SSD_STATE],
                                  tn, precision=state_dot_precision, preferred_element_type=F32)
        pairs = []
        decs = []
        for q in range(SSD_REP // 2):
            yd = []
            for r in range(2):
                h = g * SSD_REP + 2 * q + r
                colb = jnp.broadcast_to(a_cs[:, h:h + 1], (L, L))
                rowb = jnp.sum(jnp.where(eye, colb, 0.0), axis=0, keepdims=True)
                lm = jnp.where(tril, jnp.exp(jnp.minimum(colb - rowb, 0.0)), 0.0)
                m = (cb * lm).astype(BF16)
                c0 = (g * SSD_REP + 2 * q) * SSD_HEAD_DIM
                yd.append(jnp.dot(m, xdt_b[:, c0:c0 + LANES], preferred_element_type=F32))
                decs.append(jnp.broadcast_to(etot[:, h:h + 1], (SSD_HEAD_DIM, SSD_STATE)))
            pairs.append(jnp.where(lane < SSD_HEAD_DIM, yd[0], yd[1]))
        y_g = jnp.concatenate(pairs, axis=1) + y_off * eacs_x[:, g * gw:(g + 1) * gw]
        y_groups.append(y_g)
        s_ref[g * gw:(g + 1) * gw, :] = jnp.concatenate(decs, axis=0) * s_g + upd
    y = jnp.concatenate(y_groups, axis=1) + dsk_ref[...] * xs
    yz = y * (z * _sigmoid(z))
    outs = []
    for g in range(SSD_GROUPS):
        yg = yz[:, g * gw:(g + 1) * gw]
        ms = jnp.mean(yg * yg, axis=-1, keepdims=True)
        outs.append(yg * lax.rsqrt(ms + NORM_EPS) * nw_ref[:, g * gw:(g + 1) * gw])
    return jnp.concatenate(outs, axis=1)


def _ssd_prompt_kernel(xprev_ref, xcur_ref, dt_ref, z_ref, cw_ref, cb_ref, dtb_ref, alog_ref, dsk_ref, nw_ref, e_ref,
                       y_ref, s_ref):
    c = pl.program_id(0)

    @pl.when(c == 0)
    def _():
        s_ref[...] = jnp.zeros_like(s_ref)

    xprev = jnp.where(c == 0, 0.0, xprev_ref[...])
    row = c * SSD_CHUNK + lax.broadcasted_iota(I32, (SSD_CHUNK, 1), 0)
    y = _ssd_chunk(xprev, xcur_ref[...], dt_ref[...], z_ref[...], row >= LEAD, s_ref,
                   cw_ref, cb_ref, dtb_ref, alog_ref, dsk_ref, nw_ref, e_ref, None)
    y_ref[...] = y.astype(y_ref.dtype)


def _full(shape):
    return pl.BlockSpec(shape, lambda *_: (0,) * len(shape))


def ssd_prompt(proj, dtraw, params):
    L = SSD_CHUNK
    xb = PCOL_XBC // CONV_DIM
    in_specs = [
        pl.BlockSpec((SUBLANES, CONV_DIM), lambda c: (jnp.maximum(c * (L // SUBLANES) - 1, 0), xb)),
        pl.BlockSpec((L, CONV_DIM), lambda c: (c, xb)),
        pl.BlockSpec((L, LANES), lambda c: (c, 0)),
        pl.BlockSpec((L, SSD_INNER), lambda c: (c, PCOL_Z // SSD_INNER)),
    ] + [_full(p.shape) for p in params]
    return pl.pallas_call(
        _ssd_prompt_kernel,
        grid=(N_CHUNKS,),
        in_specs=in_specs,
        out_specs=[pl.BlockSpec((L, SSD_INNER), lambda c: (c, 0)), _full((SSD_INNER, SSD_STATE))],
        out_shape=[jax.ShapeDtypeStruct((P_ROWS, SSD_INNER), BF16), jax.ShapeDtypeStruct((SSD_INNER, SSD_STATE), F32)],
        compiler_params=_cparams(("arbitrary",)),
        name="ssd_prompt",
    )(proj, proj, dtraw, proj, *params)


SSD_S_NB = 4


def _ssd_sample_kernel(xprev_ref, xcur_ref, dt_ref, z_ref, sin_ref, cw_ref, cb_ref, dtb_ref, alog_ref, dsk_ref, nw_ref,
                       e_ref, y_ref, sout_ref):
    L = SUBLANES
    valid = lax.broadcasted_iota(I32, (L, 1), 0) < DEC_SEQ
    ys = []
    for b in range(SSD_S_NB):
        rows = slice(b * DEC_SEQ, (b + 1) * DEC_SEQ)
        pad = lambda u: jnp.concatenate([u, jnp.zeros((L - DEC_SEQ, u.shape[1]), u.dtype)], axis=0)
        sout_ref[b] = sin_ref[b]
        y = _ssd_chunk(xprev_ref[b], pad(xcur_ref[rows, :]), pad(dt_ref[rows, :]), pad(z_ref[rows, :]), valid,
                       sout_ref.at[b], cw_ref, cb_ref, dtb_ref, alog_ref, dsk_ref, nw_ref, e_ref, HI)
        ys.append(y[:DEC_SEQ])
    y_ref[...] = jnp.concatenate(ys, axis=0).astype(y_ref.dtype)


def ssd_sample(proj, dtraw, conv_prev, state, params):
    nb = SSD_S_NB
    rb = nb * DEC_SEQ
    r0 = P_ROWS // rb
    in_specs = [
        pl.BlockSpec((nb, SUBLANES, CONV_DIM), lambda i: (i, 0, 0)),
        pl.BlockSpec((rb, CONV_DIM), lambda i: (r0 + i, PCOL_XBC // CONV_DIM)),
        pl.BlockSpec((rb, LANES), lambda i: (r0 + i, 0)),
        pl.BlockSpec((rb, SSD_INNER), lambda i: (r0 + i, PCOL_Z // SSD_INNER)),
        pl.BlockSpec((nb, SSD_INNER, SSD_STATE), lambda i: (i, 0, 0)),
    ] + [_full(p.shape) for p in params]
    return pl.pallas_call(
        _ssd_sample_kernel,
        grid=(DEC_BATCH // nb,),
        in_specs=in_specs,
        out_specs=[pl.BlockSpec((rb, SSD_INNER), lambda i: (i, 0)),
                   pl.BlockSpec((nb, SSD_INNER, SSD_STATE), lambda i: (i, 0, 0))],
        out_shape=[jax.ShapeDtypeStruct((S_ROWS, SSD_INNER), BF16),
                   jax.ShapeDtypeStruct((DEC_BATCH, SSD_INNER, SSD_STATE), F32)],
        compiler_params=_cparams(("parallel",)),
        name="ssd_sample",
    )(conv_prev, proj, dtraw, proj, state, *params)


def _lambda(lq1_ref, lk1_ref, lq2_ref, lk2_ref):
    s1 = jnp.sum(lq1_ref[...] * lk1_ref[...], axis=-1, keepdims=True)
    s2 = jnp.sum(lq2_ref[...] * lk2_ref[...], axis=-1, keepdims=True)
    return jnp.exp(s1) - jnp.exp(s2) + LAM_INIT


def _subnorm(o, subw_ref):
    ms = jnp.mean(o * o, axis=-1, keepdims=True)
    return (o * lax.rsqrt(ms + NORM_EPS) * subw_ref[...]) * (1.0 - LAM_INIT)


ATT_TQ = 256
ATT_TK = 768
BF16_SUBLANES = 16
ATT_VT_ROWS = ATT_V_DIM + BF16_SUBLANES
EXP2_SCALE = ATT_SCALE * math.log2(math.e)


def _attn_prompt_kernel(q_ref, k_ref, vt_ref, lq1_ref, lk1_ref, lq2_ref, lk2_ref, subw_ref, o_ref, m_s, acc_s):
    i = pl.program_id(1)
    tq, tk, hd = ATT_TQ, ATT_TK, ATT_HEAD_DIM
    q = q_ref[...]
    qc = [jnp.concatenate([q[:, r * 2 * hd + c * hd: r * 2 * hd + (c + 1) * hd] for r in range(ATT_REP)], axis=0).astype(BF16)
          for c in range(2)]
    m_s[...] = jnp.full_like(m_s, NEG)
    acc_s[...] = jnp.zeros_like(acc_s)
    nt = (((1,), (1,)), ((), ()))
    n_blocks = (i * tq + tq - 1) // tk + 1

    def step(j, mask):
        k0 = pl.multiple_of(j * tk, tk)
        kblk = k_ref[pl.ds(k0, tk), :]
        vt = vt_ref[:, pl.ds(k0, tk)]
        sts = [lax.dot_general(kblk[:, c * hd:(c + 1) * hd], qc[c], nt, preferred_element_type=F32) for c in range(2)]
        ps, alphas = [], []
        for c in range(2):
            st = sts[c]
            if mask == "causal":
                kidx = j * tk + lax.broadcasted_iota(I32, st.shape, 0)
                qidx = i * tq + lax.broadcasted_iota(I32, st.shape, 1) % tq
                ok = (kidx <= qidx) & ((kidx >= LEAD) | (qidx < LEAD))
                st = jnp.where(ok, st, NEG)
            elif mask == "lead":
                st = st + jnp.where(lax.broadcasted_iota(I32, (tk, 1), 0) < LEAD, NEG, 0.0)
            m_old = m_s[c]
            m_new = jnp.maximum(m_old, jnp.max(st, axis=0, keepdims=True))
            alphas.append(jnp.exp2((m_old - m_new) * EXP2_SCALE))
            ps.append(jnp.exp2((st - m_new) * EXP2_SCALE).astype(BF16))
            m_s[c] = m_new
        for c in range(2):
            acc_s[c] = alphas[c] * acc_s[c] + jnp.dot(vt, ps[c], preferred_element_type=F32)

    @pl.when(n_blocks == 1)
    def _():
        step(0, "causal")

    @pl.when(n_blocks > 1)
    def _():
        step(0, "lead")

        def body(j, carry):
            step(j, None)
            return carry

        lax.fori_loop(1, n_blocks - 1, body, 0)
        step(n_blocks - 1, "causal")

    lam = _lambda(lq1_ref, lk1_ref, lq2_ref, lk2_ref)
    for r in range(ATT_REP):
        cols = slice(r * tq, (r + 1) * tq)
        o0 = acc_s[0, :ATT_V_DIM, cols] / acc_s[0, ATT_V_DIM:ATT_V_DIM + 1, cols]
        o1 = acc_s[1, :ATT_V_DIM, cols] / acc_s[1, ATT_V_DIM:ATT_V_DIM + 1, cols]
        ot = o0 - lam * o1
        ms = jnp.mean(ot * ot, axis=0, keepdims=True)
        ot = (ot * lax.rsqrt(ms + NORM_EPS) * subw_ref[...]) * (1.0 - LAM_INIT)
        o_ref[:, r * ATT_V_DIM:(r + 1) * ATT_V_DIM] = ot.T.astype(o_ref.dtype)


def attn_prompt(proj, kb, vt_ext, lam_params, subw_col):
    tq = ATT_TQ
    gq = ATT_REP * 2 * ATT_HEAD_DIM
    vec = _full((1, ATT_HEAD_DIM))
    return pl.pallas_call(
        _attn_prompt_kernel,
        grid=(ATT_KV_HEADS, R // tq),
        in_specs=[pl.BlockSpec((tq, gq), lambda g, i: (i, PCOL_Q // gq + g)),
                  pl.BlockSpec((R, ATT_V_DIM), lambda g, i: (0, g)),
                  pl.BlockSpec((ATT_VT_ROWS, R), lambda g, i: (g, 0)),
                  vec, vec, vec, vec, _full((ATT_V_DIM, 1))],
        out_specs=pl.BlockSpec((tq, gq), lambda g, i: (i, g)),
        out_shape=jax.ShapeDtypeStruct((R, ATT_HEADS * ATT_V_DIM), BF16),
        scratch_shapes=[pltpu.VMEM((2, 1, ATT_REP * tq), F32),
                        pltpu.VMEM((2, ATT_VT_ROWS, ATT_REP * tq), F32)],
        compiler_params=_cparams(("parallel", "arbitrary")),
        name="attn_prompt",
    )(proj, kb, vt_ext, *lam_params, subw_col)


N_PAGES = SEQ // PAGE_SIZE
ATT_S_ROWS = ATT_KV_HEADS * 2 * DEC_SEQ * ATT_REP
ATT_S_GROUP = 2 * DEC_SEQ * ATT_REP
ATT_S_PPS = 8
ATT_S_STEPS = N_PAGES // ATT_S_PPS
KROWS = PAGE_SIZE * ATT_KV_HEADS * 2
V_TILE = ATT_V_DIM
V_LANE_TILES = ATT_V_DIM // V_TILE


def _attn_sample_kernel(pt_ref, q_ref, *refs):
    npg = ATT_S_PPS
    kc_refs = refs[:npg]
    vc_refs = [refs[npg + V_LANE_TILES * p: npg + V_LANE_TILES * (p + 1)] for p in range(npg)]
    base = npg * (1 + V_LANE_TILES)
    kn_ref = refs[base]
    vn_refs = refs[base + 1: base + 1 + V_LANE_TILES]
    x_ref, lq1_ref, lk1_ref, lq2_ref, lk2_ref, subw_ref, o_ref, m_s, l_s, acc_s = refs[base + 1 + V_LANE_TILES:]
    j = pl.program_id(1)
    nt = (((1,), (1,)), ((), ()))
    nhc = ATT_KV_HEADS * 2
    rows_hc = DEC_SEQ * ATT_REP

    @pl.when(j == 0)
    def _():
        m_s[...] = jnp.full_like(m_s, NEG)
        l_s[...] = jnp.zeros_like(l_s)
        acc_s[...] = jnp.zeros_like(acc_s)

    def step(k_pages, v_pages, masked):
        q = q_ref[0]
        cols = []
        for k_src in k_pages:
            ss = []
            for hc in range(nhc):
                khc = k_src[pl.ds(hc, PAGE_SIZE, stride=nhc), :].astype(BF16)
                ss.append(lax.dot_general(q[hc * rows_hc:(hc + 1) * rows_hc, :], khc, nt, preferred_element_type=F32))
            cols.append(jnp.concatenate(ss, axis=0))
        s = jnp.concatenate(cols, axis=1) if len(cols) > 1 else cols[0]
        if masked:
            t_q = (lax.broadcasted_iota(I32, s.shape, 0) % rows_hc) // ATT_REP
            t_k = lax.broadcasted_iota(I32, s.shape, 1)
            s = jnp.where(t_k <= t_q, s, NEG)
        m_old = m_s[...]
        m_new = jnp.maximum(m_old, jnp.max(s, axis=1, keepdims=True))
        alpha = jnp.exp2((m_old - m_new) * EXP2_SCALE)
        p = jnp.exp2((s - m_new[:, :1]) * EXP2_SCALE)
        l_s[...] = alpha * l_s[...] + jnp.sum(p, axis=1, keepdims=True)
        pb = p.astype(BF16)
        head_of_col = lax.broadcasted_iota(I32, (ATT_S_ROWS, PAGE_SIZE * ATT_KV_HEADS), 1) % ATT_KV_HEADS
        head_of_row = lax.broadcasted_iota(I32, (ATT_S_ROWS, PAGE_SIZE * ATT_KV_HEADS), 0) // ATT_S_GROUP
        own_head = head_of_col == head_of_row
        pv = None
        for pi, v_tiles in enumerate(v_pages):
            v2 = jnp.concatenate([v[...].reshape(PAGE_SIZE * ATT_KV_HEADS, V_TILE) for v in v_tiles], axis=1).astype(BF16)
            spread = jnp.dot(pb[:, pi * PAGE_SIZE:(pi + 1) * PAGE_SIZE], x_ref[...], preferred_element_type=F32)
            pe = jnp.where(own_head, spread, 0.0).astype(BF16)
            d = jnp.dot(pe, v2, preferred_element_type=F32)
            pv = d if pv is None else pv + d
        acc_s[...] = alpha[:, :1] * acc_s[...] + pv
        m_s[...] = m_new

    @pl.when(j < ATT_S_STEPS)
    def _():
        step(kc_refs, vc_refs, False)

    @pl.when(j == ATT_S_STEPS)
    def _():
        step([kn_ref], [vn_refs], True)
        lam = _lambda(lq1_ref, lk1_ref, lq2_ref, lk2_ref)
        for g in range(ATT_KV_HEADS):
            r0 = g * ATT_S_GROUP
            o0 = acc_s[r0:r0 + rows_hc, :] / l_s[r0:r0 + rows_hc, :1]
            o1 = acc_s[r0 + rows_hc:r0 + 2 * rows_hc, :] / l_s[r0 + rows_hc:r0 + 2 * rows_hc, :1]
            o_ref[0, g] = _subnorm(o0 - lam * o1, subw_ref)


def attn_sample(page_table, q, kc, vc, knew, vnew, lam_params, subw):
    vec = _full((1, ATT_HEAD_DIM))
    kblk = (None, KROWS, ATT_HEAD_DIM)
    vblk = (None, None, PAGE_SIZE, ATT_KV_HEADS, V_TILE)

    def page(b, j, pt, p):
        return pt[b * N_PAGES + jnp.minimum(j, ATT_S_STEPS - 1) * ATT_S_PPS + p]

    k_specs = [pl.BlockSpec(kblk, functools.partial(lambda b, j, pt, p: (page(b, j, pt, p), 0, 0), p=p))
               for p in range(ATT_S_PPS)]
    v_specs = [pl.BlockSpec(vblk, functools.partial(lambda b, j, pt, p, e: (0, page(b, j, pt, p), 0, 0, e), p=p, e=e))
               for p in range(ATT_S_PPS) for e in range(V_LANE_TILES)]
    spread = np.zeros((PAGE_SIZE, PAGE_SIZE * ATT_KV_HEADS), np.float32)
    for t in range(PAGE_SIZE):
        spread[t, t * ATT_KV_HEADS:(t + 1) * ATT_KV_HEADS] = 1.0
    kn_spec = pl.BlockSpec(kblk, lambda b, j, pt: (b, 0, 0))
    vn_specs = [pl.BlockSpec(vblk, functools.partial(lambda b, j, pt, e: (0, b, 0, 0, e), e=e)) for e in range(V_LANE_TILES)]
    grid_spec = pltpu.PrefetchScalarGridSpec(
        num_scalar_prefetch=1,
        grid=(DEC_BATCH, ATT_S_STEPS + 1),
        in_specs=[pl.BlockSpec((1, ATT_S_ROWS, ATT_HEAD_DIM), lambda b, j, pt: (b, 0, 0)),
                  *k_specs, *v_specs, kn_spec, *vn_specs, _full((PAGE_SIZE, PAGE_SIZE * ATT_KV_HEADS)),
                  vec, vec, vec, vec, _full((1, ATT_V_DIM))],
        out_specs=pl.BlockSpec((1, ATT_KV_HEADS, DEC_SEQ * ATT_REP, ATT_V_DIM), lambda b, j, pt: (b, 0, 0, 0)),
        scratch_shapes=[pltpu.VMEM((ATT_S_ROWS, LANES), F32), pltpu.VMEM((ATT_S_ROWS, LANES), F32),
                        pltpu.VMEM((ATT_S_ROWS, ATT_V_DIM), F32)],
    )
    return pl.pallas_call(
        _attn_sample_kernel,
        grid_spec=grid_spec,
        out_shape=jax.ShapeDtypeStruct((DEC_BATCH, ATT_KV_HEADS, DEC_SEQ * ATT_REP, ATT_V_DIM), F32),
        compiler_params=_cparams(("parallel", "arbitrary")),
        name="attn_sample",
    )(page_table, q, *([kc] * ATT_S_PPS), *([vc] * (ATT_S_PPS * V_LANE_TILES)), knew, *([vnew] * V_LANE_TILES),
      jnp.asarray(spread, BF16), *lam_params, subw)


def _mix_kernel(ya_ref, yb_ref, ga_ref, gb_ref, wa_ref, wb_ref, o_ref):
    oa = jnp.dot(ya_ref[...], wa_ref[...], preferred_element_type=F32)
    ob = jnp.dot(yb_ref[...], wb_ref[...], preferred_element_type=F32)
    o_ref[...] = (_sigmoid(ga_ref[...]) * oa + _sigmoid(gb_ref[...]) * ob).astype(o_ref.dtype)


def mix_branches(yzn, on, proj, wa, wb, tm=256):
    d = D_MODEL
    row = lambda i: (i, 0)
    return pl.pallas_call(
        _mix_kernel,
        grid=(R // tm,),
        in_specs=[pl.BlockSpec((tm, d), row), pl.BlockSpec((tm, d), row),
                  pl.BlockSpec((tm, d), lambda i: (i, PCOL_GA // d)), pl.BlockSpec((tm, d), lambda i: (i, PCOL_GB // d)),
                  _full((d, d)), _full((d, d))],
        out_specs=pl.BlockSpec((tm, d), row),
        out_shape=jax.ShapeDtypeStruct((R, d), BF16),
        compiler_params=_cparams(("parallel",)),
        name="mix_branches",
    )(yzn, on, proj, proj, wa, wb)


def _post_kernel(mix_ref, h_ref, wo_ref, nw_ref, rw_ref, rb_ref, h2_ref, n_ref, ti_ref, tg_ref):
    h2 = h_ref[...] + jnp.dot(mix_ref[...], wo_ref[...], preferred_element_type=F32)
    h2_ref[...] = h2
    ms = jnp.mean(h2 * h2, axis=-1, keepdims=True)
    n = h2 * lax.rsqrt(ms + NORM_EPS) * nw_ref[...]
    n_ref[...] = n
    logits = jnp.dot(n, rw_ref[...], precision=HI, preferred_element_type=F32) + rb_ref[...]
    lane = lax.broadcasted_iota(I32, logits.shape, 1)
    work = logits
    vals, idxs = [], []
    for _ in range(TOP_K):
        mx = jnp.max(work, axis=1, keepdims=True)
        am = jnp.min(jnp.where(work == mx, lane, LANES), axis=1, keepdims=True)
        vals.append(mx)
        idxs.append(am)
        work = jnp.where(lane == am, -jnp.inf, work)
    es = [jnp.exp(v - vals[0]) for v in vals]
    den = es[0] + es[1] + es[2] + es[3]
    ti = jnp.zeros(logits.shape, I32)
    tg = jnp.zeros(logits.shape, F32)
    for k in range(TOP_K):
        ti = jnp.where(lane == k, idxs[k], ti)
        tg = jnp.where(lane == k, es[k] / den, tg)
    ti_ref[...] = ti
    tg_ref[...] = tg


def post_mixer(mix, h, wo, nw, rw, rb, tm=256):
    d = D_MODEL
    row = lambda i: (i, 0)
    return pl.pallas_call(
        _post_kernel,
        grid=(R // tm,),
        in_specs=[pl.BlockSpec((tm, d), row), pl.BlockSpec((tm, d), row), _full((d, d)), _full((1, d)),
                  _full((d, LANES)), _full((1, LANES))],
        out_specs=[pl.BlockSpec((tm, d), row), pl.BlockSpec((tm, d), row),
                   pl.BlockSpec((tm, LANES), row), pl.BlockSpec((tm, LANES), row)],
        out_shape=[jax.ShapeDtypeStruct((R, d), F32), jax.ShapeDtypeStruct((R, d), F32),
                   jax.ShapeDtypeStruct((R, LANES), I32), jax.ShapeDtypeStruct((R, LANES), F32)],
        compiler_params=_cparams(("parallel",)),
        name="post_mixer",
    )(mix, h, wo, nw, rw, rb)


def _rank_kernel(ti_ref, rank_ref, cnt_ref, carry_s):
    i = pl.program_id(0)
    tm = ti_ref.shape[0]

    @pl.when(i == 0)
    def _():
        carry_s[...] = jnp.zeros_like(carry_s)

    ti = ti_ref[...]
    lane = lax.broadcasted_iota(I32, ti.shape, 1)
    row = i * tm + lax.broadcasted_iota(I32, (tm, 1), 0)
    valid = row >= LEAD
    ohs = [jnp.where((lane == ti[:, k:k + 1]) & valid, 1.0, 0.0) for k in range(TOP_K)]
    osum = ohs[0] + ohs[1] + ohs[2] + ohs[3]
    li = lax.broadcasted_iota(I32, (tm, tm), 0)
    si = lax.broadcasted_iota(I32, (tm, tm), 1)
    before = jnp.dot((li > si).astype(BF16), osum.astype(BF16), preferred_element_type=F32) + carry_s[...]
    rank = jnp.zeros(ti.shape, I32)
    for k in range(TOP_K):
        rk = jnp.sum(ohs[k] * before, axis=1, keepdims=True)
        rank = jnp.where(lane == k, rk.astype(I32), rank)
    rank_ref[...] = rank
    carry_s[...] = carry_s[...] + jnp.sum(osum, axis=0, keepdims=True)
    cnt_ref[...] = carry_s[...].astype(I32)


def expert_ranks(ti, tm=256):
    return pl.pallas_call(
        _rank_kernel,
        grid=(R // tm,),
        in_specs=[pl.BlockSpec((tm, LANES), lambda i: (i, 0))],
        out_specs=[pl.BlockSpec((tm, LANES), lambda i: (i, 0)), _full((1, LANES))],
        out_shape=[jax.ShapeDtypeStruct((R, LANES), I32), jax.ShapeDtypeStruct((1, LANES), I32)],
        scratch_shapes=[pltpu.VMEM((1, LANES), F32)],
        compiler_params=_cparams(("arbitrary",)),
        name="expert_ranks",
    )(ti)


DISPATCH_TM = 256


def _dispatch_kernel(dest_ref, tail_ref, nu_ref, n_ref, xs_ref, zbuf, sem, zsem):
    i = pl.program_id(0)
    tm = DISPATCH_TM

    @pl.when(i == 0)
    def _():
        zbuf[...] = jnp.zeros_like(zbuf)

        def zero_block(row0):
            return pltpu.make_async_copy(zbuf, xs_ref.at[pl.ds(pl.multiple_of(row0, MOE_TM), MOE_TM), :], zsem)

        def each_unused_block(fn):
            def body(b, carry):
                fn(zero_block(b * MOE_TM))
                return carry

            lax.fori_loop(nu_ref[0], MOE_ROWS // MOE_TM, body, 0)

        for e in range(N_EXPERTS):
            @pl.when(tail_ref[e] >= 0)
            def _():
                zero_block(tail_ref[e]).start()

        each_unused_block(lambda c: c.start())
        for e in range(N_EXPERTS):
            @pl.when(tail_ref[e] >= 0)
            def _():
                zero_block(tail_ref[e]).wait()

        each_unused_block(lambda c: c.wait())

    def body(r, carry):
        for k in range(TOP_K):
            d = dest_ref[(i * tm + r) * TOP_K + k]
            pltpu.make_async_copy(n_ref.at[pl.ds(r, 1), :], xs_ref.at[pl.ds(d, 1), :], sem).start()
        return carry

    lax.fori_loop(0, tm, body, 0)
    for k in range(TOP_K):
        pltpu.make_async_copy(n_ref, xs_ref.at[pl.ds(0, tm), :], sem).wait()


def moe_dispatch(dest_flat, tail_start, n_used, n):
    tm = DISPATCH_TM
    d = n.shape[1]
    grid_spec = pltpu.PrefetchScalarGridSpec(
        num_scalar_prefetch=3,
        grid=(R // tm,),
        in_specs=[pl.BlockSpec((tm, d), lambda i, ds, ts, nu: (i, 0))],
        out_specs=pl.BlockSpec(memory_space=pl.ANY),
        scratch_shapes=[pltpu.VMEM((MOE_TM, d), n.dtype), pltpu.SemaphoreType.DMA, pltpu.SemaphoreType.DMA],
    )
    return pl.pallas_call(
        _dispatch_kernel,
        grid_spec=grid_spec,
        out_shape=jax.ShapeDtypeStruct((MOE_ROWS, d), n.dtype),
        compiler_params=_cparams(("arbitrary",)),
        name="moe_dispatch",
    )(dest_flat, tail_start, n_used, n)


def _new_expert(be_ref, i):
    return (i == 0) | (be_ref[i] != be_ref[jnp.maximum(i - 1, 0)])


def _stream_expert_weights(i, be_ref, run_ref, nxt_ref, copies, wbuf, wbf):
    slot = run_ref[i] % 2

    @pl.when(i == 0)
    def _():
        for c in copies(be_ref[0], 0):
            c.start()

    @pl.when(_new_expert(be_ref, i))
    def _():
        for c in copies(be_ref[i], slot):
            c.wait()

        @pl.when(nxt_ref[i] >= 0)
        def _():
            for c in copies(nxt_ref[i], 1 - slot):
                c.start()

        for t in range(wbf.shape[0]):
            wbf[t] = wbuf[slot, t].astype(BF16)


def _gmm1_kernel(be_ref, nu_ref, run_ref, nxt_ref, x_ref, w_hbm, bg_ref, bu_ref, h_ref, wbuf, wbf, sem):
    j = pl.program_id(0)
    i = pl.program_id(1)
    tn = MOE_TN_GATE_UP

    def copies(e, slot):
        cols = (pl.multiple_of(j * tn, tn), pl.multiple_of(D_FF + j * tn, tn))
        return [pltpu.make_async_copy(w_hbm.at[e, :, pl.ds(c0, tn)], wbuf.at[slot, t], sem.at[slot, t])
                for t, c0 in enumerate(cols)]

    @pl.when(i < nu_ref[0])
    def _():
        _stream_expert_weights(i, be_ref, run_ref, nxt_ref, copies, wbuf, wbf)
        x = x_ref[...].astype(BF16)
        hg = jnp.dot(x, wbf[0], preferred_element_type=F32) + bg_ref[0]
        hu = jnp.dot(x, wbf[1], preferred_element_type=F32) + bu_ref[0]
        g = jnp.minimum(hg, SWIGLU_LIMIT)
        u = jnp.clip(hu, -SWIGLU_LIMIT, SWIGLU_LIMIT)
        h_ref[...] = (g * _sigmoid(SWIGLU_ALPHA * g) * (u + 1.0)).astype(h_ref.dtype)

    @pl.when(i >= nu_ref[0])
    def _():
        h_ref[...] = jnp.zeros_like(h_ref)


def moe_gate_up(sched, xs, w_gu, b_gu):
    tm, tn = MOE_TM, MOE_TN_GATE_UP
    nj = D_FF // tn
    blk = lambda i, nu: jnp.minimum(i, nu[0] - 1)
    grid_spec = pltpu.PrefetchScalarGridSpec(
        num_scalar_prefetch=4,
        grid=(nj, MOE_NB),
        in_specs=[pl.BlockSpec((tm, D_MODEL), lambda j, i, be, nu, rn, nx: (blk(i, nu), 0)),
                  pl.BlockSpec(memory_space=pl.ANY),
                  pl.BlockSpec((1, 1, tn), lambda j, i, be, nu, rn, nx: (be[blk(i, nu)], 0, j)),
                  pl.BlockSpec((1, 1, tn), lambda j, i, be, nu, rn, nx: (be[blk(i, nu)], 0, nj + j))],
        out_specs=pl.BlockSpec((tm, tn), lambda j, i, be, nu, rn, nx: (i, j)),
        scratch_shapes=[pltpu.VMEM((2, 2, D_MODEL, tn), F32), pltpu.VMEM((2, D_MODEL, tn), BF16),
                        pltpu.SemaphoreType.DMA((2, 2))],
    )
    return pl.pallas_call(
        _gmm1_kernel,
        grid_spec=grid_spec,
        out_shape=jax.ShapeDtypeStruct((MOE_NB * tm, D_FF), BF16),
        compiler_params=_cparams(("arbitrary", "arbitrary")),
        name="moe_gate_up",
    )(*sched, xs, w_gu, b_gu, b_gu)


def _gmm2_kernel(be_ref, nu_ref, run_ref, nxt_ref, h_ref, w_hbm, bd_ref, y_ref, wbuf, wbf, sem):
    i = pl.program_id(0)

    def copies(e, slot):
        return [pltpu.make_async_copy(w_hbm.at[e], wbuf.at[slot, 0], sem.at[slot])]

    @pl.when(i < nu_ref[0])
    def _():
        _stream_expert_weights(i, be_ref, run_ref, nxt_ref, copies, wbuf, wbf)
        y_ref[...] = jnp.dot(h_ref[...], wbf[0], preferred_element_type=F32) + bd_ref[0]

    @pl.when(i >= nu_ref[0])
    def _():
        y_ref[...] = jnp.zeros_like(y_ref)


def moe_down(sched, hidden, w_dn, b_dn):
    tm = MOE_TM
    blk = lambda i, nu: jnp.minimum(i, nu[0] - 1)
    grid_spec = pltpu.PrefetchScalarGridSpec(
        num_scalar_prefetch=4,
        grid=(MOE_NB,),
        in_specs=[pl.BlockSpec((tm, D_FF), lambda i, be, nu, rn, nx: (blk(i, nu), 0)),
                  pl.BlockSpec(memory_space=pl.ANY),
                  pl.BlockSpec((1, 1, D_MODEL), lambda i, be, nu, rn, nx: (be[blk(i, nu)], 0, 0))],
        out_specs=pl.BlockSpec((tm, D_MODEL), lambda i, be, nu, rn, nx: (i, 0)),
        scratch_shapes=[pltpu.VMEM((2, 1, D_FF, D_MODEL), F32), pltpu.VMEM((1, D_FF, D_MODEL), BF16),
                        pltpu.SemaphoreType.DMA((2,))],
    )
    return pl.pallas_call(
        _gmm2_kernel,
        grid_spec=grid_spec,
        out_shape=jax.ShapeDtypeStruct((MOE_NB * tm, D_MODEL), F32),
        compiler_params=_cparams(("arbitrary",)),
        name="moe_down",
    )(*sched, hidden, w_dn, b_dn)


COMBINE_TM = 128
COMBINE_TILES = R // COMBINE_TM
PROMPT_TILE0 = (LEAD + N_META) // COMBINE_TM
SAMPLE_TILE = P_ROWS // COMBINE_TM


def _combine_kernel(src_ref, h2_ref, tg_ref, fw_ref, yb_ref, yp_ref, ys_ref, buf, sem):
    i = pl.program_id(0)
    tm = COMBINE_TM

    def gather(tile, slot):
        def body(r, carry):
            for k in range(TOP_K):
                s = src_ref[(tile * tm + r) * TOP_K + k]
                pltpu.make_async_copy(yb_ref.at[pl.ds(s, 1), :], buf.at[slot, k, pl.ds(r, 1), :], sem.at[slot]).start()
            return carry

        lax.fori_loop(0, tm, body, 0)

    @pl.when(i == 0)
    def _():
        gather(0, 0)

    @pl.when(i + 1 < COMBINE_TILES)
    def _():
        gather(i + 1, (i + 1) % 2)

    slot = i % 2
    for k in range(TOP_K):
        pltpu.make_async_copy(yb_ref.at[pl.ds(0, tm), :], buf.at[slot, k], sem.at[slot]).wait()
    tg = tg_ref[...]
    f = tg[:, 0:1] * buf[slot, 0]
    for k in range(1, TOP_K):
        f = f + tg[:, k:k + 1] * buf[slot, k]
    out = h2_ref[...] + f
    ms = jnp.mean(out * out, axis=-1, keepdims=True)
    y = out * lax.rsqrt(ms + NORM_EPS) * fw_ref[...]

    @pl.when((i >= PROMPT_TILE0) & (i < SAMPLE_TILE))
    def _():
        yp_ref[...] = y

    @pl.when(i == SAMPLE_TILE)
    def _():
        ys_ref[...] = y


def moe_combine(src_flat, h2, tg, fw, yb):
    tm = COMBINE_TM
    d = D_MODEL
    grid_spec = pltpu.PrefetchScalarGridSpec(
        num_scalar_prefetch=1,
        grid=(COMBINE_TILES,),
        in_specs=[pl.BlockSpec((tm, d), lambda i, s: (i, 0)), pl.BlockSpec((tm, LANES), lambda i, s: (i, 0)),
                  pl.BlockSpec((1, d), lambda i, s: (0, 0)), pl.BlockSpec(memory_space=pl.ANY)],
        out_specs=[pl.BlockSpec((tm, d), lambda i, s: (jnp.clip(i - PROMPT_TILE0, 0, SEQ // tm - 1), 0)),
                   pl.BlockSpec((tm, d), lambda i, s: (0, 0))],
        scratch_shapes=[pltpu.VMEM((2, TOP_K, tm, d), F32), pltpu.SemaphoreType.DMA((2,))],
    )
    return pl.pallas_call(
        _combine_kernel,
        grid_spec=grid_spec,
        out_shape=[jax.ShapeDtypeStruct((SEQ, d), F32), jax.ShapeDtypeStruct((S_ROWS, d), F32)],
        compiler_params=_cparams(("arbitrary",)),
        name="moe_combine",
    )(src_flat, h2, tg, fw, yb)


def _rope_tables():
    half = ROT_DIM // 2
    pos_p = jnp.maximum(jnp.arange(P_ROWS) - LEAD, 0)
    pos_s = SEQ + jnp.arange(S_ROWS) % DEC_SEQ
    pos = jnp.concatenate([pos_p, pos_s])
    inv_freq = 1.0 / (ROPE_THETA ** (jnp.arange(half, dtype=F32) * (2.0 / ROT_DIM)))
    ang = pos.astype(F32)[:, None] * inv_freq[None, :]
    cos, sin = jnp.cos(ang), jnp.sin(ang)
    zeros = jnp.zeros((R, LANES - ROT_DIM), F32)
    cos_t = jnp.concatenate([cos, cos, jnp.ones((R, LANES - ROT_DIM), F32)], axis=1)
    sin_a = jnp.concatenate([jnp.zeros_like(sin), sin, zeros], axis=1)
    sin_b = jnp.concatenate([-sin, jnp.zeros_like(sin), zeros], axis=1)
    return cos_t, sin_a, sin_b


def _pad_lanes(v, fill=0.0):
    v = v.reshape(1, -1)
    return jnp.concatenate([v, jnp.full((1, LANES - v.shape[1]), fill, v.dtype)], axis=1)


def kernel(x_prompt, x_sample, cache_k, cache_v, state_ssm, state_conv, page_table, meta_tokens,
           norm_mix_w, w_in, conv_w, conv_b, dt_bias, a_log, d_skip, ssd_norm_w, w_ssd_out,
           lambda_q1, lambda_k1, lambda_q2, lambda_k2, subln_w, w_attn_out, w_o, norm_ffn_w,
           router_w, router_b, w_gate_up, b_gate_up, w_down, b_down, final_norm_w):
    l = 0
    d = D_MODEL
    x_all = jnp.concatenate([jnp.zeros((LEAD, d), F32), meta_tokens, x_prompt[0], x_sample.reshape(S_ROWS, d)], axis=0)
    xn = rmsnorm_rows(x_all, norm_mix_w[l])
    w = w_in[l]
    o_z, o_xbc, o_dt, o_q, o_k, o_v, o_ga, o_gb = (int(o) for o in np.cumsum(
        (0, SSD_INNER, CONV_DIM, SSD_HEADS, 2048, 1024, 1024, d)))
    w_perm = jnp.concatenate([w[:, o_z:o_xbc], w[:, o_ga:o_gb + d], w[:, o_xbc:o_dt], w[:, o_q:o_ga]], axis=1).astype(BF16)
    w_dt = jnp.concatenate([w[:, o_dt:o_q], jnp.zeros((d, LANES - SSD_HEADS), F32)], axis=1).astype(BF16)
    cos_t, sin_a, sin_b = _rope_tables()
    proj, kb, vt = in_projection(xn, w_perm, cos_t, sin_a, sin_b)
    dtraw = dt_projection(xn, w_dt)

    e_mat = np.zeros((LANES, SSD_INNER), np.float32)
    for h in range(SSD_HEADS):
        e_mat[h, h * SSD_HEAD_DIM:(h + 1) * SSD_HEAD_DIM] = 1.0
    ssd_params = (conv_w[l], conv_b[l].reshape(1, CONV_DIM), _pad_lanes(dt_bias[l]), _pad_lanes(a_log[l]),
                  jnp.repeat(d_skip[l], SSD_HEAD_DIM).reshape(1, SSD_INNER), ssd_norm_w[l].reshape(1, SSD_INNER),
                  jnp.asarray(e_mat, BF16))
    yzn_p, ssm_p = ssd_prompt(proj, dtraw, ssd_params)
    conv_prev = jnp.concatenate([jnp.zeros((DEC_BATCH, SUBLANES - (SSD_CONV - 1), CONV_DIM), F32), state_conv[l]], axis=1)
    yzn_s, ssm_s = ssd_sample(proj, dtraw, conv_prev, state_ssm[l].reshape(DEC_BATCH, SSD_INNER, SSD_STATE), ssd_params)
    yzn = jnp.concatenate([yzn_p, yzn_s], axis=0)

    k_rows = proj[:, PCOL_K:PCOL_V]
    v_rows = proj[:, PCOL_V:P_COLS]
    lam_params = (lambda_q1[l].reshape(1, -1), lambda_k1[l].reshape(1, -1), lambda_q2[l].reshape(1, -1),
                  lambda_k2[l].reshape(1, -1))
    subw = subln_w[l].reshape(1, ATT_V_DIM)
    vt_ext = jnp.concatenate([vt.reshape(ATT_KV_HEADS, ATT_V_DIM, R),
                              jnp.ones((ATT_KV_HEADS, ATT_VT_ROWS - ATT_V_DIM, R), BF16)], axis=1)
    on_p = attn_prompt(proj, kb, vt_ext.reshape(ATT_KV_HEADS * ATT_VT_ROWS, R), lam_params,
                       subln_w[l].reshape(ATT_V_DIM, 1))

    q_s = proj[P_ROWS:, PCOL_Q:PCOL_K].reshape(DEC_BATCH, DEC_SEQ, ATT_KV_HEADS, ATT_REP, 2, ATT_HEAD_DIM)
    q_s = jnp.transpose(q_s, (0, 2, 4, 1, 3, 5)).reshape(DEC_BATCH, ATT_S_ROWS, ATT_HEAD_DIM).astype(BF16)
    n_phys = cache_k.shape[1]
    kc = cache_k[l].reshape(n_phys, KROWS, ATT_HEAD_DIM)
    pad_new = lambda u: jnp.concatenate([u.reshape(DEC_BATCH, DEC_SEQ, -1),
                                         jnp.zeros((DEC_BATCH, PAGE_SIZE - DEC_SEQ, u.shape[-1]), F32)], axis=1)
    k_new = pad_new(k_rows[P_ROWS:]).reshape(DEC_BATCH, KROWS, ATT_HEAD_DIM)
    v_new = pad_new(v_rows[P_ROWS:]).reshape(1, DEC_BATCH, PAGE_SIZE, ATT_KV_HEADS, ATT_V_DIM)
    o_s = attn_sample(page_table.reshape(-1), q_s, kc, cache_v[l:l + 1], k_new, v_new, lam_params, subw)
    o_s = o_s.reshape(DEC_BATCH, ATT_KV_HEADS, DEC_SEQ, ATT_REP, ATT_V_DIM)
    o_s = jnp.transpose(o_s, (0, 2, 1, 3, 4)).reshape(S_ROWS, ATT_HEADS * ATT_V_DIM).astype(BF16)
    on = lax.dynamic_update_slice(on_p, o_s, (P_ROWS, 0))

    mix = mix_branches(yzn, on, proj, w_ssd_out[l].astype(BF16), w_attn_out[l].astype(BF16))
    rw = jnp.concatenate([router_w[l], jnp.zeros((d, LANES - N_EXPERTS), F32)], axis=1)
    rb = _pad_lanes(router_b[l], NEG)
    h2, n_rows, top_i, top_g = post_mixer(mix, x_all, w_o[l].astype(BF16), norm_ffn_w[l].reshape(1, d), rw, rb)

    rank, counts = expert_ranks(top_i)
    counts = counts[0, :N_EXPERTS]
    padded = (counts + MOE_TM - 1) // MOE_TM * MOE_TM
    pad_end = jnp.cumsum(padded)
    pad_start = pad_end - padded
    dest = pad_start[top_i[:, :TOP_K]] + rank[:, :TOP_K]
    is_tok = (jnp.arange(R) >= LEAD)[:, None]
    dump = MOE_NB * MOE_TM + jnp.arange(R * TOP_K).reshape(R, TOP_K)
    dest_scatter = jnp.where(is_tok, dest, dump).astype(I32).reshape(-1)
    dest_gather = jnp.where(is_tok, dest, 0).astype(I32).reshape(-1)
    block_start = jnp.arange(MOE_NB) * MOE_TM
    block_e = jnp.minimum(jnp.sum(pad_end[None, :] <= block_start[:, None], axis=1), N_EXPERTS - 1).astype(I32)
    n_used = (pad_end[-1:] // MOE_TM).astype(I32)
    ex = jnp.arange(N_EXPERTS)
    nonempty = padded > 0
    run_of_expert = jnp.cumsum(nonempty) - 1
    later = (ex[None, :] > ex[:, None]) & nonempty[None, :]
    next_expert = jnp.min(jnp.where(later, ex[None, :], N_EXPERTS), axis=1)
    next_expert = jnp.where(next_expert < N_EXPERTS, next_expert, -1)
    sched = (block_e, n_used, run_of_expert[block_e].astype(I32), next_expert[block_e].astype(I32))
    tail_start = jnp.where(nonempty, pad_end - MOE_TM, -1).astype(I32)
    xs = moe_dispatch(dest_scatter, tail_start, n_used, n_rows)
    hidden = moe_gate_up(sched, xs, w_gate_up[l], b_gate_up[l].reshape(N_EXPERTS, 1, 2 * D_FF))
    yb = moe_down(sched, hidden, w_down[l], b_down[l].reshape(N_EXPERTS, 1, d))
    y_p, y_s = moe_combine(dest_gather, h2, top_g, final_norm_w.reshape(1, d), yb)

    tok0 = LEAD
    y_prompt = y_p[None]
    y_sample = y_s.reshape(DEC_BATCH, DEC_SEQ, d)
    k_prompt = k_rows[tok0:P_ROWS].reshape(1, 1, SEQ + N_META, ATT_KV_HEADS, 2, ATT_HEAD_DIM)
    v_prompt = v_rows[tok0:P_ROWS].reshape(1, 1, SEQ + N_META, ATT_KV_HEADS, ATT_V_DIM)
    k_sample = k_rows[P_ROWS:].reshape(1, DEC_BATCH, DEC_SEQ, ATT_KV_HEADS, 2, ATT_HEAD_DIM)
    v_sample = v_rows[P_ROWS:].reshape(1, DEC_BATCH, DEC_SEQ, ATT_KV_HEADS, ATT_V_DIM)
    ssm_prompt = ssm_p.reshape(1, 1, SSD_HEADS, SSD_HEAD_DIM, SSD_STATE)
    ssm_sample = ssm_s.reshape(1, DEC_BATCH, SSD_HEADS, SSD_HEAD_DIM, SSD_STATE)
    xbc = proj[:, PCOL_XBC:PCOL_XBC + CONV_DIM]
    conv_prompt = xbc[P_ROWS - (SSD_CONV - 1):P_ROWS][None, None]
    xbc_s = xbc[P_ROWS:].reshape(DEC_BATCH, DEC_SEQ, CONV_DIM)
    conv_sample = jnp.concatenate([state_conv[l], xbc_s], axis=1)[:, -(SSD_CONV - 1):][None]
    return (y_prompt, y_sample, k_prompt, v_prompt, k_sample, v_sample, ssm_prompt, ssm_sample, conv_prompt, conv_sample)
```

```python
import functools
import math

import jax
import jax.numpy as jnp
import numpy as np
from jax import lax
from jax.experimental import pallas as pl
from jax.experimental.pallas import tpu as pltpu

F32 = jnp.float32
BF16 = jnp.bfloat16
I32 = jnp.int32
HI = lax.Precision.HIGHEST

D_MODEL = 2048
SEQ = 8192
DEC_BATCH = 32
DEC_SEQ = 4
PAGE_SIZE = 128
N_META = 16
NORM_EPS = 1e-5
SSD_INNER = D_MODEL
SSD_HEAD_DIM = 64
SSD_HEADS = SSD_INNER // SSD_HEAD_DIM
SSD_GROUPS = 4
SSD_REP = SSD_HEADS // SSD_GROUPS
SSD_STATE = 128
SSD_CONV = 4
SSD_CHUNK = 128
CONV_DIM = SSD_INNER + 2 * SSD_GROUPS * SSD_STATE
ATT_HEADS = 8
ATT_KV_HEADS = 4
ATT_REP = ATT_HEADS // ATT_KV_HEADS
ATT_HEAD_DIM = D_MODEL // ATT_HEADS // 2
ATT_V_DIM = 2 * ATT_HEAD_DIM
ATT_SCALE = ATT_HEAD_DIM ** -0.5
ROT_DIM = ATT_HEAD_DIM // 4
ROPE_THETA = 500000.0
N_EXPERTS = 32
TOP_K = 4
D_FF = D_MODEL
SWIGLU_LIMIT = 7.0
SWIGLU_ALPHA = 1.702
LAM_INIT = 0.8 - 0.6 * math.exp(-0.3 * 0)

LEAD = (-N_META) % SSD_CHUNK
P_ROWS = LEAD + N_META + SEQ
S_ROWS = DEC_BATCH * DEC_SEQ
R = P_ROWS + S_ROWS
N_CHUNKS = P_ROWS // SSD_CHUNK

LANES = 128
SUBLANES = 8
VMEM_LIMIT = 56 * 1024 * 1024

PCOL_Z, PCOL_GA, PCOL_GB, PCOL_XBC, PCOL_Q, PCOL_K, PCOL_V = 0, 2048, 4096, 6144, 9216, 11264, 12288
P_COLS = 13312
PROJ_TN = 1024
ROPE_BLOCKS = (PCOL_Q // PROJ_TN, PCOL_V // PROJ_TN)

MOE_TM = 256
MOE_NB = -(-(R * TOP_K) // MOE_TM) + N_EXPERTS
MOE_DUMP_BLOCKS = -(-(LEAD * TOP_K) // MOE_TM)
MOE_ROWS = (MOE_NB + MOE_DUMP_BLOCKS) * MOE_TM
MOE_TN_GATE_UP = 1024
NEG = -1e30


def _cparams(sem, vmem=VMEM_LIMIT):
    return pltpu.CompilerParams(dimension_semantics=sem, vmem_limit_bytes=vmem)


_NT = (((1,), (1,)), ((), ()))


def _sigmoid(x):
    return 1.0 / (1.0 + jnp.exp(-x))


def _rmsnorm_kernel(x_ref, w_ref, o_ref):
    x = x_ref[...]
    ms = jnp.mean(x * x, axis=-1, keepdims=True)
    o_ref[...] = (x * lax.rsqrt(ms + NORM_EPS) * w_ref[...]).astype(o_ref.dtype)


def rmsnorm_rows(x, w, tm=768):
    rows, d = x.shape
    return pl.pallas_call(
        _rmsnorm_kernel,
        grid=(rows // tm,),
        in_specs=[pl.BlockSpec((tm, d), lambda i: (i, 0)), pl.BlockSpec((1, d), lambda i: (0, 0))],
        out_specs=pl.BlockSpec((tm, d), lambda i: (i, 0)),
        out_shape=jax.ShapeDtypeStruct((rows, d), BF16),
        compiler_params=_cparams(("parallel",)),
        name="rmsnorm_rows",
    )(x, w.reshape(1, d))


def _rope(blk, c, sa, sb):
    return blk * c + pltpu.roll(blk, ROT_DIM // 2, axis=1) * sa + pltpu.roll(blk, LANES - ROT_DIM // 2, axis=1) * sb


def _proj_kernel(x_ref, w_ref, c_ref, sa_ref, sb_ref, o_ref, kb_ref, vt_ref):
    j = pl.program_id(1)
    acc = lax.dot_general(x_ref[...], w_ref[...], _NT, preferred_element_type=F32)
    is_rope = (j >= ROPE_BLOCKS[0]) & (j < ROPE_BLOCKS[1])

    @pl.when(is_rope)
    def _():
        c, sa, sb = c_ref[...], sa_ref[...], sb_ref[...]
        for g in range(acc.shape[1] // LANES):
            sl = slice(g * LANES, (g + 1) * LANES)
            o_ref[:, sl] = _rope(acc[:, sl], c, sa, sb)

    @pl.when(jnp.logical_not(is_rope))
    def _():
        o_ref[...] = acc

    @pl.when(j == PCOL_K // PROJ_TN)
    def _():
        kb_ref[...] = o_ref[...].astype(kb_ref.dtype)

    @pl.when(j == PCOL_V // PROJ_TN)
    def _():
        vt_ref[...] = acc.T.astype(vt_ref.dtype)


def in_projection(xn, wt_bf, cos_t, sin_a, sin_b, tm=1408):
    rows, d = xn.shape
    tn = PROJ_TN
    tab = pl.BlockSpec((tm, LANES), lambda i, j: (i, 0))
    return pl.pallas_call(
        _proj_kernel,
        grid=(rows // tm, P_COLS // tn),
        in_specs=[pl.BlockSpec((tm, d), lambda i, j: (i, 0)), pl.BlockSpec((tn, d), lambda i, j: (j, 0)), tab, tab, tab],
        out_specs=[pl.BlockSpec((tm, tn), lambda i, j: (i, j)),
                   pl.BlockSpec((tm, tn), lambda i, j: (i, 0)),
                   pl.BlockSpec((tn, tm), lambda i, j: (0, i))],
        out_shape=[jax.ShapeDtypeStruct((rows, P_COLS), F32),
                   jax.ShapeDtypeStruct((rows, tn), BF16),
                   jax.ShapeDtypeStruct((tn, rows), BF16)],
        compiler_params=_cparams(("parallel", "arbitrary")),
        name="in_projection",
    )(xn, wt_bf, cos_t, sin_a, sin_b)


def _mm_kernel(x_ref, w_ref, o_ref):
    o_ref[...] = lax.dot_general(x_ref[...], w_ref[...], _NT, preferred_element_type=F32).astype(o_ref.dtype)


def dt_projection(xn, wt_dt, tm=1056):
    rows, d = xn.shape
    return pl.pallas_call(
        _mm_kernel,
        grid=(rows // tm,),
        in_specs=[pl.BlockSpec((tm, d), lambda i: (i, 0)), pl.BlockSpec((LANES, d), lambda i: (0, 0))],
        out_specs=pl.BlockSpec((tm, LANES), lambda i: (i, 0)),
        out_shape=jax.ShapeDtypeStruct((rows, LANES), F32),
        compiler_params=_cparams(("parallel",)),
        name="dt_projection",
    )(xn, wt_dt)


def _ssd_chunk(xprev, xcur, dtraw, z, valid, s_ref, cw_ref, cb_ref, dtb_ref, alog_ref, dsk_ref, nw_ref, e_ref,
               state_dot_precision):
    L = xcur.shape[0]
    ext = jnp.concatenate([xprev, xcur], axis=0)
    off = SUBLANES - (SSD_CONV - 1)
    acc = cb_ref[...]
    for k in range(SSD_CONV):
        acc = acc + ext[off + k:off + k + L] * cw_ref[k:k + 1, :]
    act = acc * _sigmoid(acc)
    xs = act[:, :SSD_INNER]
    nb = SSD_GROUPS * SSD_STATE
    bm = act[:, SSD_INNER:SSD_INNER + nb].astype(BF16)
    cm = act[:, SSD_INNER + nb:].astype(BF16)

    dpre = dtraw + dtb_ref[...]
    dt = jnp.maximum(dpre, 0.0) + jnp.log1p(jnp.exp(-jnp.abs(dpre)))
    dt = jnp.where(valid, dt, 0.0)
    da = dt * (-jnp.exp(alog_ref[...]))

    li = lax.broadcasted_iota(I32, (L, L), 0)
    si = lax.broadcasted_iota(I32, (L, L), 1)
    tril = li >= si
    eye = li == si
    a_cs = jnp.dot(tril.astype(F32), da, precision=HI, preferred_element_type=F32)
    a_tot = a_cs[L - 1:L, :]
    e_mat = e_ref[...]

    def expand(v):
        out = None
        for _ in range(3):
            piece = v.astype(BF16)
            term = jnp.dot(piece, e_mat, preferred_element_type=F32)
            out = term if out is None else out + term
            v = v - piece.astype(F32)
        return out

    dt_x = expand(dt)
    eacs_x = expand(jnp.exp(a_cs))
    dte_x = expand(jnp.exp(a_tot - a_cs))
    etot = jnp.exp(a_tot)

    xdt = xs * dt_x
    xdt_b = xdt.astype(BF16)
    xde = xdt * dte_x
    gw = SSD_REP * SSD_HEAD_DIM
    lane = lax.broadcasted_iota(I32, (L, LANES), 1)
    nt = (((1,), (1,)), ((), ()))
    tn = (((0,), (0,)), ((), ()))
    y_groups = []
    for g in range(SSD_GROUPS):
        bg = bm[:, g * SSD_STATE:(g + 1) * SSD_STATE]
        cg = cm[:, g * SSD_STATE:(g + 1) * SSD_STATE]
        cb = lax.dot_general(cg, bg, nt, preferred_element_type=F32)
        s_g = s_ref[g * gw:(g + 1) * gw, :]
        y_off = lax.dot_general(cg, s_g.astype(BF16), nt, preferred_element_type=F32)
        if state_dot_precision is None:
            upd = lax.dot_general(xde[:, g * gw:(g + 1) * gw].astype(BF16), bg, tn, preferred_element_type=F32)
        else:
            upd = lax.dot_general(xde[:, g * gw:(g + 1) * gw], act[:, SSD_INNER + g * SSD_STATE:SSD_INNER + (g + 1) * SSD_STATE],
                                  tn, precision=state_dot_precision, preferred_element_type=F32)
        pairs = []
        decs = []
        for q in range(SSD_REP // 2):
            yd = []
            for r in range(2):
                h = g * SSD_REP + 2 * q + r
                colb = jnp.broadcast_to(a_cs[:, h:h + 1], (L, L))
                rowb = jnp.sum(jnp.where(eye, colb, 0.0), axis=0, keepdims=True)
                lm = jnp.where(tril, jnp.exp(jnp.minimum(colb - rowb, 0.0)), 0.0)
                m = (cb * lm).astype(BF16)
                c0 = (g * SSD_REP + 2 * q) * SSD_HEAD_DIM
                yd.append(jnp.dot(m, xdt_b[:, c0:c0 + LANES], preferred_element_type=F32))
                decs.append(jnp.broadcast_to(etot[:, h:h + 1], (SSD_HEAD_DIM, SSD_STATE)))
            pairs.append(jnp.where(lane < SSD_HEAD_DIM, yd[0], yd[1]))
        y_g = jnp.concatenate(pairs, axis=1) + y_off * eacs_x[:, g * gw:(g + 1) * gw]
        y_groups.append(y_g)
        s_ref[g * gw:(g + 1) * gw, :] = jnp.concatenate(decs, axis=0) * s_g + upd
    y = jnp.concatenate(y_groups, axis=1) + dsk_ref[...] * xs
    yz = y * (z * _sigmoid(z))
    outs = []
    for g in range(SSD_GROUPS):
        yg = yz[:, g * gw:(g + 1) * gw]
        ms = jnp.mean(yg * yg, axis=-1, keepdims=True)
        outs.append(yg * lax.rsqrt(ms + NORM_EPS) * nw_ref[:, g * gw:(g + 1) * gw])
    return jnp.concatenate(outs, axis=1)


def _ssd_prompt_kernel(xprev_ref, xcur_ref, dt_ref, z_ref, cw_ref, cb_ref, dtb_ref, alog_ref, dsk_ref, nw_ref, e_ref,
                       y_ref, s_ref):
    c = pl.program_id(0)

    @pl.when(c == 0)
    def _():
        s_ref[...] = jnp.zeros_like(s_ref)

    xprev = jnp.where(c == 0, 0.0, xprev_ref[...])
    row = c * SSD_CHUNK + lax.broadcasted_iota(I32, (SSD_CHUNK, 1), 0)
    y = _ssd_chunk(xprev, xcur_ref[...], dt_ref[...], z_ref[...], row >= LEAD, s_ref,
                   cw_ref, cb_ref, dtb_ref, alog_ref, dsk_ref, nw_ref, e_ref, None)
    y_ref[...] = y.astype(y_ref.dtype)


def _full(shape):
    return pl.BlockSpec(shape, lambda *_: (0,) * len(shape))


def ssd_prompt(proj, dtraw, params):
    L = SSD_CHUNK
    xb = PCOL_XBC // CONV_DIM
    in_specs = [
        pl.BlockSpec((SUBLANES, CONV_DIM), lambda c: (jnp.maximum(c * (L // SUBLANES) - 1, 0), xb)),
        pl.BlockSpec((L, CONV_DIM), lambda c: (c, xb)),
        pl.BlockSpec((L, LANES), lambda c: (c, 0)),
        pl.BlockSpec((L, SSD_INNER), lambda c: (c, PCOL_Z // SSD_INNER)),
    ] + [_full(p.shape) for p in params]
    return pl.pallas_call(
        _ssd_prompt_kernel,
        grid=(N_CHUNKS,),
        in_specs=in_specs,
        out_specs=[pl.BlockSpec((L, SSD_INNER), lambda c: (c, 0)), _full((SSD_INNER, SSD_STATE))],
        out_shape=[jax.ShapeDtypeStruct((P_ROWS, SSD_INNER), BF16), jax.ShapeDtypeStruct((SSD_INNER, SSD_STATE), F32)],
        compiler_params=_cparams(("arbitrary",)),
        name="ssd_prompt",
    )(proj, proj, dtraw, proj, *params)


SSD_S_NB = 4


def _ssd_sample_kernel(xprev_ref, xcur_ref, dt_ref, z_ref, sin_ref, cw_ref, cb_ref, dtb_ref, alog_ref, dsk_ref, nw_ref,
                       e_ref, y_ref, sout_ref):
    L = SUBLANES
    valid = lax.broadcasted_iota(I32, (L, 1), 0) < DEC_SEQ
    ys = []
    for b in range(SSD_S_NB):
        rows = slice(b * DEC_SEQ, (b + 1) * DEC_SEQ)
        pad = lambda u: jnp.concatenate([u, jnp.zeros((L - DEC_SEQ, u.shape[1]), u.dtype)], axis=0)
        sout_ref[b] = sin_ref[b]
        y = _ssd_chunk(xprev_ref[b], pad(xcur_ref[rows, :]), pad(dt_ref[rows, :]), pad(z_ref[rows, :]), valid,
                       sout_ref.at[b], cw_ref, cb_ref, dtb_ref, alog_ref, dsk_ref, nw_ref, e_ref, HI)
        ys.append(y[:DEC_SEQ])
    y_ref[...] = jnp.concatenate(ys, axis=0).astype(y_ref.dtype)


def ssd_sample(proj, dtraw, conv_prev, state, params):
    nb = SSD_S_NB
    rb = nb * DEC_SEQ
    r0 = P_ROWS // rb
    in_specs = [
        pl.BlockSpec((nb, SUBLANES, CONV_DIM), lambda i: (i, 0, 0)),
        pl.BlockSpec((rb, CONV_DIM), lambda i: (r0 + i, PCOL_XBC // CONV_DIM)),
        pl.BlockSpec((rb, LANES), lambda i: (r0 + i, 0)),
        pl.BlockSpec((rb, SSD_INNER), lambda i: (r0 + i, PCOL_Z // SSD_INNER)),
        pl.BlockSpec((nb, SSD_INNER, SSD_STATE), lambda i: (i, 0, 0)),
    ] + [_full(p.shape) for p in params]
    return pl.pallas_call(
        _ssd_sample_kernel,
        grid=(DEC_BATCH // nb,),
        in_specs=in_specs,
        out_specs=[pl.BlockSpec((rb, SSD_INNER), lambda i: (i, 0)),
                   pl.BlockSpec((nb, SSD_INNER, SSD_STATE), lambda i: (i, 0, 0))],
        out_shape=[jax.ShapeDtypeStruct((S_ROWS, SSD_INNER), BF16),
                   jax.ShapeDtypeStruct((DEC_BATCH, SSD_INNER, SSD_STATE), F32)],
        compiler_params=_cparams(("parallel",)),
        name="ssd_sample",
    )(conv_prev, proj, dtraw, proj, state, *params)


def _lambda(lq1_ref, lk1_ref, lq2_ref, lk2_ref):
    s1 = jnp.sum(lq1_ref[...] * lk1_ref[...], axis=-1, keepdims=True)
    s2 = jnp.sum(lq2_ref[...] * lk2_ref[...], axis=-1, keepdims=True)
    return jnp.exp(s1) - jnp.exp(s2) + LAM_INIT


def _subnorm(o, subw_ref):
    ms = jnp.mean(o * o, axis=-1, keepdims=True)
    return (o * lax.rsqrt(ms + NORM_EPS) * subw_ref[...]) * (1.0 - LAM_INIT)


ATT_TQ = 256
ATT_TK = 768
BF16_SUBLANES = 16
ATT_VT_ROWS = ATT_V_DIM + BF16_SUBLANES
EXP2_SCALE = ATT_SCALE * math.log2(math.e)


def _attn_prompt_kernel(q_ref, k_ref, vt_ref, d_ref, lq1_ref, lk1_ref, lq2_ref, lk2_ref, subw_ref, o_ref, m_s, acc_s):
    i = pl.program_id(1)
    tq, tk, hd = ATT_TQ, ATT_TK, ATT_HEAD_DIM
    q = q_ref[...]
    qc = [jnp.concatenate([q[:, r * 2 * hd + c * hd: r * 2 * hd + (c + 1) * hd] for r in range(ATT_REP)], axis=0).astype(BF16)
          for c in range(2)]
    m_s[...] = jnp.full_like(m_s, NEG)
    acc_s[...] = jnp.zeros_like(acc_s)
    nt = (((1,), (1,)), ((), ()))
    n_blocks = (i * tq + tq - 1) // tk + 1

    def step(j, mask):
        k0 = pl.multiple_of(j * tk, tk)
        kblk = k_ref[pl.ds(k0, tk), :]
        vt = vt_ref[:, pl.ds(k0, tk)]
        sts = [lax.dot_general(kblk[:, c * hd:(c + 1) * hd], qc[c], nt, preferred_element_type=F32) for c in range(2)]
        ps, alphas = [], []
        for c in range(2):
            st = sts[c]
            if "causal" in mask:
                st = jnp.where(d_ref[...] <= i * tq - j * tk, st, NEG)
            if "lead" in mask:
                st = st + jnp.where(lax.broadcasted_iota(I32, (tk, 1), 0) < LEAD, NEG, 0.0)
            m_old = m_s[c]
            m_new = jnp.maximum(m_old, jnp.max(st, axis=0, keepdims=True))
            alphas.append(jnp.exp2((m_old - m_new) * EXP2_SCALE))
            ps.append(jnp.exp2((st - m_new) * EXP2_SCALE).astype(BF16))
            m_s[c] = m_new
        for c in range(2):
            acc_s[c] = alphas[c] * acc_s[c] + jnp.dot(vt, ps[c], preferred_element_type=F32)

    @pl.when(n_blocks == 1)
    def _():
        step(0, ("causal", "lead"))

    @pl.when(n_blocks > 1)
    def _():
        step(0, ("lead",))

        def body(j, carry):
            step(j, ())
            return carry

        lax.fori_loop(1, n_blocks - 1, body, 0)
        step(n_blocks - 1, ("causal",))

    lam = _lambda(lq1_ref, lk1_ref, lq2_ref, lk2_ref)
    for r in range(ATT_REP):
        cols = slice(r * tq, (r + 1) * tq)
        o0 = acc_s[0, :ATT_V_DIM, cols] / acc_s[0, ATT_V_DIM:ATT_V_DIM + 1, cols]
        o1 = acc_s[1, :ATT_V_DIM, cols] / acc_s[1, ATT_V_DIM:ATT_V_DIM + 1, cols]
        ot = o0 - lam * o1
        ms = jnp.mean(ot * ot, axis=0, keepdims=True)
        ot = (ot * lax.rsqrt(ms + NORM_EPS) * subw_ref[...]) * (1.0 - LAM_INIT)
        o_ref[:, r * ATT_V_DIM:(r + 1) * ATT_V_DIM] = ot.T.astype(o_ref.dtype)


def attn_prompt(proj, kb, vt_ext, lam_params, subw_col):
    tq = ATT_TQ
    gq = ATT_REP * 2 * ATT_HEAD_DIM
    vec = _full((1, ATT_HEAD_DIM))
    key_minus_query = (np.arange(ATT_TK)[:, None] - np.arange(ATT_REP * tq)[None, :] % tq).astype(np.int32)
    return pl.pallas_call(
        _attn_prompt_kernel,
        grid=(ATT_KV_HEADS, R // tq),
        in_specs=[pl.BlockSpec((tq, gq), lambda g, i: (i, PCOL_Q // gq + g)),
                  pl.BlockSpec((R, ATT_V_DIM), lambda g, i: (0, g)),
                  pl.BlockSpec((ATT_VT_ROWS, R), lambda g, i: (g, 0)),
                  _full((ATT_TK, ATT_REP * tq)),
                  vec, vec, vec, vec, _full((ATT_V_DIM, 1))],
        out_specs=pl.BlockSpec((tq, gq), lambda g, i: (i, g)),
        out_shape=jax.ShapeDtypeStruct((R, ATT_HEADS * ATT_V_DIM), BF16),
        scratch_shapes=[pltpu.VMEM((2, 1, ATT_REP * tq), F32),
                        pltpu.VMEM((2, ATT_VT_ROWS, ATT_REP * tq), F32)],
        compiler_params=_cparams(("parallel", "arbitrary")),
        name="attn_prompt",
    )(proj, kb, vt_ext, jnp.asarray(key_minus_query), *lam_params, subw_col)


N_PAGES = SEQ // PAGE_SIZE
ATT_S_ROWS = ATT_KV_HEADS * 2 * DEC_SEQ * ATT_REP
ATT_S_GROUP = 2 * DEC_SEQ * ATT_REP
ATT_S_PPS = 8
ATT_S_STEPS = N_PAGES // ATT_S_PPS
KROWS = PAGE_SIZE * ATT_KV_HEADS * 2
V_TILE = ATT_V_DIM
V_LANE_TILES = ATT_V_DIM // V_TILE


def _attn_sample_kernel(pt_ref, q_ref, *refs):
    npg = ATT_S_PPS
    kc_refs = refs[:npg]
    vc_refs = [refs[npg + V_LANE_TILES * p: npg + V_LANE_TILES * (p + 1)] for p in range(npg)]
    base = npg * (1 + V_LANE_TILES)
    kn_ref = refs[base]
    vn_refs = refs[base + 1: base + 1 + V_LANE_TILES]
    x_ref, lq1_ref, lk1_ref, lq2_ref, lk2_ref, subw_ref, o_ref, m_s, l_s, acc_s = refs[base + 1 + V_LANE_TILES:]
    j = pl.program_id(1)
    nt = (((1,), (1,)), ((), ()))
    nhc = ATT_KV_HEADS * 2
    rows_hc = DEC_SEQ * ATT_REP

    @pl.when(j == 0)
    def _():
        m_s[...] = jnp.full_like(m_s, NEG)
        l_s[...] = jnp.zeros_like(l_s)
        acc_s[...] = jnp.zeros_like(acc_s)

    def step(k_pages, v_pages, masked):
        q = q_ref[0]
        cols = []
        for k_src in k_pages:
            ss = []
            for hc in range(nhc):
                khc = k_src[pl.ds(hc, PAGE_SIZE, stride=nhc), :].astype(BF16)
                ss.append(lax.dot_general(q[hc * rows_hc:(hc + 1) * rows_hc, :], khc, nt, preferred_element_type=F32))
            cols.append(jnp.concatenate(ss, axis=0))
        s = jnp.concatenate(cols, axis=1) if len(cols) > 1 else cols[0]
        if masked:
            t_q = (lax.broadcasted_iota(I32, s.shape, 0) % rows_hc) // ATT_REP
            t_k = lax.broadcasted_iota(I32, s.shape, 1)
            s = jnp.where(t_k <= t_q, s, NEG)
        m_old = m_s[...]
        m_new = jnp.maximum(m_old, jnp.max(s, axis=1, keepdims=True))
        alpha = jnp.exp2((m_old - m_new) * EXP2_SCALE)
        p = jnp.exp2((s - m_new[:, :1]) * EXP2_SCALE)
        l_s[...] = alpha * l_s[...] + jnp.sum(p, axis=1, keepdims=True)
        pb = p.astype(BF16)
        head_of_col = lax.broadcasted_iota(I32, (ATT_S_ROWS, PAGE_SIZE * ATT_KV_HEADS), 1) % ATT_KV_HEADS
        head_of_row = lax.broadcasted_iota(I32, (ATT_S_ROWS, PAGE_SIZE * ATT_KV_HEADS), 0) // ATT_S_GROUP
        own_head = head_of_col == head_of_row
        pv = None
        for pi, v_tiles in enumerate(v_pages):
            v2 = jnp.concatenate([v[...].reshape(PAGE_SIZE * ATT_KV_HEADS, V_TILE) for v in v_tiles], axis=1).astype(BF16)
            spread = jnp.dot(pb[:, pi * PAGE_SIZE:(pi + 1) * PAGE_SIZE], x_ref[...], preferred_element_type=F32)
            pe = jnp.where(own_head, spread, 0.0).astype(BF16)
            d = jnp.dot(pe, v2, preferred_element_type=F32)
            pv = d if pv is None else pv + d
        acc_s[...] = alpha[:, :1] * acc_s[...] + pv
        m_s[...] = m_new

    @pl.when(j < ATT_S_STEPS)
    def _():
        step(kc_refs, vc_refs, False)

    @pl.when(j == ATT_S_STEPS)
    def _():
        step([kn_ref], [vn_refs], True)
        lam = _lambda(lq1_ref, lk1_ref, lq2_ref, lk2_ref)
        for g in range(ATT_KV_HEADS):
            r0 = g * ATT_S_GROUP
            o0 = acc_s[r0:r0 + rows_hc, :] / l_s[r0:r0 + rows_hc, :1]
            o1 = acc_s[r0 + rows_hc:r0 + 2 * rows_hc, :] / l_s[r0 + rows_hc:r0 + 2 * rows_hc, :1]
            o_ref[0, g] = _subnorm(o0 - lam * o1, subw_ref)


def attn_sample(page_table, q, kc, vc, knew, vnew, lam_params, subw):
    vec = _full((1, ATT_HEAD_DIM))
    kblk = (None, KROWS, ATT_HEAD_DIM)
    vblk = (None, None, PAGE_SIZE, ATT_KV_HEADS, V_TILE)

    def page(b, j, pt, p):
        return pt[b * N_PAGES + jnp.minimum(j, ATT_S_STEPS - 1) * ATT_S_PPS + p]

    k_specs = [pl.BlockSpec(kblk, functools.partial(lambda b, j, pt, p: (page(b, j, pt, p), 0, 0), p=p))
               for p in range(ATT_S_PPS)]
    v_specs = [pl.BlockSpec(vblk, functools.partial(lambda b, j, pt, p, e: (0, page(b, j, pt, p), 0, 0, e), p=p, e=e))
               for p in range(ATT_S_PPS) for e in range(V_LANE_TILES)]
    spread = np.zeros((PAGE_SIZE, PAGE_SIZE * ATT_KV_HEADS), np.float32)
    for t in range(PAGE_SIZE):
        spread[t, t * ATT_KV_HEADS:(t + 1) * ATT_KV_HEADS] = 1.0
    kn_spec = pl.BlockSpec(kblk, lambda b, j, pt: (b, 0, 0))
    vn_specs = [pl.BlockSpec(vblk, functools.partial(lambda b, j, pt, e: (0, b, 0, 0, e), e=e)) for e in range(V_LANE_TILES)]
    grid_spec = pltpu.PrefetchScalarGridSpec(
        num_scalar_prefetch=1,
        grid=(DEC_BATCH, ATT_S_STEPS + 1),
        in_specs=[pl.BlockSpec((1, ATT_S_ROWS, ATT_HEAD_DIM), lambda b, j, pt: (b, 0, 0)),
                  *k_specs, *v_specs, kn_spec, *vn_specs, _full((PAGE_SIZE, PAGE_SIZE * ATT_KV_HEADS)),
                  vec, vec, vec, vec, _full((1, ATT_V_DIM))],
        out_specs=pl.BlockSpec((1, ATT_KV_HEADS, DEC_SEQ * ATT_REP, ATT_V_DIM), lambda b, j, pt: (b, 0, 0, 0)),
        scratch_shapes=[pltpu.VMEM((ATT_S_ROWS, LANES), F32), pltpu.VMEM((ATT_S_ROWS, LANES), F32),
                        pltpu.VMEM((ATT_S_ROWS, ATT_V_DIM), F32)],
    )
    return pl.pallas_call(
        _attn_sample_kernel,
        grid_spec=grid_spec,
        out_shape=jax.ShapeDtypeStruct((DEC_BATCH, ATT_KV_HEADS, DEC_SEQ * ATT_REP, ATT_V_DIM), F32),
        compiler_params=_cparams(("parallel", "arbitrary")),
        name="attn_sample",
    )(page_table, q, *([kc] * ATT_S_PPS), *([vc] * (ATT_S_PPS * V_LANE_TILES)), knew, *([vnew] * V_LANE_TILES),
      jnp.asarray(spread, BF16), *lam_params, subw)


def _mix_kernel(ya_ref, yb_ref, ga_ref, gb_ref, wa_ref, wb_ref, o_ref):
    oa = jnp.dot(ya_ref[...], wa_ref[...], preferred_element_type=F32)
    ob = jnp.dot(yb_ref[...], wb_ref[...], preferred_element_type=F32)
    o_ref[...] = (_sigmoid(ga_ref[...]) * oa + _sigmoid(gb_ref[...]) * ob).astype(o_ref.dtype)


def mix_branches(yzn, on, proj, wa, wb, tm=256):
    d = D_MODEL
    row = lambda i: (i, 0)
    return pl.pallas_call(
        _mix_kernel,
        grid=(R // tm,),
        in_specs=[pl.BlockSpec((tm, d), row), pl.BlockSpec((tm, d), row),
                  pl.BlockSpec((tm, d), lambda i: (i, PCOL_GA // d)), pl.BlockSpec((tm, d), lambda i: (i, PCOL_GB // d)),
                  _full((d, d)), _full((d, d))],
        out_specs=pl.BlockSpec((tm, d), row),
        out_shape=jax.ShapeDtypeStruct((R, d), BF16),
        compiler_params=_cparams(("parallel",)),
        name="mix_branches",
    )(yzn, on, proj, proj, wa, wb)


def _post_kernel(mix_ref, h_ref, wo_ref, nw_ref, rw_ref, rb_ref, h2_ref, n_ref, ti_ref, tg_ref):
    h2 = h_ref[...] + jnp.dot(mix_ref[...], wo_ref[...], preferred_element_type=F32)
    h2_ref[...] = h2
    ms = jnp.mean(h2 * h2, axis=-1, keepdims=True)
    n = h2 * lax.rsqrt(ms + NORM_EPS) * nw_ref[...]
    n_ref[...] = n
    logits = jnp.dot(n, rw_ref[...], precision=HI, preferred_element_type=F32) + rb_ref[...]
    lane = lax.broadcasted_iota(I32, logits.shape, 1)
    work = logits
    vals, idxs = [], []
    for _ in range(TOP_K):
        mx = jnp.max(work, axis=1, keepdims=True)
        am = jnp.min(jnp.where(work == mx, lane, LANES), axis=1, keepdims=True)
        vals.append(mx)
        idxs.append(am)
        work = jnp.where(lane == am, -jnp.inf, work)
    es = [jnp.exp(v - vals[0]) for v in vals]
    den = es[0] + es[1] + es[2] + es[3]
    ti = jnp.zeros(logits.shape, I32)
    tg = jnp.zeros(logits.shape, F32)
    for k in range(TOP_K):
        ti = jnp.where(lane == k, idxs[k], ti)
        tg = jnp.where(lane == k, es[k] / den, tg)
    ti_ref[...] = ti
    tg_ref[...] = tg


def post_mixer(mix, h, wo, nw, rw, rb, tm=256):
    d = D_MODEL
    row = lambda i: (i, 0)
    return pl.pallas_call(
        _post_kernel,
        grid=(R // tm,),
        in_specs=[pl.BlockSpec((tm, d), row), pl.BlockSpec((tm, d), row), _full((d, d)), _full((1, d)),
                  _full((d, LANES)), _full((1, LANES))],
        out_specs=[pl.BlockSpec((tm, d), row), pl.BlockSpec((tm, d), row),
                   pl.BlockSpec((tm, LANES), row), pl.BlockSpec((tm, LANES), row)],
        out_shape=[jax.ShapeDtypeStruct((R, d), F32), jax.ShapeDtypeStruct((R, d), F32),
                   jax.ShapeDtypeStruct((R, LANES), I32), jax.ShapeDtypeStruct((R, LANES), F32)],
        compiler_params=_cparams(("parallel",)),
        name="post_mixer",
    )(mix, h, wo, nw, rw, rb)


def _rank_kernel(ti_ref, rank_ref, cnt_ref, carry_s):
    i = pl.program_id(0)
    tm = ti_ref.shape[0]

    @pl.when(i == 0)
    def _():
        carry_s[...] = jnp.zeros_like(carry_s)

    ti = ti_ref[...]
    lane = lax.broadcasted_iota(I32, ti.shape, 1)
    row = i * tm + lax.broadcasted_iota(I32, (tm, 1), 0)
    valid = row >= LEAD
    ohs = [jnp.where((lane == ti[:, k:k + 1]) & valid, 1.0, 0.0) for k in range(TOP_K)]
    osum = ohs[0] + ohs[1] + ohs[2] + ohs[3]
    li = lax.broadcasted_iota(I32, (tm, tm), 0)
    si = lax.broadcasted_iota(I32, (tm, tm), 1)
    before = jnp.dot((li > si).astype(BF16), osum.astype(BF16), preferred_element_type=F32) + carry_s[...]
    rank = jnp.zeros(ti.shape, I32)
    for k in range(TOP_K):
        rk = jnp.sum(ohs[k] * before, axis=1, keepdims=True)
        rank = jnp.where(lane == k, rk.astype(I32), rank)
    rank_ref[...] = rank
    carry_s[...] = carry_s[...] + jnp.sum(osum, axis=0, keepdims=True)
    cnt_ref[...] = carry_s[...].astype(I32)


def expert_ranks(ti, tm=256):
    return pl.pallas_call(
        _rank_kernel,
        grid=(R // tm,),
        in_specs=[pl.BlockSpec((tm, LANES), lambda i: (i, 0))],
        out_specs=[pl.BlockSpec((tm, LANES), lambda i: (i, 0)), _full((1, LANES))],
        out_shape=[jax.ShapeDtypeStruct((R, LANES), I32), jax.ShapeDtypeStruct((1, LANES), I32)],
        scratch_shapes=[pltpu.VMEM((1, LANES), F32)],
        compiler_params=_cparams(("arbitrary",)),
        name="expert_ranks",
    )(ti)


DISPATCH_TM = 256


def _dispatch_kernel(dest_ref, tail_ref, nu_ref, n_ref, xs_ref, zbuf, sem, zsem):
    i = pl.program_id(0)
    tm = DISPATCH_TM

    @pl.when(i == 0)
    def _():
        zbuf[...] = jnp.zeros_like(zbuf)

        def zero_block(row0):
            return pltpu.make_async_copy(zbuf, xs_ref.at[pl.ds(pl.multiple_of(row0, MOE_TM), MOE_TM), :], zsem)

        def each_unused_block(fn):
            def body(b, carry):
                fn(zero_block(b * MOE_TM))
                return carry

            lax.fori_loop(nu_ref[0], MOE_ROWS // MOE_TM, body, 0)

        for e in range(N_EXPERTS):
            @pl.when(tail_ref[e] >= 0)
            def _():
                zero_block(tail_ref[e]).start()

        each_unused_block(lambda c: c.start())
        for e in range(N_EXPERTS):
            @pl.when(tail_ref[e] >= 0)
            def _():
                zero_block(tail_ref[e]).wait()

        each_unused_block(lambda c: c.wait())

    def body(r, carry):
        for k in range(TOP_K):
            d = dest_ref[(i * tm + r) * TOP_K + k]
            pltpu.make_async_copy(n_ref.at[pl.ds(r, 1), :], xs_ref.at[pl.ds(d, 1), :], sem).start()
        return carry

    lax.fori_loop(0, tm, body, 0)
    for k in range(TOP_K):
        pltpu.make_async_copy(n_ref, xs_ref.at[pl.ds(0, tm), :], sem).wait()


def moe_dispatch(dest_flat, tail_start, n_used, n):
    tm = DISPATCH_TM
    d = n.shape[1]
    grid_spec = pltpu.PrefetchScalarGridSpec(
        num_scalar_prefetch=3,
        grid=(R // tm,),
        in_specs=[pl.BlockSpec((tm, d), lambda i, ds, ts, nu: (i, 0))],
        out_specs=pl.BlockSpec(memory_space=pl.ANY),
        scratch_shapes=[pltpu.VMEM((MOE_TM, d), n.dtype), pltpu.SemaphoreType.DMA, pltpu.SemaphoreType.DMA],
    )
    return pl.pallas_call(
        _dispatch_kernel,
        grid_spec=grid_spec,
        out_shape=jax.ShapeDtypeStruct((MOE_ROWS, d), n.dtype),
        compiler_params=_cparams(("arbitrary",)),
        name="moe_dispatch",
    )(dest_flat, tail_start, n_used, n)


def _new_expert(be_ref, i):
    return (i == 0) | (be_ref[i] != be_ref[jnp.maximum(i - 1, 0)])


def _stream_expert_weights(i, be_ref, run_ref, nxt_ref, copies, wbuf, wbf):
    slot = run_ref[i] % 2

    @pl.when(i == 0)
    def _():
        for c in copies(be_ref[0], 0):
            c.start()

    @pl.when(_new_expert(be_ref, i))
    def _():
        for c in copies(be_ref[i], slot):
            c.wait()

        @pl.when(nxt_ref[i] >= 0)
        def _():
            for c in copies(nxt_ref[i], 1 - slot):
                c.start()

        for t in range(wbf.shape[0]):
            wbf[t] = wbuf[slot, t].astype(BF16)


def _gmm1_kernel(be_ref, nu_ref, run_ref, nxt_ref, x_ref, w_hbm, bg_ref, bu_ref, h_ref, wbuf, wbf, sem):
    j = pl.program_id(0)
    i = pl.program_id(1)
    tn = MOE_TN_GATE_UP

    def copies(e, slot):
        cols = (pl.multiple_of(j * tn, tn), pl.multiple_of(D_FF + j * tn, tn))
        return [pltpu.make_async_copy(w_hbm.at[e, :, pl.ds(c0, tn)], wbuf.at[slot, t], sem.at[slot, t])
                for t, c0 in enumerate(cols)]

    @pl.when(i < nu_ref[0])
    def _():
        _stream_expert_weights(i, be_ref, run_ref, nxt_ref, copies, wbuf, wbf)
        x = x_ref[...].astype(BF16)
        hg = jnp.dot(x, wbf[0], preferred_element_type=F32) + bg_ref[0]
        hu = jnp.dot(x, wbf[1], preferred_element_type=F32) + bu_ref[0]
        g = jnp.minimum(hg, SWIGLU_LIMIT)
        u = jnp.clip(hu, -SWIGLU_LIMIT, SWIGLU_LIMIT)
        h_ref[...] = (g * _sigmoid(SWIGLU_ALPHA * g) * (u + 1.0)).astype(h_ref.dtype)

    @pl.when(i >= nu_ref[0])
    def _():
        h_ref[...] = jnp.zeros_like(h_ref)


def moe_gate_up(sched, xs, w_gu, b_gu):
    tm, tn = MOE_TM, MOE_TN_GATE_UP
    nj = D_FF // tn
    blk = lambda i, nu: jnp.minimum(i, nu[0] - 1)
    grid_spec = pltpu.PrefetchScalarGridSpec(
        num_scalar_prefetch=4,
        grid=(nj, MOE_NB),
        in_specs=[pl.BlockSpec((tm, D_MODEL), lambda j, i, be, nu, rn, nx: (blk(i, nu), 0)),
                  pl.BlockSpec(memory_space=pl.ANY),
                  pl.BlockSpec((1, 1, tn), lambda j, i, be, nu, rn, nx: (be[blk(i, nu)], 0, j)),
                  pl.BlockSpec((1, 1, tn), lambda j, i, be, nu, rn, nx: (be[blk(i, nu)], 0, nj + j))],
        out_specs=pl.BlockSpec((tm, tn), lambda j, i, be, nu, rn, nx: (i, j)),
        scratch_shapes=[pltpu.VMEM((2, 2, D_MODEL, tn), F32), pltpu.VMEM((2, D_MODEL, tn), BF16),
                        pltpu.SemaphoreType.DMA((2, 2))],
    )
    return pl.pallas_call(
        _gmm1_kernel,
        grid_spec=grid_spec,
        out_shape=jax.ShapeDtypeStruct((MOE_NB * tm, D_FF), BF16),
        compiler_params=_cparams(("arbitrary", "arbitrary")),
        name="moe_gate_up",
    )(*sched, xs, w_gu, b_gu, b_gu)


def _gmm2_kernel(be_ref, nu_ref, run_ref, nxt_ref, h_ref, w_hbm, bd_ref, y_ref, wbuf, wbf, sem):
    i = pl.program_id(0)

    def copies(e, slot):
        return [pltpu.make_async_copy(w_hbm.at[e], wbuf.at[slot, 0], sem.at[slot])]

    @pl.when(i < nu_ref[0])
    def _():
        _stream_expert_weights(i, be_ref, run_ref, nxt_ref, copies, wbuf, wbf)
        y_ref[...] = jnp.dot(h_ref[...], wbf[0], preferred_element_type=F32) + bd_ref[0]

    @pl.when(i >= nu_ref[0])
    def _():
        y_ref[...] = jnp.zeros_like(y_ref)


def moe_down(sched, hidden, w_dn, b_dn):
    tm = MOE_TM
    blk = lambda i, nu: jnp.minimum(i, nu[0] - 1)
    grid_spec = pltpu.PrefetchScalarGridSpec(
        num_scalar_prefetch=4,
        grid=(MOE_NB,),
        in_specs=[pl.BlockSpec((tm, D_FF), lambda i, be, nu, rn, nx: (blk(i, nu), 0)),
                  pl.BlockSpec(memory_space=pl.ANY),
                  pl.BlockSpec((1, 1, D_MODEL), lambda i, be, nu, rn, nx: (be[blk(i, nu)], 0, 0))],
        out_specs=pl.BlockSpec((tm, D_MODEL), lambda i, be, nu, rn, nx: (i, 0)),
        scratch_shapes=[pltpu.VMEM((2, 1, D_FF, D_MODEL), F32), pltpu.VMEM((1, D_FF, D_MODEL), BF16),
                        pltpu.SemaphoreType.DMA((2,))],
    )
    return pl.pallas_call(
        _gmm2_kernel,
        grid_spec=grid_spec,
        out_shape=jax.ShapeDtypeStruct((MOE_NB * tm, D_MODEL), F32),
        compiler_params=_cparams(("arbitrary",)),
        name="moe_down",
    )(*sched, hidden, w_dn, b_dn)


COMBINE_TM = 128
COMBINE_TILES = R // COMBINE_TM
PROMPT_TILE0 = (LEAD + N_META) // COMBINE_TM
SAMPLE_TILE = P_ROWS // COMBINE_TM


def _combine_kernel(src_ref, h2_ref, tg_ref, fw_ref, yb_ref, yp_ref, ys_ref, buf, sem):
    i = pl.program_id(0)
    tm = COMBINE_TM

    def gather(tile, slot):
        def body(r, carry):
            for k in range(TOP_K):
                s = src_ref[(tile * tm + r) * TOP_K + k]
                pltpu.make_async_copy(yb_ref.at[pl.ds(s, 1), :], buf.at[slot, k, pl.ds(r, 1), :], sem.at[slot]).start()
            return carry

        lax.fori_loop(0, tm, body, 0)

    @pl.when(i == 0)
    def _():
        gather(0, 0)

    @pl.when(i + 1 < COMBINE_TILES)
    def _():
        gather(i + 1, (i + 1) % 2)

    slot = i % 2
    for k in range(TOP_K):
        pltpu.make_async_copy(yb_ref.at[pl.ds(0, tm), :], buf.at[slot, k], sem.at[slot]).wait()
    tg = tg_ref[...]
    f = tg[:, 0:1] * buf[slot, 0]
    for k in range(1, TOP_K):
        f = f + tg[:, k:k + 1] * buf[slot, k]
    out = h2_ref[...] + f
    ms = jnp.mean(out * out, axis=-1, keepdims=True)
    y = out * lax.rsqrt(ms + NORM_EPS) * fw_ref[...]

    @pl.when((i >= PROMPT_TILE0) & (i < SAMPLE_TILE))
    def _():
        yp_ref[...] = y

    @pl.when(i == SAMPLE_TILE)
    def _():
        ys_ref[...] = y


def moe_combine(src_flat, h2, tg, fw, yb):
    tm = COMBINE_TM
    d = D_MODEL
    grid_spec = pltpu.PrefetchScalarGridSpec(
        num_scalar_prefetch=1,
        grid=(COMBINE_TILES,),
        in_specs=[pl.BlockSpec((tm, d), lambda i, s: (i, 0)), pl.BlockSpec((tm, LANES), lambda i, s: (i, 0)),
                  pl.BlockSpec((1, d), lambda i, s: (0, 0)), pl.BlockSpec(memory_space=pl.ANY)],
        out_specs=[pl.BlockSpec((tm, d), lambda i, s: (jnp.clip(i - PROMPT_TILE0, 0, SEQ // tm - 1), 0)),
                   pl.BlockSpec((tm, d), lambda i, s: (0, 0))],
        scratch_shapes=[pltpu.VMEM((2, TOP_K, tm, d), F32), pltpu.SemaphoreType.DMA((2,))],
    )
    return pl.pallas_call(
        _combine_kernel,
        grid_spec=grid_spec,
        out_shape=[jax.ShapeDtypeStruct((SEQ, d), F32), jax.ShapeDtypeStruct((S_ROWS, d), F32)],
        compiler_params=_cparams(("arbitrary",)),
        name="moe_combine",
    )(src_flat, h2, tg, fw, yb)


def _rope_tables():
    half = ROT_DIM // 2
    pos_p = jnp.maximum(jnp.arange(P_ROWS) - LEAD, 0)
    pos_s = SEQ + jnp.arange(S_ROWS) % DEC_SEQ
    pos = jnp.concatenate([pos_p, pos_s])
    inv_freq = 1.0 / (ROPE_THETA ** (jnp.arange(half, dtype=F32) * (2.0 / ROT_DIM)))
    ang = pos.astype(F32)[:, None] * inv_freq[None, :]
    cos, sin = jnp.cos(ang), jnp.sin(ang)
    zeros = jnp.zeros((R, LANES - ROT_DIM), F32)
    cos_t = jnp.concatenate([cos, cos, jnp.ones((R, LANES - ROT_DIM), F32)], axis=1)
    sin_a = jnp.concatenate([jnp.zeros_like(sin), sin, zeros], axis=1)
    sin_b = jnp.concatenate([-sin, jnp.zeros_like(sin), zeros], axis=1)
    return cos_t, sin_a, sin_b


def _pad_lanes(v, fill=0.0):
    v = v.reshape(1, -1)
    return jnp.concatenate([v, jnp.full((1, LANES - v.shape[1]), fill, v.dtype)], axis=1)


def kernel(x_prompt, x_sample, cache_k, cache_v, state_ssm, state_conv, page_table, meta_tokens,
           norm_mix_w, w_in, conv_w, conv_b, dt_bias, a_log, d_skip, ssd_norm_w, w_ssd_out,
           lambda_q1, lambda_k1, lambda_q2, lambda_k2, subln_w, w_attn_out, w_o, norm_ffn_w,
           router_w, router_b, w_gate_up, b_gate_up, w_down, b_down, final_norm_w):
    l = 0
    d = D_MODEL
    x_all = jnp.concatenate([jnp.zeros((LEAD, d), F32), meta_tokens, x_prompt[0], x_sample.reshape(S_ROWS, d)], axis=0)
    xn = rmsnorm_rows(x_all, norm_mix_w[l])
    wt = w_in[l].T
    o_z, o_xbc, o_dt, o_q, o_k, o_v, o_ga, o_gb = (int(o) for o in np.cumsum(
        (0, SSD_INNER, CONV_DIM, SSD_HEADS, 2048, 1024, 1024, d)))
    wt_perm = jnp.concatenate([wt[o_z:o_xbc].astype(BF16), wt[o_ga:o_gb + d].astype(BF16),
                               wt[o_xbc:o_dt].astype(BF16), wt[o_q:o_ga].astype(BF16)], axis=0)
    wt_dt = wt[o_dt:o_dt + LANES].astype(BF16)
    cos_t, sin_a, sin_b = _rope_tables()
    proj, kb, vt = in_projection(xn, wt_perm, cos_t, sin_a, sin_b)
    dtraw = dt_projection(xn, wt_dt)

    e_mat = np.zeros((LANES, SSD_INNER), np.float32)
    for h in range(SSD_HEADS):
        e_mat[h, h * SSD_HEAD_DIM:(h + 1) * SSD_HEAD_DIM] = 1.0
    ssd_params = (conv_w[l], conv_b[l].reshape(1, CONV_DIM), _pad_lanes(dt_bias[l]), _pad_lanes(a_log[l]),
                  jnp.repeat(d_skip[l], SSD_HEAD_DIM).reshape(1, SSD_INNER), ssd_norm_w[l].reshape(1, SSD_INNER),
                  jnp.asarray(e_mat, BF16))
    yzn_p, ssm_p = ssd_prompt(proj, dtraw, ssd_params)
    conv_prev = jnp.concatenate([jnp.zeros((DEC_BATCH, SUBLANES - (SSD_CONV - 1), CONV_DIM), F32), state_conv[l]], axis=1)
    yzn_s, ssm_s = ssd_sample(proj, dtraw, conv_prev, state_ssm[l].reshape(DEC_BATCH, SSD_INNER, SSD_STATE), ssd_params)
    yzn = jnp.concatenate([yzn_p, yzn_s], axis=0)

    k_rows = proj[:, PCOL_K:PCOL_V]
    v_rows = proj[:, PCOL_V:P_COLS]
    lam_params = (lambda_q1[l].reshape(1, -1), lambda_k1[l].reshape(1, -1), lambda_q2[l].reshape(1, -1),
                  lambda_k2[l].reshape(1, -1))
    subw = subln_w[l].reshape(1, ATT_V_DIM)
    vt_ext = jnp.concatenate([vt.reshape(ATT_KV_HEADS, ATT_V_DIM, R),
                              jnp.ones((ATT_KV_HEADS, ATT_VT_ROWS - ATT_V_DIM, R), BF16)], axis=1)
    on_p = attn_prompt(proj, kb, vt_ext.reshape(ATT_KV_HEADS * ATT_VT_ROWS, R), lam_params,
                       subln_w[l].reshape(ATT_V_DIM, 1))

    q_s = proj[P_ROWS:, PCOL_Q:PCOL_K].reshape(DEC_BATCH, DEC_SEQ, ATT_KV_HEADS, ATT_REP, 2, ATT_HEAD_DIM)
    q_s = jnp.transpose(q_s, (0, 2, 4, 1, 3, 5)).reshape(DEC_BATCH, ATT_S_ROWS, ATT_HEAD_DIM).astype(BF16)
    n_phys = cache_k.shape[1]
    kc = cache_k[l].reshape(n_phys, KROWS, ATT_HEAD_DIM)
    pad_new = lambda u: jnp.concatenate([u.reshape(DEC_BATCH, DEC_SEQ, -1),
                                         jnp.zeros((DEC_BATCH, PAGE_SIZE - DEC_SEQ, u.shape[-1]), F32)], axis=1)
    k_new = pad_new(k_rows[P_ROWS:]).reshape(DEC_BATCH, KROWS, ATT_HEAD_DIM)
    v_new = pad_new(v_rows[P_ROWS:]).reshape(1, DEC_BATCH, PAGE_SIZE, ATT_KV_HEADS, ATT_V_DIM)
    o_s = attn_sample(page_table.reshape(-1), q_s, kc, cache_v[l:l + 1], k_new, v_new, lam_params, subw)
    o_s = o_s.reshape(DEC_BATCH, ATT_KV_HEADS, DEC_SEQ, ATT_REP, ATT_V_DIM)
    o_s = jnp.transpose(o_s, (0, 2, 1, 3, 4)).reshape(S_ROWS, ATT_HEADS * ATT_V_DIM).astype(BF16)
    on = lax.dynamic_update_slice(on_p, o_s, (P_ROWS, 0))

    mix = mix_branches(yzn, on, proj, w_ssd_out[l].astype(BF16), w_attn_out[l].astype(BF16))
    rw = jnp.concatenate([router_w[l], jnp.zeros((d, LANES - N_EXPERTS), F32)], axis=1)
    rb = _pad_lanes(router_b[l], NEG)
    h2, n_rows, top_i, top_g = post_mixer(mix, x_all, w_o[l].astype(BF16), norm_ffn_w[l].reshape(1, d), rw, rb)

    rank, counts = expert_ranks(top_i)
    counts = counts[0, :N_EXPERTS]
    padded = (counts + MOE_TM - 1) // MOE_TM * MOE_TM
    pad_end = jnp.cumsum(padded)
    pad_start = pad_end - padded
    dest = pad_start[top_i[:, :TOP_K]] + rank[:, :TOP_K]
    is_tok = (jnp.arange(R) >= LEAD)[:, None]
    dump = MOE_NB * MOE_TM + jnp.arange(R * TOP_K).reshape(R, TOP_K)
    dest_scatter = jnp.where(is_tok, dest, dump).astype(I32).reshape(-1)
    dest_gather = jnp.where(is_tok, dest, 0).astype(I32).reshape(-1)
    block_start = jnp.arange(MOE_NB) * MOE_TM
    block_e = jnp.minimum(jnp.sum(pad_end[None, :] <= block_start[:, None], axis=1), N_EXPERTS - 1).astype(I32)
    n_used = (pad_end[-1:] // MOE_TM).astype(I32)
    ex = jnp.arange(N_EXPERTS)
    nonempty = padded > 0
    run_of_expert = jnp.cumsum(nonempty) - 1
    later = (ex[None, :] > ex[:, None]) & nonempty[None, :]
    next_expert = jnp.min(jnp.where(later, ex[None, :], N_EXPERTS), axis=1)
    next_expert = jnp.where(next_expert < N_EXPERTS, next_expert, -1)
    sched = (block_e, n_used, run_of_expert[block_e].astype(I32), next_expert[block_e].astype(I32))
    tail_start = jnp.where(nonempty, pad_end - MOE_TM, -1).astype(I32)
    xs = moe_dispatch(dest_scatter, tail_start, n_used, n_rows)
    hidden = moe_gate_up(sched, xs, w_gate_up[l], b_gate_up[l].reshape(N_EXPERTS, 1, 2 * D_FF))
    yb = moe_down(sched, hidden, w_down[l], b_down[l].reshape(N_EXPERTS, 1, d))
    y_p, y_s = moe_combine(dest_gather, h2, top_g, final_norm_w.reshape(1, d), yb)

    tok0 = LEAD
    y_prompt = y_p[None]
    y_sample = y_s.reshape(DEC_BATCH, DEC_SEQ, d)
    k_prompt = k_rows[tok0:P_ROWS].reshape(1, 1, SEQ + N_META, ATT_KV_HEADS, 2, ATT_HEAD_DIM)
    v_prompt = v_rows[tok0:P_ROWS].reshape(1, 1, SEQ + N_META, ATT_KV_HEADS, ATT_V_DIM)
    k_sample = k_rows[P_ROWS:].reshape(1, DEC_BATCH, DEC_SEQ, ATT_KV_HEADS, 2, ATT_HEAD_DIM)
    v_sample = v_rows[P_ROWS:].reshape(1, DEC_BATCH, DEC_SEQ, ATT_KV_HEADS, ATT_V_DIM)
    ssm_prompt = ssm_p.reshape(1, 1, SSD_HEADS, SSD_HEAD_DIM, SSD_STATE)
    ssm_sample = ssm_s.reshape(1, DEC_BATCH, SSD_HEADS, SSD_HEAD_DIM, SSD_STATE)
    xbc = proj[:, PCOL_XBC:PCOL_XBC + CONV_DIM]
    conv_prompt = xbc[P_ROWS - (SSD_CONV - 1):P_ROWS][None, None]
    xbc_s = xbc[P_ROWS:].reshape(DEC_BATCH, DEC_SEQ, CONV_DIM)
    conv_sample = jnp.concatenate([state_conv[l], xbc_s], axis=1)[:, -(SSD_CONV - 1):][None]
    return (y_prompt, y_sample, k_prompt, v_prompt, k_sample, v_sample, ssm_prompt, ssm_sample, conv_prompt, conv_sample)
```

```python
import functools
import math

import jax
import jax.numpy as jnp
import numpy as np
from jax import lax
from jax.experimental import pallas as pl
from jax.experimental.pallas import tpu as pltpu

F32 = jnp.float32
BF16 = jnp.bfloat16
I32 = jnp.int32
HI = lax.Precision.HIGHEST

D_MODEL = 2048
SEQ = 8192
DEC_BATCH = 32
DEC_SEQ = 4
PAGE_SIZE = 128
N_META = 16
NORM_EPS = 1e-5
SSD_INNER = D_MODEL
SSD_HEAD_DIM = 64
SSD_HEADS = SSD_INNER // SSD_HEAD_DIM
SSD_GROUPS = 4
SSD_REP = SSD_HEADS // SSD_GROUPS
SSD_STATE = 128
SSD_CONV = 4
SSD_CHUNK = 128
CONV_DIM = SSD_INNER + 2 * SSD_GROUPS * SSD_STATE
ATT_HEADS = 8
ATT_KV_HEADS = 4
ATT_REP = ATT_HEADS // ATT_KV_HEADS
ATT_HEAD_DIM = D_MODEL // ATT_HEADS // 2
ATT_V_DIM = 2 * ATT_HEAD_DIM
ATT_SCALE = ATT_HEAD_DIM ** -0.5
ROT_DIM = ATT_HEAD_DIM // 4
ROPE_THETA = 500000.0
N_EXPERTS = 32
TOP_K = 4
D_FF = D_MODEL
SWIGLU_LIMIT = 7.0
SWIGLU_ALPHA = 1.702
LAM_INIT = 0.8 - 0.6 * math.exp(-0.3 * 0)

LEAD = (-N_META) % SSD_CHUNK
P_ROWS = LEAD + N_META + SEQ
S_ROWS = DEC_BATCH * DEC_SEQ
R = P_ROWS + S_ROWS
N_CHUNKS = P_ROWS // SSD_CHUNK

LANES = 128
SUBLANES = 8
VMEM_LIMIT = 56 * 1024 * 1024

PCOL_Z, PCOL_GA, PCOL_GB, PCOL_XBC, PCOL_Q, PCOL_K, PCOL_V = 0, 2048, 4096, 6144, 9216, 11264, 12288
P_COLS = 13312
PROJ_TN = 1024
ROPE_BLOCKS = (PCOL_Q // PROJ_TN, PCOL_V // PROJ_TN)
SRC_Z, SRC_XBC, SRC_DT = 0, SSD_INNER, SSD_INNER + CONV_DIM
SRC_Q = SRC_DT + SSD_HEADS
SRC_GA = SRC_Q + 2 * D_MODEL
PROJ_SRC_ROW = tuple(base + PROJ_TN * b for base, n in
                     ((SRC_Z, 2), (SRC_GA, 4), (SRC_XBC, 3), (SRC_Q, 4)) for b in range(n))

MOE_TM = 256
MOE_NB = -(-(R * TOP_K) // MOE_TM) + N_EXPERTS
MOE_DUMP_BLOCKS = -(-(LEAD * TOP_K) // MOE_TM)
MOE_ROWS = (MOE_NB + MOE_DUMP_BLOCKS) * MOE_TM
MOE_TN_GATE_UP = 1024
NEG = -1e30


def _cparams(sem, vmem=VMEM_LIMIT):
    return pltpu.CompilerParams(dimension_semantics=sem, vmem_limit_bytes=vmem)


_NT = (((1,), (1,)), ((), ()))


def _sigmoid(x):
    return 1.0 / (1.0 + jnp.exp(-x))


def _rmsnorm_kernel(x_ref, w_ref, o_ref):
    x = x_ref[...]
    ms = jnp.mean(x * x, axis=-1, keepdims=True)
    o_ref[...] = (x * lax.rsqrt(ms + NORM_EPS) * w_ref[...]).astype(o_ref.dtype)


def rmsnorm_rows(x, w, tm=768):
    rows, d = x.shape
    return pl.pallas_call(
        _rmsnorm_kernel,
        grid=(rows // tm,),
        in_specs=[pl.BlockSpec((tm, d), lambda i: (i, 0)), pl.BlockSpec((1, d), lambda i: (0, 0))],
        out_specs=pl.BlockSpec((tm, d), lambda i: (i, 0)),
        out_shape=jax.ShapeDtypeStruct((rows, d), BF16),
        compiler_params=_cparams(("parallel",)),
        name="rmsnorm_rows",
    )(x, w.reshape(1, d))


def _rope(blk, c, sa, sb):
    return blk * c + pltpu.roll(blk, ROT_DIM // 2, axis=1) * sa + pltpu.roll(blk, LANES - ROT_DIM // 2, axis=1) * sb


def _proj_kernel(x_ref, w_ref, c_ref, sa_ref, sb_ref, o_ref, kb_ref, vt_ref):
    j = pl.program_id(1)
    acc = lax.dot_general(x_ref[...], w_ref[...], _NT, preferred_element_type=F32)
    is_rope = (j >= ROPE_BLOCKS[0]) & (j < ROPE_BLOCKS[1])

    @pl.when(is_rope)
    def _():
        c, sa, sb = c_ref[...], sa_ref[...], sb_ref[...]
        for g in range(acc.shape[1] // LANES):
            sl = slice(g * LANES, (g + 1) * LANES)
            o_ref[:, sl] = _rope(acc[:, sl], c, sa, sb)

    @pl.when(jnp.logical_not(is_rope))
    def _():
        o_ref[...] = acc

    @pl.when(j == PCOL_K // PROJ_TN)
    def _():
        kb_ref[...] = o_ref[...].astype(kb_ref.dtype)

    @pl.when(j == PCOL_V // PROJ_TN)
    def _():
        vt_ref[...] = acc.T.astype(vt_ref.dtype)


def in_projection(xn, wt_bf, cos_t, sin_a, sin_b, tm=1408):
    rows, d = xn.shape
    tn = PROJ_TN
    tab = pl.BlockSpec((tm, LANES), lambda i, j: (i, 0))

    def src_row(j):
        unit = math.gcd(*PROJ_SRC_ROW[1:])
        off = jnp.int32(0)
        for jj, r0 in enumerate(PROJ_SRC_ROW):
            off = jnp.where(j == jj, r0 // unit, off)
        return off * unit

    return pl.pallas_call(
        _proj_kernel,
        grid=(rows // tm, P_COLS // tn),
        in_specs=[pl.BlockSpec((tm, d), lambda i, j: (i, 0)),
                  pl.BlockSpec((pl.Element(tn), pl.Element(d)), lambda i, j: (src_row(j), 0)), tab, tab, tab],
        out_specs=[pl.BlockSpec((tm, tn), lambda i, j: (i, j)),
                   pl.BlockSpec((tm, tn), lambda i, j: (i, 0)),
                   pl.BlockSpec((tn, tm), lambda i, j: (0, i))],
        out_shape=[jax.ShapeDtypeStruct((rows, P_COLS), F32),
                   jax.ShapeDtypeStruct((rows, tn), BF16),
                   jax.ShapeDtypeStruct((tn, rows), BF16)],
        compiler_params=_cparams(("parallel", "arbitrary")),
        name="in_projection",
    )(xn, wt_bf, cos_t, sin_a, sin_b)


def _mm_kernel(x_ref, w_ref, o_ref):
    o_ref[...] = lax.dot_general(x_ref[...], w_ref[...], _NT, preferred_element_type=F32).astype(o_ref.dtype)


def dt_projection(xn, wt_bf, tm=1056):
    rows, d = xn.shape
    return pl.pallas_call(
        _mm_kernel,
        grid=(rows // tm,),
        in_specs=[pl.BlockSpec((tm, d), lambda i: (i, 0)), pl.BlockSpec((LANES, d), lambda i: (SRC_DT // LANES, 0))],
        out_specs=pl.BlockSpec((tm, LANES), lambda i: (i, 0)),
        out_shape=jax.ShapeDtypeStruct((rows, LANES), F32),
        compiler_params=_cparams(("parallel",)),
        name="dt_projection",
    )(xn, wt_bf)


def _ssd_chunk(xprev, xcur, dtraw, z, valid, s_ref, cw_ref, cb_ref, dtb_ref, alog_ref, dsk_ref, nw_ref, e_ref,
               state_dot_precision):
    L = xcur.shape[0]
    ext = jnp.concatenate([xprev, xcur], axis=0)
    off = SUBLANES - (SSD_CONV - 1)
    acc = cb_ref[...]
    for k in range(SSD_CONV):
        acc = acc + ext[off + k:off + k + L] * cw_ref[k:k + 1, :]
    act = acc * _sigmoid(acc)
    xs = act[:, :SSD_INNER]
    nb = SSD_GROUPS * SSD_STATE
    bm = act[:, SSD_INNER:SSD_INNER + nb].astype(BF16)
    cm = act[:, SSD_INNER + nb:].astype(BF16)

    dpre = dtraw + dtb_ref[...]
    dt = jnp.maximum(dpre, 0.0) + jnp.log1p(jnp.exp(-jnp.abs(dpre)))
    dt = jnp.where(valid, dt, 0.0)
    da = dt * (-jnp.exp(alog_ref[...]))

    li = lax.broadcasted_iota(I32, (L, L), 0)
    si = lax.broadcasted_iota(I32, (L, L), 1)
    tril = li >= si
    eye = li == si
    a_cs = jnp.dot(tril.astype(F32), da, precision=HI, preferred_element_type=F32)
    a_tot = a_cs[L - 1:L, :]
    e_mat = e_ref[...]

    def expand(v):
        out = None
        for _ in range(3):
            piece = v.astype(BF16)
            term = jnp.dot(piece, e_mat, preferred_element_type=F32)
            out = term if out is None else out + term
            v = v - piece.astype(F32)
        return out

    dt_x = expand(dt)
    eacs_x = expand(jnp.exp(a_cs))
    dte_x = expand(jnp.exp(a_tot - a_cs))
    etot = jnp.exp(a_tot)

    xdt = xs * dt_x
    xdt_b = xdt.astype(BF16)
    xde = xdt * dte_x
    gw = SSD_REP * SSD_HEAD_DIM
    lane = lax.broadcasted_iota(I32, (L, LANES), 1)
    nt = (((1,), (1,)), ((), ()))
    tn = (((0,), (0,)), ((), ()))
    y_groups = []
    for g in range(SSD_GROUPS):
        bg = bm[:, g * SSD_STATE:(g + 1) * SSD_STATE]
        cg = cm[:, g * SSD_STATE:(g + 1) * SSD_STATE]
        cb = lax.dot_general(cg, bg, nt, preferred_element_type=F32)
        s_g = s_ref[g * gw:(g + 1) * gw, :]
        y_off = lax.dot_general(cg, s_g.astype(BF16), nt, preferred_element_type=F32)
        if state_dot_precision is None:
            upd = lax.dot_general(xde[:, g * gw:(g + 1) * gw].astype(BF16), bg, tn, preferred_element_type=F32)
        else:
            upd = lax.dot_general(xde[:, g * gw:(g + 1) * gw], act[:, SSD_INNER + g * SSD_STATE:SSD_INNER + (g + 1) * SSD_STATE],
                                  tn, precision=state_dot_precision, preferred_element_type=F32)
        pairs = []
        decs = []
        for q in range(SSD_REP // 2):
            yd = []
            for r in range(2):
                h = g * SSD_REP + 2 * q + r
                colb = jnp.broadcast_to(a_cs[:, h:h + 1], (L, L))
                rowb = jnp.sum(jnp.where(eye, colb, 0.0), axis=0, keepdims=True)
                lm = jnp.where(tril, jnp.exp(jnp.minimum(colb - rowb, 0.0)), 0.0)
                m = (cb * lm).astype(BF16)
                c0 = (g * SSD_REP + 2 * q) * SSD_HEAD_DIM
                yd.append(jnp.dot(m, xdt_b[:, c0:c0 + LANES], preferred_element_type=F32))
                decs.append(jnp.broadcast_to(etot[:, h:h + 1], (SSD_HEAD_DIM, SSD_STATE)))
            pairs.append(jnp.where(lane < SSD_HEAD_DIM, yd[0], yd[1]))
        y_g = jnp.concatenate(pairs, axis=1) + y_off * eacs_x[:, g * gw:(g + 1) * gw]
        y_groups.append(y_g)
        s_ref[g * gw:(g + 1) * gw, :] = jnp.concatenate(decs, axis=0) * s_g + upd
    y = jnp.concatenate(y_groups, axis=1) + dsk_ref[...] * xs
    yz = y * (z * _sigmoid(z))
    outs = []
    for g in range(SSD_GROUPS):
        yg = yz[:, g * gw:(g + 1) * gw]
        ms = jnp.mean(yg * yg, axis=-1, keepdims=True)
        outs.append(yg * lax.rsqrt(ms + NORM_EPS) * nw_ref[:, g * gw:(g + 1) * gw])
    return jnp.concatenate(outs, axis=1)


def _ssd_prompt_kernel(xprev_ref, xcur_ref, dt_ref, z_ref, cw_ref, cb_ref, dtb_ref, alog_ref, dsk_ref, nw_ref, e_ref,
                       y_ref, s_ref):
    c = pl.program_id(0)

    @pl.when(c == 0)
    def _():
        s_ref[...] = jnp.zeros_like(s_ref)

    xprev = jnp.where(c == 0, 0.0, xprev_ref[...])
    row = c * SSD_CHUNK + lax.broadcasted_iota(I32, (SSD_CHUNK, 1), 0)
    y = _ssd_chunk(xprev, xcur_ref[...], dt_ref[...], z_ref[...], row >= LEAD, s_ref,
                   cw_ref, cb_ref, dtb_ref, alog_ref, dsk_ref, nw_ref, e_ref, None)
    y_ref[...] = y.astype(y_ref.dtype)


def _full(shape):
    return pl.BlockSpec(shape, lambda *_: (0,) * len(shape))


def ssd_prompt(proj, dtraw, params):
    L = SSD_CHUNK
    xb = PCOL_XBC // CONV_DIM
    in_specs = [
        pl.BlockSpec((SUBLANES, CONV_DIM), lambda c: (jnp.maximum(c * (L // SUBLANES) - 1, 0), xb)),
        pl.BlockSpec((L, CONV_DIM), lambda c: (c, xb)),
        pl.BlockSpec((L, LANES), lambda c: (c, 0)),
        pl.BlockSpec((L, SSD_INNER), lambda c: (c, PCOL_Z // SSD_INNER)),
    ] + [_full(p.shape) for p in params]
    return pl.pallas_call(
        _ssd_prompt_kernel,
        grid=(N_CHUNKS,),
        in_specs=in_specs,
        out_specs=[pl.BlockSpec((L, SSD_INNER), lambda c: (c, 0)), _full((SSD_INNER, SSD_STATE))],
        out_shape=[jax.ShapeDtypeStruct((P_ROWS, SSD_INNER), BF16), jax.ShapeDtypeStruct((SSD_INNER, SSD_STATE), F32)],
        compiler_params=_cparams(("arbitrary",)),
        name="ssd_prompt",
    )(proj, proj, dtraw, proj, *params)


SSD_S_NB = 4


def _ssd_sample_kernel(xprev_ref, xcur_ref, dt_ref, z_ref, sin_ref, cw_ref, cb_ref, dtb_ref, alog_ref, dsk_ref, nw_ref,
                       e_ref, y_ref, sout_ref):
    L = SUBLANES
    valid = lax.broadcasted_iota(I32, (L, 1), 0) < DEC_SEQ
    ys = []
    for b in range(SSD_S_NB):
        rows = slice(b * DEC_SEQ, (b + 1) * DEC_SEQ)
        pad = lambda u: jnp.concatenate([u, jnp.zeros((L - DEC_SEQ, u.shape[1]), u.dtype)], axis=0)
        sout_ref[b] = sin_ref[b]
        y = _ssd_chunk(xprev_ref[b], pad(xcur_ref[rows, :]), pad(dt_ref[rows, :]), pad(z_ref[rows, :]), valid,
                       sout_ref.at[b], cw_ref, cb_ref, dtb_ref, alog_ref, dsk_ref, nw_ref, e_ref, HI)
        ys.append(y[:DEC_SEQ])
    y_ref[...] = jnp.concatenate(ys, axis=0).astype(y_ref.dtype)


def ssd_sample(proj, dtraw, conv_prev, state, params):
    nb = SSD_S_NB
    rb = nb * DEC_SEQ
    r0 = P_ROWS // rb
    in_specs = [
        pl.BlockSpec((nb, SUBLANES, CONV_DIM), lambda i: (i, 0, 0)),
        pl.BlockSpec((rb, CONV_DIM), lambda i: (r0 + i, PCOL_XBC // CONV_DIM)),
        pl.BlockSpec((rb, LANES), lambda i: (r0 + i, 0)),
        pl.BlockSpec((rb, SSD_INNER), lambda i: (r0 + i, PCOL_Z // SSD_INNER)),
        pl.BlockSpec((nb, SSD_INNER, SSD_STATE), lambda i: (i, 0, 0)),
    ] + [_full(p.shape) for p in params]
    return pl.pallas_call(
        _ssd_sample_kernel,
        grid=(DEC_BATCH // nb,),
        in_specs=in_specs,
        out_specs=[pl.BlockSpec((rb, SSD_INNER), lambda i: (i, 0)),
                   pl.BlockSpec((nb, SSD_INNER, SSD_STATE), lambda i: (i, 0, 0))],
        out_shape=[jax.ShapeDtypeStruct((S_ROWS, SSD_INNER), BF16),
                   jax.ShapeDtypeStruct((DEC_BATCH, SSD_INNER, SSD_STATE), F32)],
        compiler_params=_cparams(("parallel",)),
        name="ssd_sample",
    )(conv_prev, proj, dtraw, proj, state, *params)


def _lambda(lq1_ref, lk1_ref, lq2_ref, lk2_ref):
    s1 = jnp.sum(lq1_ref[...] * lk1_ref[...], axis=-1, keepdims=True)
    s2 = jnp.sum(lq2_ref[...] * lk2_ref[...], axis=-1, keepdims=True)
    return jnp.exp(s1) - jnp.exp(s2) + LAM_INIT


def _subnorm(o, subw_ref):
    ms = jnp.mean(o * o, axis=-1, keepdims=True)
    return (o * lax.rsqrt(ms + NORM_EPS) * subw_ref[...]) * (1.0 - LAM_INIT)


ATT_TQ = 256
ATT_TK = 768
BF16_SUBLANES = 16
ATT_VT_ROWS = ATT_V_DIM + BF16_SUBLANES
EXP2_SCALE = ATT_SCALE * math.log2(math.e)


def _attn_prompt_kernel(q_ref, k_ref, vt_ref, d_ref, lq1_ref, lk1_ref, lq2_ref, lk2_ref, subw_ref, o_ref, m_s, acc_s):
    i = pl.program_id(1)
    tq, tk, hd = ATT_TQ, ATT_TK, ATT_HEAD_DIM
    q = q_ref[...]
    qc = [jnp.concatenate([q[:, r * 2 * hd + c * hd: r * 2 * hd + (c + 1) * hd] for r in range(ATT_REP)], axis=0).astype(BF16)
          for c in range(2)]
    m_s[...] = jnp.full_like(m_s, NEG)
    acc_s[...] = jnp.zeros_like(acc_s)
    nt = (((1,), (1,)), ((), ()))
    n_blocks = (i * tq + tq - 1) // tk + 1

    def step(j, mask):
        k0 = pl.multiple_of(j * tk, tk)
        kblk = k_ref[pl.ds(k0, tk), :]
        vt = vt_ref[:, pl.ds(k0, tk)]
        sts = [lax.dot_general(kblk[:, c * hd:(c + 1) * hd], qc[c], nt, preferred_element_type=F32) for c in range(2)]
        ps, alphas = [], []
        for c in range(2):
            st = sts[c]
            if "causal" in mask:
                st = jnp.where(d_ref[...] <= i * tq - j * tk, st, NEG)
            if "lead" in mask:
                st = st + jnp.where(lax.broadcasted_iota(I32, (tk, 1), 0) < LEAD, NEG, 0.0)
            m_old = m_s[c]
            m_new = jnp.maximum(m_old, jnp.max(st, axis=0, keepdims=True))
            alphas.append(jnp.exp2((m_old - m_new) * EXP2_SCALE))
            ps.append(jnp.exp2((st - m_new) * EXP2_SCALE).astype(BF16))
            m_s[c] = m_new
        for c in range(2):
            acc_s[c] = alphas[c] * acc_s[c] + jnp.dot(vt, ps[c], preferred_element_type=F32)

    @pl.when(n_blocks == 1)
    def _():
        step(0, ("causal", "lead"))

    @pl.when(n_blocks > 1)
    def _():
        step(0, ("lead",))

        def body(j, carry):
            step(j, ())
            return carry

        lax.fori_loop(1, n_blocks - 1, body, 0)
        step(n_blocks - 1, ("causal",))

    lam = _lambda(lq1_ref, lk1_ref, lq2_ref, lk2_ref)
    for r in range(ATT_REP):
        cols = slice(r * tq, (r + 1) * tq)
        o0 = acc_s[0, :ATT_V_DIM, cols] / acc_s[0, ATT_V_DIM:ATT_V_DIM + 1, cols]
        o1 = acc_s[1, :ATT_V_DIM, cols] / acc_s[1, ATT_V_DIM:ATT_V_DIM + 1, cols]
        ot = o0 - lam * o1
        ms = jnp.mean(ot * ot, axis=0, keepdims=True)
        ot = (ot * lax.rsqrt(ms + NORM_EPS) * subw_ref[...]) * (1.0 - LAM_INIT)
        o_ref[:, r * ATT_V_DIM:(r + 1) * ATT_V_DIM] = ot.T.astype(o_ref.dtype)


def attn_prompt(proj, kb, vt_ext, lam_params, subw_col):
    tq = ATT_TQ
    gq = ATT_REP * 2 * ATT_HEAD_DIM
    vec = _full((1, ATT_HEAD_DIM))
    key_minus_query = (np.arange(ATT_TK)[:, None] - np.arange(ATT_REP * tq)[None, :] % tq).astype(np.int32)
    return pl.pallas_call(
        _attn_prompt_kernel,
        grid=(ATT_KV_HEADS, R // tq),
        in_specs=[pl.BlockSpec((tq, gq), lambda g, i: (i, PCOL_Q // gq + g)),
                  pl.BlockSpec((R, ATT_V_DIM), lambda g, i: (0, g)),
                  pl.BlockSpec((ATT_VT_ROWS, R), lambda g, i: (g, 0)),
                  _full((ATT_TK, ATT_REP * tq)),
                  vec, vec, vec, vec, _full((ATT_V_DIM, 1))],
        out_specs=pl.BlockSpec((tq, gq), lambda g, i: (i, g)),
        out_shape=jax.ShapeDtypeStruct((R, ATT_HEADS * ATT_V_DIM), BF16),
        scratch_shapes=[pltpu.VMEM((2, 1, ATT_REP * tq), F32),
                        pltpu.VMEM((2, ATT_VT_ROWS, ATT_REP * tq), F32)],
        compiler_params=_cparams(("parallel", "arbitrary")),
        name="attn_prompt",
    )(proj, kb, vt_ext, jnp.asarray(key_minus_query), *lam_params, subw_col)


N_PAGES = SEQ // PAGE_SIZE
ATT_S_ROWS = ATT_KV_HEADS * 2 * DEC_SEQ * ATT_REP
ATT_S_GROUP = 2 * DEC_SEQ * ATT_REP
ATT_S_PPS = 16
ATT_S_STEPS = N_PAGES // ATT_S_PPS
KROWS = PAGE_SIZE * ATT_KV_HEADS * 2
V_TILE = ATT_V_DIM
V_LANE_TILES = ATT_V_DIM // V_TILE


def _attn_sample_kernel(pt_ref, q_ref, *refs):
    npg = ATT_S_PPS
    kc_refs = refs[:npg]
    vc_refs = [refs[npg + V_LANE_TILES * p: npg + V_LANE_TILES * (p + 1)] for p in range(npg)]
    base = npg * (1 + V_LANE_TILES)
    kn_ref = refs[base]
    vn_refs = refs[base + 1: base + 1 + V_LANE_TILES]
    x_ref, lq1_ref, lk1_ref, lq2_ref, lk2_ref, subw_ref, o_ref, m_s, l_s, acc_s = refs[base + 1 + V_LANE_TILES:]
    j = pl.program_id(1)
    nt = (((1,), (1,)), ((), ()))
    nhc = ATT_KV_HEADS * 2
    rows_hc = DEC_SEQ * ATT_REP

    @pl.when(j == 0)
    def _():
        m_s[...] = jnp.full_like(m_s, NEG)
        l_s[...] = jnp.zeros_like(l_s)
        acc_s[...] = jnp.zeros_like(acc_s)

    def step(k_pages, v_pages, masked):
        q = q_ref[0]
        cols = []
        for k_src in k_pages:
            ss = []
            for hc in range(nhc):
                khc = k_src[pl.ds(hc, PAGE_SIZE, stride=nhc), :].astype(BF16)
                ss.append(lax.dot_general(q[hc * rows_hc:(hc + 1) * rows_hc, :], khc, nt, preferred_element_type=F32))
            cols.append(jnp.concatenate(ss, axis=0))
        s = jnp.concatenate(cols, axis=1) if len(cols) > 1 else cols[0]
        if masked:
            t_q = (lax.broadcasted_iota(I32, s.shape, 0) % rows_hc) // ATT_REP
            t_k = lax.broadcasted_iota(I32, s.shape, 1)
            s = jnp.where(t_k <= t_q, s, NEG)
        m_old = m_s[...]
        m_new = jnp.maximum(m_old, jnp.max(s, axis=1, keepdims=True))
        alpha = jnp.exp2((m_old - m_new) * EXP2_SCALE)
        p = jnp.exp2((s - m_new[:, :1]) * EXP2_SCALE)
        l_s[...] = alpha * l_s[...] + jnp.sum(p, axis=1, keepdims=True)
        pb = p.astype(BF16)
        head_of_col = lax.broadcasted_iota(I32, (ATT_S_ROWS, PAGE_SIZE * ATT_KV_HEADS), 1) % ATT_KV_HEADS
        head_of_row = lax.broadcasted_iota(I32, (ATT_S_ROWS, PAGE_SIZE * ATT_KV_HEADS), 0) // ATT_S_GROUP
        own_head = head_of_col == head_of_row
        pv = None
        for pi, v_tiles in enumerate(v_pages):
            v2 = jnp.concatenate([v[...].reshape(PAGE_SIZE * ATT_KV_HEADS, V_TILE) for v in v_tiles], axis=1).astype(BF16)
            spread = jnp.dot(pb[:, pi * PAGE_SIZE:(pi + 1) * PAGE_SIZE], x_ref[...], preferred_element_type=F32)
            pe = jnp.where(own_head, spread, 0.0).astype(BF16)
            d = jnp.dot(pe, v2, preferred_element_type=F32)
            pv = d if pv is None else pv + d
        acc_s[...] = alpha[:, :1] * acc_s[...] + pv
        m_s[...] = m_new

    @pl.when(j < ATT_S_STEPS)
    def _():
        step(kc_refs, vc_refs, False)

    @pl.when(j == ATT_S_STEPS)
    def _():
        step([kn_ref], [vn_refs], True)
        lam = _lambda(lq1_ref, lk1_ref, lq2_ref, lk2_ref)
        for g in range(ATT_KV_HEADS):
            r0 = g * ATT_S_GROUP
            o0 = acc_s[r0:r0 + rows_hc, :] / l_s[r0:r0 + rows_hc, :1]
            o1 = acc_s[r0 + rows_hc:r0 + 2 * rows_hc, :] / l_s[r0 + rows_hc:r0 + 2 * rows_hc, :1]
            o_ref[0, g] = _subnorm(o0 - lam * o1, subw_ref)


def attn_sample(page_table, q, kc, vc, knew, vnew, lam_params, subw):
    vec = _full((1, ATT_HEAD_DIM))
    kblk = (None, KROWS, ATT_HEAD_DIM)
    vblk = (None, None, PAGE_SIZE, ATT_KV_HEADS, V_TILE)

    def page(b, j, pt, p):
        return pt[b * N_PAGES + jnp.minimum(j, ATT_S_STEPS - 1) * ATT_S_PPS + p]

    k_specs = [pl.BlockSpec(kblk, functools.partial(lambda b, j, pt, p: (page(b, j, pt, p), 0, 0), p=p))
               for p in range(ATT_S_PPS)]
    v_specs = [pl.BlockSpec(vblk, functools.partial(lambda b, j, pt, p, e: (0, page(b, j, pt, p), 0, 0, e), p=p, e=e))
               for p in range(ATT_S_PPS) for e in range(V_LANE_TILES)]
    spread = np.zeros((PAGE_SIZE, PAGE_SIZE * ATT_KV_HEADS), np.float32)
    for t in range(PAGE_SIZE):
        spread[t, t * ATT_KV_HEADS:(t + 1) * ATT_KV_HEADS] = 1.0
    kn_spec = pl.BlockSpec(kblk, lambda b, j, pt: (b, 0, 0))
    vn_specs = [pl.BlockSpec(vblk, functools.partial(lambda b, j, pt, e: (0, b, 0, 0, e), e=e)) for e in range(V_LANE_TILES)]
    grid_spec = pltpu.PrefetchScalarGridSpec(
        num_scalar_prefetch=1,
        grid=(DEC_BATCH, ATT_S_STEPS + 1),
        in_specs=[pl.BlockSpec((1, ATT_S_ROWS, ATT_HEAD_DIM), lambda b, j, pt: (b, 0, 0)),
                  *k_specs, *v_specs, kn_spec, *vn_specs, _full((PAGE_SIZE, PAGE_SIZE * ATT_KV_HEADS)),
                  vec, vec, vec, vec, _full((1, ATT_V_DIM))],
        out_specs=pl.BlockSpec((1, ATT_KV_HEADS, DEC_SEQ * ATT_REP, ATT_V_DIM), lambda b, j, pt: (b, 0, 0, 0)),
        scratch_shapes=[pltpu.VMEM((ATT_S_ROWS, LANES), F32), pltpu.VMEM((ATT_S_ROWS, LANES), F32),
                        pltpu.VMEM((ATT_S_ROWS, ATT_V_DIM), F32)],
    )
    return pl.pallas_call(
        _attn_sample_kernel,
        grid_spec=grid_spec,
        out_shape=jax.ShapeDtypeStruct((DEC_BATCH, ATT_KV_HEADS, DEC_SEQ * ATT_REP, ATT_V_DIM), F32),
        compiler_params=_cparams(("parallel", "arbitrary")),
        name="attn_sample",
    )(page_table, q, *([kc] * ATT_S_PPS), *([vc] * (ATT_S_PPS * V_LANE_TILES)), knew, *([vnew] * V_LANE_TILES),
      jnp.asarray(spread, BF16), *lam_params, subw)


def _mix_kernel(ya_ref, yb_ref, ga_ref, gb_ref, wa_ref, wb_ref, o_ref):
    oa = jnp.dot(ya_ref[...], wa_ref[...], preferred_element_type=F32)
    ob = jnp.dot(yb_ref[...], wb_ref[...], preferred_element_type=F32)
    o_ref[...] = (_sigmoid(ga_ref[...]) * oa + _sigmoid(gb_ref[...]) * ob).astype(o_ref.dtype)


def mix_branches(yzn, on, proj, wa, wb, tm=256):
    d = D_MODEL
    row = lambda i: (i, 0)
    return pl.pallas_call(
        _mix_kernel,
        grid=(R // tm,),
        in_specs=[pl.BlockSpec((tm, d), row), pl.BlockSpec((tm, d), row),
                  pl.BlockSpec((tm, d), lambda i: (i, PCOL_GA // d)), pl.BlockSpec((tm, d), lambda i: (i, PCOL_GB // d)),
                  _full((d, d)), _full((d, d))],
        out_specs=pl.BlockSpec((tm, d), row),
        out_shape=jax.ShapeDtypeStruct((R, d), BF16),
        compiler_params=_cparams(("parallel",)),
        name="mix_branches",
    )(yzn, on, proj, proj, wa, wb)


def _post_kernel(mix_ref, h_ref, wo_ref, nw_ref, rw_ref, rb_ref, h2_ref, n_ref, ti_ref, tg_ref):
    h2 = h_ref[...] + jnp.dot(mix_ref[...], wo_ref[...], preferred_element_type=F32)
    h2_ref[...] = h2
    ms = jnp.mean(h2 * h2, axis=-1, keepdims=True)
    n = h2 * lax.rsqrt(ms + NORM_EPS) * nw_ref[...]
    n_ref[...] = n
    logits = jnp.dot(n, rw_ref[...], precision=HI, preferred_element_type=F32) + rb_ref[...]
    lane = lax.broadcasted_iota(I32, logits.shape, 1)
    work = logits
    vals, idxs = [], []
    for _ in range(TOP_K):
        mx = jnp.max(work, axis=1, keepdims=True)
        am = jnp.min(jnp.where(work == mx, lane, LANES), axis=1, keepdims=True)
        vals.append(mx)
        idxs.append(am)
        work = jnp.where(lane == am, -jnp.inf, work)
    es = [jnp.exp(v - vals[0]) for v in vals]
    den = es[0] + es[1] + es[2] + es[3]
    ti = jnp.zeros(logits.shape, I32)
    tg = jnp.zeros(logits.shape, F32)
    for k in range(TOP_K):
        ti = jnp.where(lane == k, idxs[k], ti)
        tg = jnp.where(lane == k, es[k] / den, tg)
    ti_ref[...] = ti
    tg_ref[...] = tg


def post_mixer(mix, h, wo, nw, rw, rb, tm=256):
    d = D_MODEL
    row = lambda i: (i, 0)
    return pl.pallas_call(
        _post_kernel,
        grid=(R // tm,),
        in_specs=[pl.BlockSpec((tm, d), row), pl.BlockSpec((tm, d), row), _full((d, d)), _full((1, d)),
                  _full((d, LANES)), _full((1, LANES))],
        out_specs=[pl.BlockSpec((tm, d), row), pl.BlockSpec((tm, d), row),
                   pl.BlockSpec((tm, LANES), row), pl.BlockSpec((tm, LANES), row)],
        out_shape=[jax.ShapeDtypeStruct((R, d), F32), jax.ShapeDtypeStruct((R, d), F32),
                   jax.ShapeDtypeStruct((R, LANES), I32), jax.ShapeDtypeStruct((R, LANES), F32)],
        compiler_params=_cparams(("parallel",)),
        name="post_mixer",
    )(mix, h, wo, nw, rw, rb)


def _rank_kernel(ti_ref, rank_ref, cnt_ref, carry_s):
    i = pl.program_id(0)
    tm = ti_ref.shape[0]

    @pl.when(i == 0)
    def _():
        carry_s[...] = jnp.zeros_like(carry_s)

    ti = ti_ref[...]
    lane = lax.broadcasted_iota(I32, ti.shape, 1)
    row = i * tm + lax.broadcasted_iota(I32, (tm, 1), 0)
    valid = row >= LEAD
    ohs = [jnp.where((lane == ti[:, k:k + 1]) & valid, 1.0, 0.0) for k in range(TOP_K)]
    osum = ohs[0] + ohs[1] + ohs[2] + ohs[3]
    li = lax.broadcasted_iota(I32, (tm, tm), 0)
    si = lax.broadcasted_iota(I32, (tm, tm), 1)
    before = jnp.dot((li > si).astype(BF16), osum.astype(BF16), preferred_element_type=F32) + carry_s[...]
    rank = jnp.zeros(ti.shape, I32)
    for k in range(TOP_K):
        rk = jnp.sum(ohs[k] * before, axis=1, keepdims=True)
        rank = jnp.where(lane == k, rk.astype(I32), rank)
    rank_ref[...] = rank
    carry_s[...] = carry_s[...] + jnp.sum(osum, axis=0, keepdims=True)
    cnt_ref[...] = carry_s[...].astype(I32)


def expert_ranks(ti, tm=256):
    return pl.pallas_call(
        _rank_kernel,
        grid=(R // tm,),
        in_specs=[pl.BlockSpec((tm, LANES), lambda i: (i, 0))],
        out_specs=[pl.BlockSpec((tm, LANES), lambda i: (i, 0)), _full((1, LANES))],
        out_shape=[jax.ShapeDtypeStruct((R, LANES), I32), jax.ShapeDtypeStruct((1, LANES), I32)],
        scratch_shapes=[pltpu.VMEM((1, LANES), F32)],
        compiler_params=_cparams(("arbitrary",)),
        name="expert_ranks",
    )(ti)


DISPATCH_TM = 256


def _dispatch_kernel(dest_ref, tail_ref, nu_ref, n_ref, xs_ref, zbuf, sem, zsem):
    i = pl.program_id(0)
    tm = DISPATCH_TM

    @pl.when(i == 0)
    def _():
        zbuf[...] = jnp.zeros_like(zbuf)

        def zero_block(row0):
            return pltpu.make_async_copy(zbuf, xs_ref.at[pl.ds(pl.multiple_of(row0, MOE_TM), MOE_TM), :], zsem)

        def each_unused_block(fn):
            def body(b, carry):
                fn(zero_block(b * MOE_TM))
                return carry

            lax.fori_loop(nu_ref[0], MOE_ROWS // MOE_TM, body, 0)

        for e in range(N_EXPERTS):
            @pl.when(tail_ref[e] >= 0)
            def _():
                zero_block(tail_ref[e]).start()

        each_unused_block(lambda c: c.start())
        for e in range(N_EXPERTS):
            @pl.when(tail_ref[e] >= 0)
            def _():
                zero_block(tail_ref[e]).wait()

        each_unused_block(lambda c: c.wait())

    def body(r, carry):
        for k in range(TOP_K):
            d = dest_ref[(i * tm + r) * TOP_K + k]
            pltpu.make_async_copy(n_ref.at[pl.ds(r, 1), :], xs_ref.at[pl.ds(d, 1), :], sem).start()
        return carry

    lax.fori_loop(0, tm, body, 0)
    for k in range(TOP_K):
        pltpu.make_async_copy(n_ref, xs_ref.at[pl.ds(0, tm), :], sem).wait()


def moe_dispatch(dest_flat, tail_start, n_used, n):
    tm = DISPATCH_TM
    d = n.shape[1]
    grid_spec = pltpu.PrefetchScalarGridSpec(
        num_scalar_prefetch=3,
        grid=(R // tm,),
        in_specs=[pl.BlockSpec((tm, d), lambda i, ds, ts, nu: (i, 0))],
        out_specs=pl.BlockSpec(memory_space=pl.ANY),
        scratch_shapes=[pltpu.VMEM((MOE_TM, d), n.dtype), pltpu.SemaphoreType.DMA, pltpu.SemaphoreType.DMA],
    )
    return pl.pallas_call(
        _dispatch_kernel,
        grid_spec=grid_spec,
        out_shape=jax.ShapeDtypeStruct((MOE_ROWS, d), n.dtype),
        compiler_params=_cparams(("arbitrary",)),
        name="moe_dispatch",
    )(dest_flat, tail_start, n_used, n)


def _new_expert(be_ref, i):
    return (i == 0) | (be_ref[i] != be_ref[jnp.maximum(i - 1, 0)])


def _stream_expert_weights(i, be_ref, run_ref, nxt_ref, copies, wbuf, wbf):
    slot = run_ref[i] % 2

    @pl.when(i == 0)
    def _():
        for c in copies(be_ref[0], 0):
            c.start()

    @pl.when(_new_expert(be_ref, i))
    def _():
        for c in copies(be_ref[i], slot):
            c.wait()

        @pl.when(nxt_ref[i] >= 0)
        def _():
            for c in copies(nxt_ref[i], 1 - slot):
                c.start()

        for t in range(wbf.shape[0]):
            wbf[t] = wbuf[slot, t].astype(BF16)


def _gmm1_kernel(be_ref, nu_ref, run_ref, nxt_ref, x_ref, w_hbm, bg_ref, bu_ref, h_ref, wbuf, wbf, sem):
    j = pl.program_id(0)
    i = pl.program_id(1)
    tn = MOE_TN_GATE_UP

    def copies(e, slot):
        cols = (pl.multiple_of(j * tn, tn), pl.multiple_of(D_FF + j * tn, tn))
        return [pltpu.make_async_copy(w_hbm.at[e, :, pl.ds(c0, tn)], wbuf.at[slot, t], sem.at[slot, t])
                for t, c0 in enumerate(cols)]

    @pl.when(i < nu_ref[0])
    def _():
        _stream_expert_weights(i, be_ref, run_ref, nxt_ref, copies, wbuf, wbf)
        x = x_ref[...].astype(BF16)
        hg = jnp.dot(x, wbf[0], preferred_element_type=F32) + bg_ref[0]
        hu = jnp.dot(x, wbf[1], preferred_element_type=F32) + bu_ref[0]
        g = jnp.minimum(hg, SWIGLU_LIMIT)
        u = jnp.clip(hu, -SWIGLU_LIMIT, SWIGLU_LIMIT)
        h_ref[...] = (g * _sigmoid(SWIGLU_ALPHA * g) * (u + 1.0)).astype(h_ref.dtype)

    @pl.when(i >= nu_ref[0])
    def _():
        h_ref[...] = jnp.zeros_like(h_ref)


def moe_gate_up(sched, xs, w_gu, b_gu):
    tm, tn = MOE_TM, MOE_TN_GATE_UP
    nj = D_FF // tn
    blk = lambda i, nu: jnp.minimum(i, nu[0] - 1)
    grid_spec = pltpu.PrefetchScalarGridSpec(
        num_scalar_prefetch=4,
        grid=(nj, MOE_NB),
        in_specs=[pl.BlockSpec((tm, D_MODEL), lambda j, i, be, nu, rn, nx: (blk(i, nu), 0)),
                  pl.BlockSpec(memory_space=pl.ANY),
                  pl.BlockSpec((1, 1, tn), lambda j, i, be, nu, rn, nx: (be[blk(i, nu)], 0, j)),
                  pl.BlockSpec((1, 1, tn), lambda j, i, be, nu, rn, nx: (be[blk(i, nu)], 0, nj + j))],
        out_specs=pl.BlockSpec((tm, tn), lambda j, i, be, nu, rn, nx: (i, j)),
        scratch_shapes=[pltpu.VMEM((2, 2, D_MODEL, tn), F32), pltpu.VMEM((2, D_MODEL, tn), BF16),
                        pltpu.SemaphoreType.DMA((2, 2))],
    )
    return pl.pallas_call(
        _gmm1_kernel,
        grid_spec=grid_spec,
        out_shape=jax.ShapeDtypeStruct((MOE_NB * tm, D_FF), BF16),
        compiler_params=_cparams(("arbitrary", "arbitrary")),
        name="moe_gate_up",
    )(*sched, xs, w_gu, b_gu, b_gu)


def _gmm2_kernel(be_ref, nu_ref, run_ref, nxt_ref, h_ref, w_hbm, bd_ref, y_ref, wbuf, wbf, sem):
    i = pl.program_id(0)

    def copies(e, slot):
        return [pltpu.make_async_copy(w_hbm.at[e], wbuf.at[slot, 0], sem.at[slot])]

    @pl.when(i < nu_ref[0])
    def _():
        _stream_expert_weights(i, be_ref, run_ref, nxt_ref, copies, wbuf, wbf)
        y_ref[...] = jnp.dot(h_ref[...], wbf[0], preferred_element_type=F32) + bd_ref[0]

    @pl.when(i >= nu_ref[0])
    def _():
        y_ref[...] = jnp.zeros_like(y_ref)


def moe_down(sched, hidden, w_dn, b_dn):
    tm = MOE_TM
    blk = lambda i, nu: jnp.minimum(i, nu[0] - 1)
    grid_spec = pltpu.PrefetchScalarGridSpec(
        num_scalar_prefetch=4,
        grid=(MOE_NB,),
        in_specs=[pl.BlockSpec((tm, D_FF), lambda i, be, nu, rn, nx: (blk(i, nu), 0)),
                  pl.BlockSpec(memory_space=pl.ANY),
                  pl.BlockSpec((1, 1, D_MODEL), lambda i, be, nu, rn, nx: (be[blk(i, nu)], 0, 0))],
        out_specs=pl.BlockSpec((tm, D_MODEL), lambda i, be, nu, rn, nx: (i, 0)),
        scratch_shapes=[pltpu.VMEM((2, 1, D_FF, D_MODEL), F32), pltpu.VMEM((1, D_FF, D_MODEL), BF16),
                        pltpu.SemaphoreType.DMA((2,))],
    )
    return pl.pallas_call(
        _gmm2_kernel,
        grid_spec=grid_spec,
        out_shape=jax.ShapeDtypeStruct((MOE_NB * tm, D_MODEL), F32),
        compiler_params=_cparams(("arbitrary",)),
        name="moe_down",
    )(*sched, hidden, w_dn, b_dn)


COMBINE_TM = 128
COMBINE_TILES = R // COMBINE_TM
PROMPT_TILE0 = (LEAD + N_META) // COMBINE_TM
SAMPLE_TILE = P_ROWS // COMBINE_TM


def _combine_kernel(src_ref, h2_ref, tg_ref, fw_ref, yb_ref, yp_ref, ys_ref, buf, sem):
    i = pl.program_id(0)
    tm = COMBINE_TM

    def gather(tile, slot):
        def body(r, carry):
            for k in range(TOP_K):
                s = src_ref[(tile * tm + r) * TOP_K + k]
                pltpu.make_async_copy(yb_ref.at[pl.ds(s, 1), :], buf.at[slot, k, pl.ds(r, 1), :], sem.at[slot]).start()
            return carry

        lax.fori_loop(0, tm, body, 0)

    @pl.when(i == 0)
    def _():
        gather(0, 0)

    @pl.when(i + 1 < COMBINE_TILES)
    def _():
        gather(i + 1, (i + 1) % 2)

    slot = i % 2
    for k in range(TOP_K):
        pltpu.make_async_copy(yb_ref.at[pl.ds(0, tm), :], buf.at[slot, k], sem.at[slot]).wait()
    tg = tg_ref[...]
    f = tg[:, 0:1] * buf[slot, 0]
    for k in range(1, TOP_K):
        f = f + tg[:, k:k + 1] * buf[slot, k]
    out = h2_ref[...] + f
    ms = jnp.mean(out * out, axis=-1, keepdims=True)
    y = out * lax.rsqrt(ms + NORM_EPS) * fw_ref[...]

    @pl.when((i >= PROMPT_TILE0) & (i < SAMPLE_TILE))
    def _():
        yp_ref[...] = y

    @pl.when(i == SAMPLE_TILE)
    def _():
        ys_ref[...] = y


def moe_combine(src_flat, h2, tg, fw, yb):
    tm = COMBINE_TM
    d = D_MODEL
    grid_spec = pltpu.PrefetchScalarGridSpec(
        num_scalar_prefetch=1,
        grid=(COMBINE_TILES,),
        in_specs=[pl.BlockSpec((tm, d), lambda i, s: (i, 0)), pl.BlockSpec((tm, LANES), lambda i, s: (i, 0)),
                  pl.BlockSpec((1, d), lambda i, s: (0, 0)), pl.BlockSpec(memory_space=pl.ANY)],
        out_specs=[pl.BlockSpec((tm, d), lambda i, s: (jnp.clip(i - PROMPT_TILE0, 0, SEQ // tm - 1), 0)),
                   pl.BlockSpec((tm, d), lambda i, s: (0, 0))],
        scratch_shapes=[pltpu.VMEM((2, TOP_K, tm, d), F32), pltpu.SemaphoreType.DMA((2,))],
    )
    return pl.pallas_call(
        _combine_kernel,
        grid_spec=grid_spec,
        out_shape=[jax.ShapeDtypeStruct((SEQ, d), F32), jax.ShapeDtypeStruct((S_ROWS, d), F32)],
        compiler_params=_cparams(("arbitrary",)),
        name="moe_combine",
    )(src_flat, h2, tg, fw, yb)


def _rope_tables():
    half = ROT_DIM // 2
    pos_p = jnp.maximum(jnp.arange(P_ROWS) - LEAD, 0)
    pos_s = SEQ + jnp.arange(S_ROWS) % DEC_SEQ
    pos = jnp.concatenate([pos_p, pos_s])
    inv_freq = 1.0 / (ROPE_THETA ** (jnp.arange(half, dtype=F32) * (2.0 / ROT_DIM)))
    ang = pos.astype(F32)[:, None] * inv_freq[None, :]
    cos, sin = jnp.cos(ang), jnp.sin(ang)
    zeros = jnp.zeros((R, LANES - ROT_DIM), F32)
    cos_t = jnp.concatenate([cos, cos, jnp.ones((R, LANES - ROT_DIM), F32)], axis=1)
    sin_a = jnp.concatenate([jnp.zeros_like(sin), sin, zeros], axis=1)
    sin_b = jnp.concatenate([-sin, jnp.zeros_like(sin), zeros], axis=1)
    return cos_t, sin_a, sin_b


def _pad_lanes(v, fill=0.0):
    v = v.reshape(1, -1)
    return jnp.concatenate([v, jnp.full((1, LANES - v.shape[1]), fill, v.dtype)], axis=1)


def kernel(x_prompt, x_sample, cache_k, cache_v, state_ssm, state_conv, page_table, meta_tokens,
           norm_mix_w, w_in, conv_w, conv_b, dt_bias, a_log, d_skip, ssd_norm_w, w_ssd_out,
           lambda_q1, lambda_k1, lambda_q2, lambda_k2, subln_w, w_attn_out, w_o, norm_ffn_w,
           router_w, router_b, w_gate_up, b_gate_up, w_down, b_down, final_norm_w):
    l = 0
    d = D_MODEL
    x_all = jnp.concatenate([jnp.zeros((LEAD, d), F32), meta_tokens, x_prompt[0], x_sample.reshape(S_ROWS, d)], axis=0)
    xn = rmsnorm_rows(x_all, norm_mix_w[l])
    wt_bf = w_in[l].T.astype(BF16)
    cos_t, sin_a, sin_b = _rope_tables()
    proj, kb, vt = in_projection(xn, wt_bf, cos_t, sin_a, sin_b)
    dtraw = dt_projection(xn, wt_bf)

    e_mat = np.zeros((LANES, SSD_INNER), np.float32)
    for h in range(SSD_HEADS):
        e_mat[h, h * SSD_HEAD_DIM:(h + 1) * SSD_HEAD_DIM] = 1.0
    ssd_params = (conv_w[l], conv_b[l].reshape(1, CONV_DIM), _pad_lanes(dt_bias[l]), _pad_lanes(a_log[l]),
                  jnp.repeat(d_skip[l], SSD_HEAD_DIM).reshape(1, SSD_INNER), ssd_norm_w[l].reshape(1, SSD_INNER),
                  jnp.asarray(e_mat, BF16))
    yzn_p, ssm_p = ssd_prompt(proj, dtraw, ssd_params)
    conv_prev = jnp.concatenate([jnp.zeros((DEC_BATCH, SUBLANES - (SSD_CONV - 1), CONV_DIM), F32), state_conv[l]], axis=1)
    yzn_s, ssm_s = ssd_sample(proj, dtraw, conv_prev, state_ssm[l].reshape(DEC_BATCH, SSD_INNER, SSD_STATE), ssd_params)
    yzn = jnp.concatenate([yzn_p, yzn_s], axis=0)

    k_rows = proj[:, PCOL_K:PCOL_V]
    v_rows = proj[:, PCOL_V:P_COLS]
    lam_params = (lambda_q1[l].reshape(1, -1), lambda_k1[l].reshape(1, -1), lambda_q2[l].reshape(1, -1),
                  lambda_k2[l].reshape(1, -1))
    subw = subln_w[l].reshape(1, ATT_V_DIM)
    vt_ext = jnp.concatenate([vt.reshape(ATT_KV_HEADS, ATT_V_DIM, R),
                              jnp.ones((ATT_KV_HEADS, ATT_VT_ROWS - ATT_V_DIM, R), BF16)], axis=1)
    on_p = attn_prompt(proj, kb, vt_ext.reshape(ATT_KV_HEADS * ATT_VT_ROWS, R), lam_params,
                       subln_w[l].reshape(ATT_V_DIM, 1))

    q_s = proj[P_ROWS:, PCOL_Q:PCOL_K].reshape(DEC_BATCH, DEC_SEQ, ATT_KV_HEADS, ATT_REP, 2, ATT_HEAD_DIM)
    q_s = jnp.transpose(q_s, (0, 2, 4, 1, 3, 5)).reshape(DEC_BATCH, ATT_S_ROWS, ATT_HEAD_DIM).astype(BF16)
    n_phys = cache_k.shape[1]
    kc = cache_k[l].reshape(n_phys, KROWS, ATT_HEAD_DIM)
    pad_new = lambda u: jnp.concatenate([u.reshape(DEC_BATCH, DEC_SEQ, -1),
                                         jnp.zeros((DEC_BATCH, PAGE_SIZE - DEC_SEQ, u.shape[-1]), F32)], axis=1)
    k_new = pad_new(k_rows[P_ROWS:]).reshape(DEC_BATCH, KROWS, ATT_HEAD_DIM)
    v_new = pad_new(v_rows[P_ROWS:]).reshape(1, DEC_BATCH, PAGE_SIZE, ATT_KV_HEADS, ATT_V_DIM)
    o_s = attn_sample(page_table.reshape(-1), q_s, kc, cache_v[l:l + 1], k_new, v_new, lam_params, subw)
    o_s = o_s.reshape(DEC_BATCH, ATT_KV_HEADS, DEC_SEQ, ATT_REP, ATT_V_DIM)
    o_s = jnp.transpose(o_s, (0, 2, 1, 3, 4)).reshape(S_ROWS, ATT_HEADS * ATT_V_DIM).astype(BF16)
    on = lax.dynamic_update_slice(on_p, o_s, (P_ROWS, 0))

    mix = mix_branches(yzn, on, proj, w_ssd_out[l].astype(BF16), w_attn_out[l].astype(BF16))
    rw = jnp.concatenate([router_w[l], jnp.zeros((d, LANES - N_EXPERTS), F32)], axis=1)
    rb = _pad_lanes(router_b[l], NEG)
    h2, n_rows, top_i, top_g = post_mixer(mix, x_all, w_o[l].astype(BF16), norm_ffn_w[l].reshape(1, d), rw, rb)

    rank, counts = expert_ranks(top_i)
    counts = counts[0, :N_EXPERTS]
    padded = (counts + MOE_TM - 1) // MOE_TM * MOE_TM
    pad_end = jnp.cumsum(padded)
    pad_start = pad_end - padded
    dest = pad_start[top_i[:, :TOP_K]] + rank[:, :TOP_K]
    is_tok = (jnp.arange(R) >= LEAD)[:, None]
    dump = MOE_NB * MOE_TM + jnp.arange(R * TOP_K).reshape(R, TOP_K)
    dest_scatter = jnp.where(is_tok, dest, dump).astype(I32).reshape(-1)
    dest_gather = jnp.where(is_tok, dest, 0).astype(I32).reshape(-1)
    block_start = jnp.arange(MOE_NB) * MOE_TM
    block_e = jnp.minimum(jnp.sum(pad_end[None, :] <= block_start[:, None], axis=1), N_EXPERTS - 1).astype(I32)
    n_used = (pad_end[-1:] // MOE_TM).astype(I32)
    ex = jnp.arange(N_EXPERTS)
    nonempty = padded > 0
    run_of_expert = jnp.cumsum(nonempty) - 1
    later = (ex[None, :] > ex[:, None]) & nonempty[None, :]
    next_expert = jnp.min(jnp.where(later, ex[None, :], N_EXPERTS), axis=1)
    next_expert = jnp.where(next_expert < N_EXPERTS, next_expert, -1)
    sched = (block_e, n_used, run_of_expert[block_e].astype(I32), next_expert[block_e].astype(I32))
    tail_start = jnp.where(nonempty, pad_end - MOE_TM, -1).astype(I32)
    xs = moe_dispatch(dest_scatter, tail_start, n_used, n_rows)
    hidden = moe_gate_up(sched, xs, w_gate_up[l], b_gate_up[l].reshape(N_EXPERTS, 1, 2 * D_FF))
    yb = moe_down(sched, hidden, w_down[l], b_down[l].reshape(N_EXPERTS, 1, d))
    y_p, y_s = moe_combine(dest_gather, h2, top_g, final_norm_w.reshape(1, d), yb)

    tok0 = LEAD
    y_prompt = y_p[None]
    y_sample = y_s.reshape(DEC_BATCH, DEC_SEQ, d)
    k_prompt = k_rows[tok0:P_ROWS].reshape(1, 1, SEQ + N_META, ATT_KV_HEADS, 2, ATT_HEAD_DIM)
    v_prompt = v_rows[tok0:P_ROWS].reshape(1, 1, SEQ + N_META, ATT_KV_HEADS, ATT_V_DIM)
    k_sample = k_rows[P_ROWS:].reshape(1, DEC_BATCH, DEC_SEQ, ATT_KV_HEADS, 2, ATT_HEAD_DIM)
    v_sample = v_rows[P_ROWS:].reshape(1, DEC_BATCH, DEC_SEQ, ATT_KV_HEADS, ATT_V_DIM)
    ssm_prompt = ssm_p.reshape(1, 1, SSD_HEADS, SSD_HEAD_DIM, SSD_STATE)
    ssm_sample = ssm_s.reshape(1, DEC_BATCH, SSD_HEADS, SSD_HEAD_DIM, SSD_STATE)
    xbc = proj[:, PCOL_XBC:PCOL_XBC + CONV_DIM]
    conv_prompt = xbc[P_ROWS - (SSD_CONV - 1):P_ROWS][None, None]
    xbc_s = xbc[P_ROWS:].reshape(DEC_BATCH, DEC_SEQ, CONV_DIM)
    conv_sample = jnp.concatenate([state_conv[l], xbc_s], axis=1)[:, -(SSD_CONV - 1):][None]
    return (y_prompt, y_sample, k_prompt, v_prompt, k_sample, v_sample, ssm_prompt, ssm_sample, conv_prompt, conv_sample)
```

```python
import functools
import math

import jax
import jax.numpy as jnp
import numpy as np
from jax import lax
from jax.experimental import pallas as pl
from jax.experimental.pallas import tpu as pltpu

F32 = jnp.float32
BF16 = jnp.bfloat16
I32 = jnp.int32
HI = lax.Precision.HIGHEST

D_MODEL = 2048
SEQ = 8192
DEC_BATCH = 32
DEC_SEQ = 4
PAGE_SIZE = 128
N_META = 16
NORM_EPS = 1e-5
SSD_INNER = D_MODEL
SSD_HEAD_DIM = 64
SSD_HEADS = SSD_INNER // SSD_HEAD_DIM
SSD_GROUPS = 4
SSD_REP = SSD_HEADS // SSD_GROUPS
SSD_STATE = 128
SSD_CONV = 4
SSD_CHUNK = 128
CONV_DIM = SSD_INNER + 2 * SSD_GROUPS * SSD_STATE
ATT_HEADS = 8
ATT_KV_HEADS = 4
ATT_REP = ATT_HEADS // ATT_KV_HEADS
ATT_HEAD_DIM = D_MODEL // ATT_HEADS // 2
ATT_V_DIM = 2 * ATT_HEAD_DIM
ATT_SCALE = ATT_HEAD_DIM ** -0.5
ROT_DIM = ATT_HEAD_DIM // 4
ROPE_THETA = 500000.0
N_EXPERTS = 32
TOP_K = 4
D_FF = D_MODEL
SWIGLU_LIMIT = 7.0
SWIGLU_ALPHA = 1.702
LAM_INIT = 0.8 - 0.6 * math.exp(-0.3 * 0)

LEAD = (-N_META) % SSD_CHUNK
P_ROWS = LEAD + N_META + SEQ
S_ROWS = DEC_BATCH * DEC_SEQ
R = P_ROWS + S_ROWS
N_CHUNKS = P_ROWS // SSD_CHUNK

LANES = 128
SUBLANES = 8
VMEM_LIMIT = 56 * 1024 * 1024

PCOL_Z, PCOL_GA, PCOL_GB, PCOL_XBC, PCOL_Q, PCOL_K, PCOL_V = 0, 2048, 4096, 6144, 9216, 11264, 12288
P_COLS = 13312
PROJ_TN = 1024
ROPE_BLOCKS = (PCOL_Q // PROJ_TN, PCOL_V // PROJ_TN)
SRC_Z, SRC_XBC, SRC_DT = 0, SSD_INNER, SSD_INNER + CONV_DIM
SRC_Q = SRC_DT + SSD_HEADS
SRC_GA = SRC_Q + 2 * D_MODEL
PROJ_SRC_ROW = tuple(base + PROJ_TN * b for base, n in
                     ((SRC_Z, 2), (SRC_GA, 4), (SRC_XBC, 3), (SRC_Q, 4)) for b in range(n))

MOE_TM = 256
MOE_NB = -(-(R * TOP_K) // MOE_TM) + N_EXPERTS
MOE_DUMP_BLOCKS = -(-(LEAD * TOP_K) // MOE_TM)
MOE_ROWS = (MOE_NB + MOE_DUMP_BLOCKS) * MOE_TM
MOE_TN_GATE_UP = 1024
NEG = -1e30


def _cparams(sem, vmem=VMEM_LIMIT):
    return pltpu.CompilerParams(dimension_semantics=sem, vmem_limit_bytes=vmem)


_NT = (((1,), (1,)), ((), ()))


def _sigmoid(x):
    return 1.0 / (1.0 + jnp.exp(-x))


def _rmsnorm_kernel(x_ref, w_ref, o_ref):
    x = x_ref[...]
    ms = jnp.mean(x * x, axis=-1, keepdims=True)
    o_ref[...] = (x * lax.rsqrt(ms + NORM_EPS) * w_ref[...]).astype(o_ref.dtype)


def rmsnorm_rows(x, w, tm=768):
    rows, d = x.shape
    return pl.pallas_call(
        _rmsnorm_kernel,
        grid=(rows // tm,),
        in_specs=[pl.BlockSpec((tm, d), lambda i: (i, 0)), pl.BlockSpec((1, d), lambda i: (0, 0))],
        out_specs=pl.BlockSpec((tm, d), lambda i: (i, 0)),
        out_shape=jax.ShapeDtypeStruct((rows, d), BF16),
        compiler_params=_cparams(("parallel",)),
        name="rmsnorm_rows",
    )(x, w.reshape(1, d))


def _rope(blk, c, sa, sb):
    return blk * c + pltpu.roll(blk, ROT_DIM // 2, axis=1) * sa + pltpu.roll(blk, LANES - ROT_DIM // 2, axis=1) * sb


def _proj_kernel(x_ref, w_ref, c_ref, sa_ref, sb_ref, o_ref, kb_ref, vt_ref):
    j = pl.program_id(1)
    acc = lax.dot_general(x_ref[...], w_ref[...], _NT, preferred_element_type=F32)
    is_rope = (j >= ROPE_BLOCKS[0]) & (j < ROPE_BLOCKS[1])

    @pl.when(is_rope)
    def _():
        c, sa, sb = c_ref[...], sa_ref[...], sb_ref[...]
        for g in range(acc.shape[1] // LANES):
            sl = slice(g * LANES, (g + 1) * LANES)
            o_ref[:, sl] = _rope(acc[:, sl], c, sa, sb)

    @pl.when(jnp.logical_not(is_rope))
    def _():
        o_ref[...] = acc

    @pl.when(j == PCOL_K // PROJ_TN)
    def _():
        kb_ref[...] = o_ref[...].astype(kb_ref.dtype)

    @pl.when(j == PCOL_V // PROJ_TN)
    def _():
        vt_ref[...] = acc.T.astype(vt_ref.dtype)


def in_projection(xn, wt_bf, cos_t, sin_a, sin_b, tm=1408):
    rows, d = xn.shape
    tn = PROJ_TN
    tab = pl.BlockSpec((tm, LANES), lambda i, j: (i, 0))

    def src_row(j):
        unit = math.gcd(*PROJ_SRC_ROW[1:])
        off = jnp.int32(0)
        for jj, r0 in enumerate(PROJ_SRC_ROW):
            off = jnp.where(j == jj, r0 // unit, off)
        return off * unit

    return pl.pallas_call(
        _proj_kernel,
        grid=(rows // tm, P_COLS // tn),
        in_specs=[pl.BlockSpec((tm, d), lambda i, j: (i, 0)),
                  pl.BlockSpec((pl.Element(tn), pl.Element(d)), lambda i, j: (src_row(j), 0)), tab, tab, tab],
        out_specs=[pl.BlockSpec((tm, tn), lambda i, j: (i, j)),
                   pl.BlockSpec((tm, tn), lambda i, j: (i, 0)),
                   pl.BlockSpec((tn, tm), lambda i, j: (0, i))],
        out_shape=[jax.ShapeDtypeStruct((rows, P_COLS), F32),
                   jax.ShapeDtypeStruct((rows, tn), BF16),
                   jax.ShapeDtypeStruct((tn, rows), BF16)],
        compiler_params=_cparams(("parallel", "arbitrary")),
        name="in_projection",
    )(xn, wt_bf, cos_t, sin_a, sin_b)


def _mm_kernel(x_ref, w_ref, o_ref):
    o_ref[...] = lax.dot_general(x_ref[...], w_ref[...], _NT, preferred_element_type=F32).astype(o_ref.dtype)


def dt_projection(xn, wt_bf, tm=1056):
    rows, d = xn.shape
    return pl.pallas_call(
        _mm_kernel,
        grid=(rows // tm,),
        in_specs=[pl.BlockSpec((tm, d), lambda i: (i, 0)), pl.BlockSpec((LANES, d), lambda i: (SRC_DT // LANES, 0))],
        out_specs=pl.BlockSpec((tm, LANES), lambda i: (i, 0)),
        out_shape=jax.ShapeDtypeStruct((rows, LANES), F32),
        compiler_params=_cparams(("parallel",)),
        name="dt_projection",
    )(xn, wt_bf)


def _ssd_chunk(xprev, xcur, dtraw, z, valid, s_ref, cw_ref, cb_ref, dtb_ref, alog_ref, dsk_ref, nw_ref, e_ref,
               state_dot_precision):
    L = xcur.shape[0]
    ext = jnp.concatenate([xprev, xcur], axis=0)
    off = SUBLANES - (SSD_CONV - 1)
    acc = cb_ref[...]
    for k in range(SSD_CONV):
        acc = acc + ext[off + k:off + k + L] * cw_ref[k:k + 1, :]
    act = acc * _sigmoid(acc)
    xs = act[:, :SSD_INNER]
    nb = SSD_GROUPS * SSD_STATE
    bm = act[:, SSD_INNER:SSD_INNER + nb].astype(BF16)
    cm = act[:, SSD_INNER + nb:].astype(BF16)

    dpre = dtraw + dtb_ref[...]
    dt = jnp.maximum(dpre, 0.0) + jnp.log1p(jnp.exp(-jnp.abs(dpre)))
    dt = jnp.where(valid, dt, 0.0)
    da = dt * (-jnp.exp(alog_ref[...]))

    li = lax.broadcasted_iota(I32, (L, L), 0)
    si = lax.broadcasted_iota(I32, (L, L), 1)
    tril = li >= si
    eye = li == si
    a_cs = jnp.dot(tril.astype(F32), da, precision=HI, preferred_element_type=F32)
    a_tot = a_cs[L - 1:L, :]
    e_mat = e_ref[...]

    def expand(v):
        out = None
        for _ in range(3):
            piece = v.astype(BF16)
            term = jnp.dot(piece, e_mat, preferred_element_type=F32)
            out = term if out is None else out + term
            v = v - piece.astype(F32)
        return out

    dt_x = expand(dt)
    eacs_x = expand(jnp.exp(a_cs))
    dte_x = expand(jnp.exp(a_tot - a_cs))
    etot = jnp.exp(a_tot)

    xdt = xs * dt_x
    xdt_b = xdt.astype(BF16)
    xde = xdt * dte_x
    gw = SSD_REP * SSD_HEAD_DIM
    lane = lax.broadcasted_iota(I32, (L, LANES), 1)
    nt = (((1,), (1,)), ((), ()))
    tn = (((0,), (0,)), ((), ()))
    y_groups = []
    for g in range(SSD_GROUPS):
        bg = bm[:, g * SSD_STATE:(g + 1) * SSD_STATE]
        cg = cm[:, g * SSD_STATE:(g + 1) * SSD_STATE]
        cb = lax.dot_general(cg, bg, nt, preferred_element_type=F32)
        s_g = s_ref[g * gw:(g + 1) * gw, :]
        y_off = lax.dot_general(cg, s_g.astype(BF16), nt, preferred_element_type=F32)
        if state_dot_precision is None:
            upd = lax.dot_general(xde[:, g * gw:(g + 1) * gw].astype(BF16), bg, tn, preferred_element_type=F32)
        else:
            upd = lax.dot_general(xde[:, g * gw:(g + 1) * gw], act[:, SSD_INNER + g * SSD_STATE:SSD_INNER + (g + 1) * SSD_STATE],
                                  tn, precision=state_dot_precision, preferred_element_type=F32)
        pairs = []
        decs = []
        for q in range(SSD_REP // 2):
            yd = []
            for r in range(2):
                h = g * SSD_REP + 2 * q + r
                colb = jnp.broadcast_to(a_cs[:, h:h + 1], (L, L))
                rowb = jnp.sum(jnp.where(eye, colb, 0.0), axis=0, keepdims=True)
                lm = jnp.where(tril, jnp.exp(jnp.minimum(colb - rowb, 0.0)), 0.0)
                m = (cb * lm).astype(BF16)
                c0 = (g * SSD_REP + 2 * q) * SSD_HEAD_DIM
                yd.append(jnp.dot(m, xdt_b[:, c0:c0 + LANES], preferred_element_type=F32))
                decs.append(jnp.broadcast_to(etot[:, h:h + 1], (SSD_HEAD_DIM, SSD_STATE)))
            pairs.append(jnp.where(lane < SSD_HEAD_DIM, yd[0], yd[1]))
        y_g = jnp.concatenate(pairs, axis=1) + y_off * eacs_x[:, g * gw:(g + 1) * gw]
        y_groups.append(y_g)
        s_ref[g * gw:(g + 1) * gw, :] = jnp.concatenate(decs, axis=0) * s_g + upd
    y = jnp.concatenate(y_groups, axis=1) + dsk_ref[...] * xs
    yz = y * (z * _sigmoid(z))
    outs = []
    for g in range(SSD_GROUPS):
        yg = yz[:, g * gw:(g + 1) * gw]
        ms = jnp.mean(yg * yg, axis=-1, keepdims=True)
        outs.append(yg * lax.rsqrt(ms + NORM_EPS) * nw_ref[:, g * gw:(g + 1) * gw])
    return jnp.concatenate(outs, axis=1)


def _ssd_prompt_kernel(xprev_ref, xcur_ref, dt_ref, z_ref, cw_ref, cb_ref, dtb_ref, alog_ref, dsk_ref, nw_ref, e_ref,
                       y_ref, s_ref):
    c = pl.program_id(0)

    @pl.when(c == 0)
    def _():
        s_ref[...] = jnp.zeros_like(s_ref)

    xprev = jnp.where(c == 0, 0.0, xprev_ref[...])
    row = c * SSD_CHUNK + lax.broadcasted_iota(I32, (SSD_CHUNK, 1), 0)
    y = _ssd_chunk(xprev, xcur_ref[...], dt_ref[...], z_ref[...], row >= LEAD, s_ref,
                   cw_ref, cb_ref, dtb_ref, alog_ref, dsk_ref, nw_ref, e_ref, None)
    y_ref[...] = y.astype(y_ref.dtype)


def _full(shape):
    return pl.BlockSpec(shape, lambda *_: (0,) * len(shape))


def ssd_prompt(proj, dtraw, params):
    L = SSD_CHUNK
    xb = PCOL_XBC // CONV_DIM
    in_specs = [
        pl.BlockSpec((SUBLANES, CONV_DIM), lambda c: (jnp.maximum(c * (L // SUBLANES) - 1, 0), xb)),
        pl.BlockSpec((L, CONV_DIM), lambda c: (c, xb)),
        pl.BlockSpec((L, LANES), lambda c: (c, 0)),
        pl.BlockSpec((L, SSD_INNER), lambda c: (c, PCOL_Z // SSD_INNER)),
    ] + [_full(p.shape) for p in params]
    return pl.pallas_call(
        _ssd_prompt_kernel,
        grid=(N_CHUNKS,),
        in_specs=in_specs,
        out_specs=[pl.BlockSpec((L, SSD_INNER), lambda c: (c, 0)), _full((SSD_INNER, SSD_STATE))],
        out_shape=[jax.ShapeDtypeStruct((P_ROWS, SSD_INNER), BF16), jax.ShapeDtypeStruct((SSD_INNER, SSD_STATE), F32)],
        compiler_params=_cparams(("arbitrary",)),
        name="ssd_prompt",
    )(proj, proj, dtraw, proj, *params)


SSD_S_NB = 4


def _ssd_sample_kernel(xprev_ref, xcur_ref, dt_ref, z_ref, sin_ref, cw_ref, cb_ref, dtb_ref, alog_ref, dsk_ref, nw_ref,
                       e_ref, y_ref, sout_ref):
    L = SUBLANES
    valid = lax.broadcasted_iota(I32, (L, 1), 0) < DEC_SEQ
    ys = []
    for b in range(SSD_S_NB):
        rows = slice(b * DEC_SEQ, (b + 1) * DEC_SEQ)
        pad = lambda u: jnp.concatenate([u, jnp.zeros((L - DEC_SEQ, u.shape[1]), u.dtype)], axis=0)
        sout_ref[b] = sin_ref[b]
        y = _ssd_chunk(xprev_ref[b], pad(xcur_ref[rows, :]), pad(dt_ref[rows, :]), pad(z_ref[rows, :]), valid,
                       sout_ref.at[b], cw_ref, cb_ref, dtb_ref, alog_ref, dsk_ref, nw_ref, e_ref, HI)
        ys.append(y[:DEC_SEQ])
    y_ref[...] = jnp.concatenate(ys, axis=0).astype(y_ref.dtype)


def ssd_sample(proj, dtraw, conv_prev, state, params):
    nb = SSD_S_NB
    rb = nb * DEC_SEQ
    r0 = P_ROWS // rb
    in_specs = [
        pl.BlockSpec((nb, SUBLANES, CONV_DIM), lambda i: (i, 0, 0)),
        pl.BlockSpec((rb, CONV_DIM), lambda i: (r0 + i, PCOL_XBC // CONV_DIM)),
        pl.BlockSpec((rb, LANES), lambda i: (r0 + i, 0)),
        pl.BlockSpec((rb, SSD_INNER), lambda i: (r0 + i, PCOL_Z // SSD_INNER)),
        pl.BlockSpec((nb, SSD_INNER, SSD_STATE), lambda i: (i, 0, 0)),
    ] + [_full(p.shape) for p in params]
    return pl.pallas_call(
        _ssd_sample_kernel,
        grid=(DEC_BATCH // nb,),
        in_specs=in_specs,
        out_specs=[pl.BlockSpec((rb, SSD_INNER), lambda i: (i, 0)),
                   pl.BlockSpec((nb, SSD_INNER, SSD_STATE), lambda i: (i, 0, 0))],
        out_shape=[jax.ShapeDtypeStruct((S_ROWS, SSD_INNER), BF16),
                   jax.ShapeDtypeStruct((DEC_BATCH, SSD_INNER, SSD_STATE), F32)],
        compiler_params=_cparams(("parallel",)),
        name="ssd_sample",
    )(conv_prev, proj, dtraw, proj, state, *params)


def _lambda(lq1_ref, lk1_ref, lq2_ref, lk2_ref):
    s1 = jnp.sum(lq1_ref[...] * lk1_ref[...], axis=-1, keepdims=True)
    s2 = jnp.sum(lq2_ref[...] * lk2_ref[...], axis=-1, keepdims=True)
    return jnp.exp(s1) - jnp.exp(s2) + LAM_INIT


def _subnorm(o, subw_ref):
    ms = jnp.mean(o * o, axis=-1, keepdims=True)
    return (o * lax.rsqrt(ms + NORM_EPS) * subw_ref[...]) * (1.0 - LAM_INIT)


ATT_TQ = 256
ATT_TK = 768
BF16_SUBLANES = 16
ATT_VT_ROWS = ATT_V_DIM + BF16_SUBLANES
EXP2_SCALE = ATT_SCALE * math.log2(math.e)


def _attn_prompt_kernel(q_ref, k_ref, vt_ref, d_ref, lq1_ref, lk1_ref, lq2_ref, lk2_ref, subw_ref, o_ref, m_s, acc_s):
    i = pl.program_id(1)
    tq, tk, hd = ATT_TQ, ATT_TK, ATT_HEAD_DIM
    q = q_ref[...]
    qc = [jnp.concatenate([q[:, r * 2 * hd + c * hd: r * 2 * hd + (c + 1) * hd] for r in range(ATT_REP)], axis=0).astype(BF16)
          for c in range(2)]
    m_s[...] = jnp.full_like(m_s, NEG)
    acc_s[...] = jnp.zeros_like(acc_s)
    nt = (((1,), (1,)), ((), ()))
    n_blocks = (i * tq + tq - 1) // tk + 1

    def step(j, mask, tkc=tk):
        k0 = pl.multiple_of(j * tk, tk)
        kblk = k_ref[pl.ds(k0, tkc), :]
        vt = vt_ref[:, pl.ds(k0, tkc)]
        sts = [lax.dot_general(kblk[:, c * hd:(c + 1) * hd], qc[c], nt, preferred_element_type=F32) for c in range(2)]
        ps, alphas = [], []
        for c in range(2):
            st = sts[c]
            if "causal" in mask:
                st = jnp.where(d_ref[:tkc, :] <= i * tq - j * tk, st, NEG)
            if "lead" in mask:
                st = st + jnp.where(lax.broadcasted_iota(I32, (tkc, 1), 0) < LEAD, NEG, 0.0)
            m_old = m_s[c]
            m_new = jnp.maximum(m_old, jnp.max(st, axis=0, keepdims=True))
            alphas.append(jnp.exp2((m_old - m_new) * EXP2_SCALE))
            ps.append(jnp.exp2((st - m_new) * EXP2_SCALE).astype(BF16))
            m_s[c] = m_new
        for c in range(2):
            acc_s[c] = alphas[c] * acc_s[c] + jnp.dot(vt, ps[c], preferred_element_type=F32)

    def diagonal_step(j, mask):
        live = (i * tq + tq - 1 - j * tk) // tq + 1
        for n in range(1, tk // tq + 1):
            @pl.when(live == n)
            def _():
                step(j, mask, n * tq)

    @pl.when(n_blocks == 1)
    def _():
        diagonal_step(0, ("causal", "lead"))

    @pl.when(n_blocks > 1)
    def _():
        step(0, ("lead",))

        def body(j, carry):
            step(j, ())
            return carry

        lax.fori_loop(1, n_blocks - 1, body, 0)
        diagonal_step(n_blocks - 1, ("causal",))

    lam = _lambda(lq1_ref, lk1_ref, lq2_ref, lk2_ref)
    for r in range(ATT_REP):
        cols = slice(r * tq, (r + 1) * tq)
        o0 = acc_s[0, :ATT_V_DIM, cols] / acc_s[0, ATT_V_DIM:ATT_V_DIM + 1, cols]
        o1 = acc_s[1, :ATT_V_DIM, cols] / acc_s[1, ATT_V_DIM:ATT_V_DIM + 1, cols]
        ot = o0 - lam * o1
        ms = jnp.mean(ot * ot, axis=0, keepdims=True)
        ot = (ot * lax.rsqrt(ms + NORM_EPS) * subw_ref[...]) * (1.0 - LAM_INIT)
        o_ref[:, r * ATT_V_DIM:(r + 1) * ATT_V_DIM] = ot.T.astype(o_ref.dtype)


def attn_prompt(proj, kb, vt_ext, lam_params, subw_col):
    tq = ATT_TQ
    gq = ATT_REP * 2 * ATT_HEAD_DIM
    vec = _full((1, ATT_HEAD_DIM))
    key_minus_query = (np.arange(ATT_TK)[:, None] - np.arange(ATT_REP * tq)[None, :] % tq).astype(np.int32)
    return pl.pallas_call(
        _attn_prompt_kernel,
        grid=(ATT_KV_HEADS, R // tq),
        in_specs=[pl.BlockSpec((tq, gq), lambda g, i: (i, PCOL_Q // gq + g)),
                  pl.BlockSpec((R, ATT_V_DIM), lambda g, i: (0, g)),
                  pl.BlockSpec((ATT_VT_ROWS, R), lambda g, i: (g, 0)),
                  _full((ATT_TK, ATT_REP * tq)),
                  vec, vec, vec, vec, _full((ATT_V_DIM, 1))],
        out_specs=pl.BlockSpec((tq, gq), lambda g, i: (i, g)),
        out_shape=jax.ShapeDtypeStruct((R, ATT_HEADS * ATT_V_DIM), BF16),
        scratch_shapes=[pltpu.VMEM((2, 1, ATT_REP * tq), F32),
                        pltpu.VMEM((2, ATT_VT_ROWS, ATT_REP * tq), F32)],
        compiler_params=_cparams(("parallel", "arbitrary")),
        name="attn_prompt",
    )(proj, kb, vt_ext, jnp.asarray(key_minus_query), *lam_params, subw_col)


N_PAGES = SEQ // PAGE_SIZE
ATT_S_ROWS = ATT_KV_HEADS * 2 * DEC_SEQ * ATT_REP
ATT_S_GROUP = 2 * DEC_SEQ * ATT_REP
ATT_S_PPS = 16
ATT_S_STEPS = N_PAGES // ATT_S_PPS
KROWS = PAGE_SIZE * ATT_KV_HEADS * 2
V_TILE = ATT_V_DIM
V_LANE_TILES = ATT_V_DIM // V_TILE


def _attn_sample_kernel(pt_ref, q_ref, *refs):
    npg = ATT_S_PPS
    kc_refs = refs[:npg]
    vc_refs = [refs[npg + V_LANE_TILES * p: npg + V_LANE_TILES * (p + 1)] for p in range(npg)]
    base = npg * (1 + V_LANE_TILES)
    kn_ref = refs[base]
    vn_refs = refs[base + 1: base + 1 + V_LANE_TILES]
    x_ref, lq1_ref, lk1_ref, lq2_ref, lk2_ref, subw_ref, o_ref, m_s, l_s, acc_s = refs[base + 1 + V_LANE_TILES:]
    j = pl.program_id(1)
    nt = (((1,), (1,)), ((), ()))
    nhc = ATT_KV_HEADS * 2
    rows_hc = DEC_SEQ * ATT_REP

    @pl.when(j == 0)
    def _():
        m_s[...] = jnp.full_like(m_s, NEG)
        l_s[...] = jnp.zeros_like(l_s)
        acc_s[...] = jnp.zeros_like(acc_s)

    def step(k_pages, v_pages, masked):
        q = q_ref[0]
        cols = []
        for k_src in k_pages:
            ss = []
            for hc in range(nhc):
                khc = k_src[pl.ds(hc, PAGE_SIZE, stride=nhc), :].astype(BF16)
                ss.append(lax.dot_general(q[hc * rows_hc:(hc + 1) * rows_hc, :], khc, nt, preferred_element_type=F32))
            cols.append(jnp.concatenate(ss, axis=0))
        s = jnp.concatenate(cols, axis=1) if len(cols) > 1 else cols[0]
        if masked:
            t_q = (lax.broadcasted_iota(I32, s.shape, 0) % rows_hc) // ATT_REP
            t_k = lax.broadcasted_iota(I32, s.shape, 1)
            s = jnp.where(t_k <= t_q, s, NEG)
        m_old = m_s[...]
        m_new = jnp.maximum(m_old, jnp.max(s, axis=1, keepdims=True))
        alpha = jnp.exp2((m_old - m_new) * EXP2_SCALE)
        p = jnp.exp2((s - m_new[:, :1]) * EXP2_SCALE)
        l_s[...] = alpha * l_s[...] + jnp.sum(p, axis=1, keepdims=True)
        pb = p.astype(BF16)
        head_of_col = lax.broadcasted_iota(I32, (ATT_S_ROWS, PAGE_SIZE * ATT_KV_HEADS), 1) % ATT_KV_HEADS
        head_of_row = lax.broadcasted_iota(I32, (ATT_S_ROWS, PAGE_SIZE * ATT_KV_HEADS), 0) // ATT_S_GROUP
        own_head = head_of_col == head_of_row
        pv = None
        for pi, v_tiles in enumerate(v_pages):
            v2 = jnp.concatenate([v[...].reshape(PAGE_SIZE * ATT_KV_HEADS, V_TILE) for v in v_tiles], axis=1).astype(BF16)
            spread = jnp.dot(pb[:, pi * PAGE_SIZE:(pi + 1) * PAGE_SIZE], x_ref[...], preferred_element_type=F32)
            pe = jnp.where(own_head, spread, 0.0).astype(BF16)
            d = jnp.dot(pe, v2, preferred_element_type=F32)
            pv = d if pv is None else pv + d
        acc_s[...] = alpha[:, :1] * acc_s[...] + pv
        m_s[...] = m_new

    @pl.when(j < ATT_S_STEPS)
    def _():
        step(kc_refs, vc_refs, False)

    @pl.when(j == ATT_S_STEPS)
    def _():
        step([kn_ref], [vn_refs], True)
        lam = _lambda(lq1_ref, lk1_ref, lq2_ref, lk2_ref)
        for g in range(ATT_KV_HEADS):
            r0 = g * ATT_S_GROUP
            o0 = acc_s[r0:r0 + rows_hc, :] / l_s[r0:r0 + rows_hc, :1]
            o1 = acc_s[r0 + rows_hc:r0 + 2 * rows_hc, :] / l_s[r0 + rows_hc:r0 + 2 * rows_hc, :1]
            o_ref[0, g] = _subnorm(o0 - lam * o1, subw_ref)


def attn_sample(page_table, q, kc, vc, knew, vnew, lam_params, subw):
    vec = _full((1, ATT_HEAD_DIM))
    kblk = (None, KROWS, ATT_HEAD_DIM)
    vblk = (None, None, PAGE_SIZE, ATT_KV_HEADS, V_TILE)

    def page(b, j, pt, p):
        return pt[b * N_PAGES + jnp.minimum(j, ATT_S_STEPS - 1) * ATT_S_PPS + p]

    k_specs = [pl.BlockSpec(kblk, functools.partial(lambda b, j, pt, p: (page(b, j, pt, p), 0, 0), p=p))
               for p in range(ATT_S_PPS)]
    v_specs = [pl.BlockSpec(vblk, functools.partial(lambda b, j, pt, p, e: (0, page(b, j, pt, p), 0, 0, e), p=p, e=e))
               for p in range(ATT_S_PPS) for e in range(V_LANE_TILES)]
    spread = np.zeros((PAGE_SIZE, PAGE_SIZE * ATT_KV_HEADS), np.float32)
    for t in range(PAGE_SIZE):
        spread[t, t * ATT_KV_HEADS:(t + 1) * ATT_KV_HEADS] = 1.0
    kn_spec = pl.BlockSpec(kblk, lambda b, j, pt: (b, 0, 0))
    vn_specs = [pl.BlockSpec(vblk, functools.partial(lambda b, j, pt, e: (0, b, 0, 0, e), e=e)) for e in range(V_LANE_TILES)]
    grid_spec = pltpu.PrefetchScalarGridSpec(
        num_scalar_prefetch=1,
        grid=(DEC_BATCH, ATT_S_STEPS + 1),
        in_specs=[pl.BlockSpec((1, ATT_S_ROWS, ATT_HEAD_DIM), lambda b, j, pt: (b, 0, 0)),
                  *k_specs, *v_specs, kn_spec, *vn_specs, _full((PAGE_SIZE, PAGE_SIZE * ATT_KV_HEADS)),
                  vec, vec, vec, vec, _full((1, ATT_V_DIM))],
        out_specs=pl.BlockSpec((1, ATT_KV_HEADS, DEC_SEQ * ATT_REP, ATT_V_DIM), lambda b, j, pt: (b, 0, 0, 0)),
        scratch_shapes=[pltpu.VMEM((ATT_S_ROWS, LANES), F32), pltpu.VMEM((ATT_S_ROWS, LANES), F32),
                        pltpu.VMEM((ATT_S_ROWS, ATT_V_DIM), F32)],
    )
    return pl.pallas_call(
        _attn_sample_kernel,
        grid_spec=grid_spec,
        out_shape=jax.ShapeDtypeStruct((DEC_BATCH, ATT_KV_HEADS, DEC_SEQ * ATT_REP, ATT_V_DIM), F32),
        compiler_params=_cparams(("parallel", "arbitrary")),
        name="attn_sample",
    )(page_table, q, *([kc] * ATT_S_PPS), *([vc] * (ATT_S_PPS * V_LANE_TILES)), knew, *([vnew] * V_LANE_TILES),
      jnp.asarray(spread, BF16), *lam_params, subw)


def _mix_kernel(ya_ref, yb_ref, ga_ref, gb_ref, wa_ref, wb_ref, o_ref):
    oa = jnp.dot(ya_ref[...], wa_ref[...], preferred_element_type=F32)
    ob = jnp.dot(yb_ref[...], wb_ref[...], preferred_element_type=F32)
    o_ref[...] = (_sigmoid(ga_ref[...]) * oa + _sigmoid(gb_ref[...]) * ob).astype(o_ref.dtype)


def mix_branches(yzn, on, proj, wa, wb, tm=256):
    d = D_MODEL
    row = lambda i: (i, 0)
    return pl.pallas_call(
        _mix_kernel,
        grid=(R // tm,),
        in_specs=[pl.BlockSpec((tm, d), row), pl.BlockSpec((tm, d), row),
                  pl.BlockSpec((tm, d), lambda i: (i, PCOL_GA // d)), pl.BlockSpec((tm, d), lambda i: (i, PCOL_GB // d)),
                  _full((d, d)), _full((d, d))],
        out_specs=pl.BlockSpec((tm, d), row),
        out_shape=jax.ShapeDtypeStruct((R, d), BF16),
        compiler_params=_cparams(("parallel",)),
        name="mix_branches",
    )(yzn, on, proj, proj, wa, wb)


def _post_kernel(mix_ref, h_ref, wo_ref, nw_ref, rw_ref, rb_ref, h2_ref, n_ref, ti_ref, tg_ref):
    h2 = h_ref[...] + jnp.dot(mix_ref[...], wo_ref[...], preferred_element_type=F32)
    h2_ref[...] = h2
    ms = jnp.mean(h2 * h2, axis=-1, keepdims=True)
    n = h2 * lax.rsqrt(ms + NORM_EPS) * nw_ref[...]
    n_ref[...] = n
    logits = jnp.dot(n, rw_ref[...], precision=HI, preferred_element_type=F32) + rb_ref[...]
    lane = lax.broadcasted_iota(I32, logits.shape, 1)
    work = logits
    vals, idxs = [], []
    for _ in range(TOP_K):
        mx = jnp.max(work, axis=1, keepdims=True)
        am = jnp.min(jnp.where(work == mx, lane, LANES), axis=1, keepdims=True)
        vals.append(mx)
        idxs.append(am)
        work = jnp.where(lane == am, -jnp.inf, work)
    es = [jnp.exp(v - vals[0]) for v in vals]
    den = es[0] + es[1] + es[2] + es[3]
    ti = jnp.zeros(logits.shape, I32)
    tg = jnp.zeros(logits.shape, F32)
    for k in range(TOP_K):
        ti = jnp.where(lane == k, idxs[k], ti)
        tg = jnp.where(lane == k, es[k] / den, tg)
    ti_ref[...] = ti
    tg_ref[...] = tg


def post_mixer(mix, h, wo, nw, rw, rb, tm=256):
    d = D_MODEL
    row = lambda i: (i, 0)
    return pl.pallas_call(
        _post_kernel,
        grid=(R // tm,),
        in_specs=[pl.BlockSpec((tm, d), row), pl.BlockSpec((tm, d), row), _full((d, d)), _full((1, d)),
                  _full((d, LANES)), _full((1, LANES))],
        out_specs=[pl.BlockSpec((tm, d), row), pl.BlockSpec((tm, d), row),
                   pl.BlockSpec((tm, LANES), row), pl.BlockSpec((tm, LANES), row)],
        out_shape=[jax.ShapeDtypeStruct((R, d), F32), jax.ShapeDtypeStruct((R, d), F32),
                   jax.ShapeDtypeStruct((R, LANES), I32), jax.ShapeDtypeStruct((R, LANES), F32)],
        compiler_params=_cparams(("parallel",)),
        name="post_mixer",
    )(mix, h, wo, nw, rw, rb)


def _rank_kernel(ti_ref, rank_ref, cnt_ref, carry_s):
    i = pl.program_id(0)
    tm = ti_ref.shape[0]

    @pl.when(i == 0)
    def _():
        carry_s[...] = jnp.zeros_like(carry_s)

    ti = ti_ref[...]
    lane = lax.broadcasted_iota(I32, ti.shape, 1)
    row = i * tm + lax.broadcasted_iota(I32, (tm, 1), 0)
    valid = row >= LEAD
    ohs = [jnp.where((lane == ti[:, k:k + 1]) & valid, 1.0, 0.0) for k in range(TOP_K)]
    osum = ohs[0] + ohs[1] + ohs[2] + ohs[3]
    li = lax.broadcasted_iota(I32, (tm, tm), 0)
    si = lax.broadcasted_iota(I32, (tm, tm), 1)
    before = jnp.dot((li > si).astype(BF16), osum.astype(BF16), preferred_element_type=F32) + carry_s[...]
    rank = jnp.zeros(ti.shape, I32)
    for k in range(TOP_K):
        rk = jnp.sum(ohs[k] * before, axis=1, keepdims=True)
        rank = jnp.where(lane == k, rk.astype(I32), rank)
    rank_ref[...] = rank
    carry_s[...] = carry_s[...] + jnp.sum(osum, axis=0, keepdims=True)
    cnt_ref[...] = carry_s[...].astype(I32)


def expert_ranks(ti, tm=256):
    return pl.pallas_call(
        _rank_kernel,
        grid=(R // tm,),
        in_specs=[pl.BlockSpec((tm, LANES), lambda i: (i, 0))],
        out_specs=[pl.BlockSpec((tm, LANES), lambda i: (i, 0)), _full((1, LANES))],
        out_shape=[jax.ShapeDtypeStruct((R, LANES), I32), jax.ShapeDtypeStruct((1, LANES), I32)],
        scratch_shapes=[pltpu.VMEM((1, LANES), F32)],
        compiler_params=_cparams(("arbitrary",)),
        name="expert_ranks",
    )(ti)


DISPATCH_TM = 256


def _dispatch_kernel(dest_ref, tail_ref, nu_ref, n_ref, xs_ref, zbuf, sem, zsem):
    i = pl.program_id(0)
    tm = DISPATCH_TM

    @pl.when(i == 0)
    def _():
        zbuf[...] = jnp.zeros_like(zbuf)

        def zero_block(row0):
            return pltpu.make_async_copy(zbuf, xs_ref.at[pl.ds(pl.multiple_of(row0, MOE_TM), MOE_TM), :], zsem)

        def each_unused_block(fn):
            def body(b, carry):
                fn(zero_block(b * MOE_TM))
                return carry

            lax.fori_loop(nu_ref[0], MOE_ROWS // MOE_TM, body, 0)

        for e in range(N_EXPERTS):
            @pl.when(tail_ref[e] >= 0)
            def _():
                zero_block(tail_ref[e]).start()

        each_unused_block(lambda c: c.start())
        for e in range(N_EXPERTS):
            @pl.when(tail_ref[e] >= 0)
            def _():
                zero_block(tail_ref[e]).wait()

        each_unused_block(lambda c: c.wait())

    def body(r, carry):
        for k in range(TOP_K):
            d = dest_ref[(i * tm + r) * TOP_K + k]
            pltpu.make_async_copy(n_ref.at[pl.ds(r, 1), :], xs_ref.at[pl.ds(d, 1), :], sem).start()
        return carry

    lax.fori_loop(0, tm, body, 0)
    for k in range(TOP_K):
        pltpu.make_async_copy(n_ref, xs_ref.at[pl.ds(0, tm), :], sem).wait()


def moe_dispatch(dest_flat, tail_start, n_used, n):
    tm = DISPATCH_TM
    d = n.shape[1]
    grid_spec = pltpu.PrefetchScalarGridSpec(
        num_scalar_prefetch=3,
        grid=(R // tm,),
        in_specs=[pl.BlockSpec((tm, d), lambda i, ds, ts, nu: (i, 0))],
        out_specs=pl.BlockSpec(memory_space=pl.ANY),
        scratch_shapes=[pltpu.VMEM((MOE_TM, d), n.dtype), pltpu.SemaphoreType.DMA, pltpu.SemaphoreType.DMA],
    )
    return pl.pallas_call(
        _dispatch_kernel,
        grid_spec=grid_spec,
        out_shape=jax.ShapeDtypeStruct((MOE_ROWS, d), n.dtype),
        compiler_params=_cparams(("arbitrary",)),
        name="moe_dispatch",
    )(dest_flat, tail_start, n_used, n)


def _new_expert(be_ref, i):
    return (i == 0) | (be_ref[i] != be_ref[jnp.maximum(i - 1, 0)])


def _stream_expert_weights(i, be_ref, run_ref, nxt_ref, copies, wbuf, wbf):
    slot = run_ref[i] % 2

    @pl.when(i == 0)
    def _():
        for c in copies(be_ref[0], 0):
            c.start()

    @pl.when(_new_expert(be_ref, i))
    def _():
        for c in copies(be_ref[i], slot):
            c.wait()

        @pl.when(nxt_ref[i] >= 0)
        def _():
            for c in copies(nxt_ref[i], 1 - slot):
                c.start()

        for t in range(wbf.shape[0]):
            wbf[t] = wbuf[slot, t].astype(BF16)


def _gmm1_kernel(be_ref, nu_ref, run_ref, nxt_ref, x_ref, w_hbm, bg_ref, bu_ref, h_ref, wbuf, wbf, sem):
    j = pl.program_id(0)
    i = pl.program_id(1)
    tn = MOE_TN_GATE_UP

    def copies(e, slot):
        cols = (pl.multiple_of(j * tn, tn), pl.multiple_of(D_FF + j * tn, tn))
        return [pltpu.make_async_copy(w_hbm.at[e, :, pl.ds(c0, tn)], wbuf.at[slot, t], sem.at[slot, t])
                for t, c0 in enumerate(cols)]

    @pl.when(i < nu_ref[0])
    def _():
        _stream_expert_weights(i, be_ref, run_ref, nxt_ref, copies, wbuf, wbf)
        x = x_ref[...].astype(BF16)
        hg = jnp.dot(x, wbf[0], preferred_element_type=F32) + bg_ref[0]
        hu = jnp.dot(x, wbf[1], preferred_element_type=F32) + bu_ref[0]
        g = jnp.minimum(hg, SWIGLU_LIMIT)
        u = jnp.clip(hu, -SWIGLU_LIMIT, SWIGLU_LIMIT)
        h_ref[...] = (g * _sigmoid(SWIGLU_ALPHA * g) * (u + 1.0)).astype(h_ref.dtype)

    @pl.when(i >= nu_ref[0])
    def _():
        h_ref[...] = jnp.zeros_like(h_ref)


def moe_gate_up(sched, xs, w_gu, b_gu):
    tm, tn = MOE_TM, MOE_TN_GATE_UP
    nj = D_FF // tn
    blk = lambda i, nu: jnp.minimum(i, nu[0] - 1)
    grid_spec = pltpu.PrefetchScalarGridSpec(
        num_scalar_prefetch=4,
        grid=(nj, MOE_NB),
        in_specs=[pl.BlockSpec((tm, D_MODEL), lambda j, i, be, nu, rn, nx: (blk(i, nu), 0)),
                  pl.BlockSpec(memory_space=pl.ANY),
                  pl.BlockSpec((1, 1, tn), lambda j, i, be, nu, rn, nx: (be[blk(i, nu)], 0, j)),
                  pl.BlockSpec((1, 1, tn), lambda j, i, be, nu, rn, nx: (be[blk(i, nu)], 0, nj + j))],
        out_specs=pl.BlockSpec((tm, tn), lambda j, i, be, nu, rn, nx: (i, j)),
        scratch_shapes=[pltpu.VMEM((2, 2, D_MODEL, tn), F32), pltpu.VMEM((2, D_MODEL, tn), BF16),
                        pltpu.SemaphoreType.DMA((2, 2))],
    )
    return pl.pallas_call(
        _gmm1_kernel,
        grid_spec=grid_spec,
        out_shape=jax.ShapeDtypeStruct((MOE_NB * tm, D_FF), BF16),
        compiler_params=_cparams(("arbitrary", "arbitrary")),
        name="moe_gate_up",
    )(*sched, xs, w_gu, b_gu, b_gu)


def _gmm2_kernel(be_ref, nu_ref, run_ref, nxt_ref, h_ref, w_hbm, bd_ref, y_ref, wbuf, wbf, sem):
    i = pl.program_id(0)

    def copies(e, slot):
        return [pltpu.make_async_copy(w_hbm.at[e], wbuf.at[slot, 0], sem.at[slot])]

    @pl.when(i < nu_ref[0])
    def _():
        _stream_expert_weights(i, be_ref, run_ref, nxt_ref, copies, wbuf, wbf)
        y_ref[...] = jnp.dot(h_ref[...], wbf[0], preferred_element_type=F32) + bd_ref[0]

    @pl.when(i >= nu_ref[0])
    def _():
        y_ref[...] = jnp.zeros_like(y_ref)


def moe_down(sched, hidden, w_dn, b_dn):
    tm = MOE_TM
    blk = lambda i, nu: jnp.minimum(i, nu[0] - 1)
    grid_spec = pltpu.PrefetchScalarGridSpec(
        num_scalar_prefetch=4,
        grid=(MOE_NB,),
        in_specs=[pl.BlockSpec((tm, D_FF), lambda i, be, nu, rn, nx: (blk(i, nu), 0)),
                  pl.BlockSpec(memory_space=pl.ANY),
                  pl.BlockSpec((1, 1, D_MODEL), lambda i, be, nu, rn, nx: (be[blk(i, nu)], 0, 0))],
        out_specs=pl.BlockSpec((tm, D_MODEL), lambda i, be, nu, rn, nx: (i, 0)),
        scratch_shapes=[pltpu.VMEM((2, 1, D_FF, D_MODEL), F32), pltpu.VMEM((1, D_FF, D_MODEL), BF16),
                        pltpu.SemaphoreType.DMA((2,))],
    )
    return pl.pallas_call(
        _gmm2_kernel,
        grid_spec=grid_spec,
        out_shape=jax.ShapeDtypeStruct((MOE_NB * tm, D_MODEL), F32),
        compiler_params=_cparams(("arbitrary",)),
        name="moe_down",
    )(*sched, hidden, w_dn, b_dn)


COMBINE_TM = 128
COMBINE_TILES = R // COMBINE_TM
PROMPT_TILE0 = (LEAD + N_META) // COMBINE_TM
SAMPLE_TILE = P_ROWS // COMBINE_TM


def _combine_kernel(src_ref, h2_ref, tg_ref, fw_ref, yb_ref, yp_ref, ys_ref, buf, sem):
    i = pl.program_id(0)
    tm = COMBINE_TM

    def gather(tile, slot):
        def body(r, carry):
            for k in range(TOP_K):
                s = src_ref[(tile * tm + r) * TOP_K + k]
                pltpu.make_async_copy(yb_ref.at[pl.ds(s, 1), :], buf.at[slot, k, pl.ds(r, 1), :], sem.at[slot]).start()
            return carry

        lax.fori_loop(0, tm, body, 0)

    @pl.when(i == 0)
    def _():
        gather(0, 0)

    @pl.when(i + 1 < COMBINE_TILES)
    def _():
        gather(i + 1, (i + 1) % 2)

    slot = i % 2
    for k in range(TOP_K):
        pltpu.make_async_copy(yb_ref.at[pl.ds(0, tm), :], buf.at[slot, k], sem.at[slot]).wait()
    tg = tg_ref[...]
    f = tg[:, 0:1] * buf[slot, 0]
    for k in range(1, TOP_K):
        f = f + tg[:, k:k + 1] * buf[slot, k]
    out = h2_ref[...] + f
    ms = jnp.mean(out * out, axis=-1, keepdims=True)
    y = out * lax.rsqrt(ms + NORM_EPS) * fw_ref[...]

    @pl.when((i >= PROMPT_TILE0) & (i < SAMPLE_TILE))
    def _():
        yp_ref[...] = y

    @pl.when(i == SAMPLE_TILE)
    def _():
        ys_ref[...] = y


def moe_combine(src_flat, h2, tg, fw, yb):
    tm = COMBINE_TM
    d = D_MODEL
    grid_spec = pltpu.PrefetchScalarGridSpec(
        num_scalar_prefetch=1,
        grid=(COMBINE_TILES,),
        in_specs=[pl.BlockSpec((tm, d), lambda i, s: (i, 0)), pl.BlockSpec((tm, LANES), lambda i, s: (i, 0)),
                  pl.BlockSpec((1, d), lambda i, s: (0, 0)), pl.BlockSpec(memory_space=pl.ANY)],
        out_specs=[pl.BlockSpec((tm, d), lambda i, s: (jnp.clip(i - PROMPT_TILE0, 0, SEQ // tm - 1), 0)),
                   pl.BlockSpec((tm, d), lambda i, s: (0, 0))],
        scratch_shapes=[pltpu.VMEM((2, TOP_K, tm, d), F32), pltpu.SemaphoreType.DMA((2,))],
    )
    return pl.pallas_call(
        _combine_kernel,
        grid_spec=grid_spec,
        out_shape=[jax.ShapeDtypeStruct((SEQ, d), F32), jax.ShapeDtypeStruct((S_ROWS, d), F32)],
        compiler_params=_cparams(("arbitrary",)),
        name="moe_combine",
    )(src_flat, h2, tg, fw, yb)


def _rope_tables():
    half = ROT_DIM // 2
    pos_p = jnp.maximum(jnp.arange(P_ROWS) - LEAD, 0)
    pos_s = SEQ + jnp.arange(S_ROWS) % DEC_SEQ
    pos = jnp.concatenate([pos_p, pos_s])
    inv_freq = 1.0 / (ROPE_THETA ** (jnp.arange(half, dtype=F32) * (2.0 / ROT_DIM)))
    ang = pos.astype(F32)[:, None] * inv_freq[None, :]
    cos, sin = jnp.cos(ang), jnp.sin(ang)
    zeros = jnp.zeros((R, LANES - ROT_DIM), F32)
    cos_t = jnp.concatenate([cos, cos, jnp.ones((R, LANES - ROT_DIM), F32)], axis=1)
    sin_a = jnp.concatenate([jnp.zeros_like(sin), sin, zeros], axis=1)
    sin_b = jnp.concatenate([-sin, jnp.zeros_like(sin), zeros], axis=1)
    return cos_t, sin_a, sin_b


def _pad_lanes(v, fill=0.0):
    v = v.reshape(1, -1)
    return jnp.concatenate([v, jnp.full((1, LANES - v.shape[1]), fill, v.dtype)], axis=1)


def kernel(x_prompt, x_sample, cache_k, cache_v, state_ssm, state_conv, page_table, meta_tokens,
           norm_mix_w, w_in, conv_w, conv_b, dt_bias, a_log, d_skip, ssd_norm_w, w_ssd_out,
           lambda_q1, lambda_k1, lambda_q2, lambda_k2, subln_w, w_attn_out, w_o, norm_ffn_w,
           router_w, router_b, w_gate_up, b_gate_up, w_down, b_down, final_norm_w):
    l = 0
    d = D_MODEL
    x_all = jnp.concatenate([jnp.zeros((LEAD, d), F32), meta_tokens, x_prompt[0], x_sample.reshape(S_ROWS, d)], axis=0)
    xn = rmsnorm_rows(x_all, norm_mix_w[l])
    wt_bf = w_in[l].T.astype(BF16)
    cos_t, sin_a, sin_b = _rope_tables()
    proj, kb, vt = in_projection(xn, wt_bf, cos_t, sin_a, sin_b)
    dtraw = dt_projection(xn, wt_bf)

    e_mat = np.zeros((LANES, SSD_INNER), np.float32)
    for h in range(SSD_HEADS):
        e_mat[h, h * SSD_HEAD_DIM:(h + 1) * SSD_HEAD_DIM] = 1.0
    ssd_params = (conv_w[l], conv_b[l].reshape(1, CONV_DIM), _pad_lanes(dt_bias[l]), _pad_lanes(a_log[l]),
                  jnp.repeat(d_skip[l], SSD_HEAD_DIM).reshape(1, SSD_INNER), ssd_norm_w[l].reshape(1, SSD_INNER),
                  jnp.asarray(e_mat, BF16))
    yzn_p, ssm_p = ssd_prompt(proj, dtraw, ssd_params)
    conv_prev = jnp.concatenate([jnp.zeros((DEC_BATCH, SUBLANES - (SSD_CONV - 1), CONV_DIM), F32), state_conv[l]], axis=1)
    yzn_s, ssm_s = ssd_sample(proj, dtraw, conv_prev, state_ssm[l].reshape(DEC_BATCH, SSD_INNER, SSD_STATE), ssd_params)
    yzn = jnp.concatenate([yzn_p, yzn_s], axis=0)

    k_rows = proj[:, PCOL_K:PCOL_V]
    v_rows = proj[:, PCOL_V:P_COLS]
    lam_params = (lambda_q1[l].reshape(1, -1), lambda_k1[l].reshape(1, -1), lambda_q2[l].reshape(1, -1),
                  lambda_k2[l].reshape(1, -1))
    subw = subln_w[l].reshape(1, ATT_V_DIM)
    vt_ext = jnp.concatenate([vt.reshape(ATT_KV_HEADS, ATT_V_DIM, R),
                              jnp.ones((ATT_KV_HEADS, ATT_VT_ROWS - ATT_V_DIM, R), BF16)], axis=1)
    on_p = attn_prompt(proj, kb, vt_ext.reshape(ATT_KV_HEADS * ATT_VT_ROWS, R), lam_params,
                       subln_w[l].reshape(ATT_V_DIM, 1))

    q_s = proj[P_ROWS:, PCOL_Q:PCOL_K].reshape(DEC_BATCH, DEC_SEQ, ATT_KV_HEADS, ATT_REP, 2, ATT_HEAD_DIM)
    q_s = jnp.transpose(q_s, (0, 2, 4, 1, 3, 5)).reshape(DEC_BATCH, ATT_S_ROWS, ATT_HEAD_DIM).astype(BF16)
    n_phys = cache_k.shape[1]
    kc = cache_k[l].reshape(n_phys, KROWS, ATT_HEAD_DIM)
    pad_new = lambda u: jnp.concatenate([u.reshape(DEC_BATCH, DEC_SEQ, -1),
                                         jnp.zeros((DEC_BATCH, PAGE_SIZE - DEC_SEQ, u.shape[-1]), F32)], axis=1)
    k_new = pad_new(k_rows[P_ROWS:]).reshape(DEC_BATCH, KROWS, ATT_HEAD_DIM)
    v_new = pad_new(v_rows[P_ROWS:]).reshape(1, DEC_BATCH, PAGE_SIZE, ATT_KV_HEADS, ATT_V_DIM)
    o_s = attn_sample(page_table.reshape(-1), q_s, kc, cache_v[l:l + 1], k_new, v_new, lam_params, subw)
    o_s = o_s.reshape(DEC_BATCH, ATT_KV_HEADS, DEC_SEQ, ATT_REP, ATT_V_DIM)
    o_s = jnp.transpose(o_s, (0, 2, 1, 3, 4)).reshape(S_ROWS, ATT_HEADS * ATT_V_DIM).astype(BF16)
    on = lax.dynamic_update_slice(on_p, o_s, (P_ROWS, 0))

    mix = mix_branches(yzn, on, proj, w_ssd_out[l].astype(BF16), w_attn_out[l].astype(BF16))
    rw = jnp.concatenate([router_w[l], jnp.zeros((d, LANES - N_EXPERTS), F32)], axis=1)
    rb = _pad_lanes(router_b[l], NEG)
    h2, n_rows, top_i, top_g = post_mixer(mix, x_all, w_o[l].astype(BF16), norm_ffn_w[l].reshape(1, d), rw, rb)

    rank, counts = expert_ranks(top_i)
    counts = counts[0, :N_EXPERTS]
    padded = (counts + MOE_TM - 1) // MOE_TM * MOE_TM
    pad_end = jnp.cumsum(padded)
    pad_start = pad_end - padded
    dest = pad_start[top_i[:, :TOP_K]] + rank[:, :TOP_K]
    is_tok = (jnp.arange(R) >= LEAD)[:, None]
    dump = MOE_NB * MOE_TM + jnp.arange(R * TOP_K).reshape(R, TOP_K)
    dest_scatter = jnp.where(is_tok, dest, dump).astype(I32).reshape(-1)
    dest_gather = jnp.where(is_tok, dest, 0).astype(I32).reshape(-1)
    block_start = jnp.arange(MOE_NB) * MOE_TM
    block_e = jnp.minimum(jnp.sum(pad_end[None, :] <= block_start[:, None], axis=1), N_EXPERTS - 1).astype(I32)
    n_used = (pad_end[-1:] // MOE_TM).astype(I32)
    ex = jnp.arange(N_EXPERTS)
    nonempty = padded > 0
    run_of_expert = jnp.cumsum(nonempty) - 1
    later = (ex[None, :] > ex[:, None]) & nonempty[None, :]
    next_expert = jnp.min(jnp.where(later, ex[None, :], N_EXPERTS), axis=1)
    next_expert = jnp.where(next_expert < N_EXPERTS, next_expert, -1)
    sched = (block_e, n_used, run_of_expert[block_e].astype(I32), next_expert[block_e].astype(I32))
    tail_start = jnp.where(nonempty, pad_end - MOE_TM, -1).astype(I32)
    xs = moe_dispatch(dest_scatter, tail_start, n_used, n_rows)
    hidden = moe_gate_up(sched, xs, w_gate_up[l], b_gate_up[l].reshape(N_EXPERTS, 1, 2 * D_FF))
    yb = moe_down(sched, hidden, w_down[l], b_down[l].reshape(N_EXPERTS, 1, d))
    y_p, y_s = moe_combine(dest_gather, h2, top_g, final_norm_w.reshape(1, d), yb)

    tok0 = LEAD
    y_prompt = y_p[None]
    y_sample = y_s.reshape(DEC_BATCH, DEC_SEQ, d)
    k_prompt = k_rows[tok0:P_ROWS].reshape(1, 1, SEQ + N_META, ATT_KV_HEADS, 2, ATT_HEAD_DIM)
    v_prompt = v_rows[tok0:P_ROWS].reshape(1, 1, SEQ + N_META, ATT_KV_HEADS, ATT_V_DIM)
    k_sample = k_rows[P_ROWS:].reshape(1, DEC_BATCH, DEC_SEQ, ATT_KV_HEADS, 2, ATT_HEAD_DIM)
    v_sample = v_rows[P_ROWS:].reshape(1, DEC_BATCH, DEC_SEQ, ATT_KV_HEADS, ATT_V_DIM)
    ssm_prompt = ssm_p.reshape(1, 1, SSD_HEADS, SSD_HEAD_DIM, SSD_STATE)
    ssm_sample = ssm_s.reshape(1, DEC_BATCH, SSD_HEADS, SSD_HEAD_DIM, SSD_STATE)
    xbc = proj[:, PCOL_XBC:PCOL_XBC + CONV_DIM]
    conv_prompt = xbc[P_ROWS - (SSD_CONV - 1):P_ROWS][None, None]
    xbc_s = xbc[P_ROWS:].reshape(DEC_BATCH, DEC_SEQ, CONV_DIM)
    conv_sample = jnp.concatenate([state_conv[l], xbc_s], axis=1)[:, -(SSD_CONV - 1):][None]
    return (y_prompt, y_sample, k_prompt, v_prompt, k_sample, v_sample, ssm_prompt, ssm_sample, conv_prompt, conv_sample)
```

```python
import functools
import math

import jax
import jax.numpy as jnp
import numpy as np
from jax import lax
from jax.experimental import pallas as pl
from jax.experimental.pallas import tpu as pltpu

F32 = jnp.float32
BF16 = jnp.bfloat16
I32 = jnp.int32
HI = lax.Precision.HIGHEST

D_MODEL = 2048
SEQ = 8192
DEC_BATCH = 32
DEC_SEQ = 4
PAGE_SIZE = 128
N_META = 16
NORM_EPS = 1e-5
SSD_INNER = D_MODEL
SSD_HEAD_DIM = 64
SSD_HEADS = SSD_INNER // SSD_HEAD_DIM
SSD_GROUPS = 4
SSD_REP = SSD_HEADS // SSD_GROUPS
SSD_STATE = 128
SSD_CONV = 4
SSD_CHUNK = 128
CONV_DIM = SSD_INNER + 2 * SSD_GROUPS * SSD_STATE
ATT_HEADS = 8
ATT_KV_HEADS = 4
ATT_REP = ATT_HEADS // ATT_KV_HEADS
ATT_HEAD_DIM = D_MODEL // ATT_HEADS // 2
ATT_V_DIM = 2 * ATT_HEAD_DIM
ATT_SCALE = ATT_HEAD_DIM ** -0.5
ROT_DIM = ATT_HEAD_DIM // 4
ROPE_THETA = 500000.0
N_EXPERTS = 32
TOP_K = 4
D_FF = D_MODEL
SWIGLU_LIMIT = 7.0
SWIGLU_ALPHA = 1.702
LAM_INIT = 0.8 - 0.6 * math.exp(-0.3 * 0)

LEAD = (-N_META) % SSD_CHUNK
P_ROWS = LEAD + N_META + SEQ
S_ROWS = DEC_BATCH * DEC_SEQ
R = P_ROWS + S_ROWS
N_CHUNKS = P_ROWS // SSD_CHUNK

LANES = 128
SUBLANES = 8
VMEM_LIMIT = 56 * 1024 * 1024

PCOL_Z, PCOL_GA, PCOL_GB, PCOL_XBC, PCOL_Q, PCOL_K, PCOL_V = 0, 2048, 4096, 6144, 9216, 11264, 12288
P_COLS = 13312
PROJ_TN = 1024
ROPE_BLOCKS = (PCOL_Q // PROJ_TN, PCOL_V // PROJ_TN)
SRC_Z, SRC_XBC, SRC_DT = 0, SSD_INNER, SSD_INNER + CONV_DIM
SRC_Q = SRC_DT + SSD_HEADS
SRC_GA = SRC_Q + 2 * D_MODEL
PROJ_SRC_ROW = tuple(base + PROJ_TN * b for base, n in
                     ((SRC_Z, 2), (SRC_GA, 4), (SRC_XBC, 3), (SRC_Q, 4)) for b in range(n))

MOE_TM = 256
MOE_NB = -(-(R * TOP_K) // MOE_TM) + N_EXPERTS
MOE_DUMP_BLOCKS = -(-(LEAD * TOP_K) // MOE_TM)
MOE_ROWS = (MOE_NB + MOE_DUMP_BLOCKS) * MOE_TM
MOE_TN_GATE_UP = 1024
NEG = -1e30


def _cparams(sem, vmem=VMEM_LIMIT):
    return pltpu.CompilerParams(dimension_semantics=sem, vmem_limit_bytes=vmem)


_NT = (((1,), (1,)), ((), ()))


def _sigmoid(x):
    return 1.0 / (1.0 + jnp.exp(-x))


def _rmsnorm_kernel(x_ref, w_ref, o_ref):
    x = x_ref[...]
    ms = jnp.mean(x * x, axis=-1, keepdims=True)
    o_ref[...] = (x * lax.rsqrt(ms + NORM_EPS) * w_ref[...]).astype(o_ref.dtype)


def rmsnorm_rows(x, w, tm=768):
    rows, d = x.shape
    return pl.pallas_call(
        _rmsnorm_kernel,
        grid=(rows // tm,),
        in_specs=[pl.BlockSpec((tm, d), lambda i: (i, 0)), pl.BlockSpec((1, d), lambda i: (0, 0))],
        out_specs=pl.BlockSpec((tm, d), lambda i: (i, 0)),
        out_shape=jax.ShapeDtypeStruct((rows, d), BF16),
        compiler_params=_cparams(("parallel",)),
        name="rmsnorm_rows",
    )(x, w.reshape(1, d))


def _rope(blk, c, sa, sb):
    return blk * c + pltpu.roll(blk, ROT_DIM // 2, axis=1) * sa + pltpu.roll(blk, LANES - ROT_DIM // 2, axis=1) * sb


def _proj_kernel(x_ref, w_ref, c_ref, sa_ref, sb_ref, o_ref, kb_ref, vt_ref):
    j = pl.program_id(1)
    acc = lax.dot_general(x_ref[...], w_ref[...], _NT, preferred_element_type=F32)
    is_rope = (j >= ROPE_BLOCKS[0]) & (j < ROPE_BLOCKS[1])

    @pl.when(is_rope)
    def _():
        c, sa, sb = c_ref[...], sa_ref[...], sb_ref[...]
        for g in range(acc.shape[1] // LANES):
            sl = slice(g * LANES, (g + 1) * LANES)
            o_ref[:, sl] = _rope(acc[:, sl], c, sa, sb)

    @pl.when(jnp.logical_not(is_rope))
    def _():
        o_ref[...] = acc

    @pl.when(j == PCOL_K // PROJ_TN)
    def _():
        kb_ref[...] = o_ref[...].astype(kb_ref.dtype)

    @pl.when(j == PCOL_V // PROJ_TN)
    def _():
        vt_ref[...] = acc.T.astype(vt_ref.dtype)


def in_projection(xn, wt_bf, cos_t, sin_a, sin_b, tm=1408):
    rows, d = xn.shape
    tn = PROJ_TN
    tab = pl.BlockSpec((tm, LANES), lambda i, j: (i, 0))

    def src_row(j):
        unit = math.gcd(*PROJ_SRC_ROW[1:])
        off = jnp.int32(0)
        for jj, r0 in enumerate(PROJ_SRC_ROW):
            off = jnp.where(j == jj, r0 // unit, off)
        return off * unit

    return pl.pallas_call(
        _proj_kernel,
        grid=(rows // tm, P_COLS // tn),
        in_specs=[pl.BlockSpec((tm, d), lambda i, j: (i, 0)),
                  pl.BlockSpec((pl.Element(tn), pl.Element(d)), lambda i, j: (src_row(j), 0)), tab, tab, tab],
        out_specs=[pl.BlockSpec((tm, tn), lambda i, j: (i, j)),
                   pl.BlockSpec((tm, tn), lambda i, j: (i, 0)),
                   pl.BlockSpec((tn, tm), lambda i, j: (0, i))],
        out_shape=[jax.ShapeDtypeStruct((rows, P_COLS), F32),
                   jax.ShapeDtypeStruct((rows, tn), BF16),
                   jax.ShapeDtypeStruct((tn, rows), BF16)],
        compiler_params=_cparams(("parallel", "arbitrary")),
        name="in_projection",
    )(xn, wt_bf, cos_t, sin_a, sin_b)


def _mm_kernel(x_ref, w_ref, o_ref):
    o_ref[...] = lax.dot_general(x_ref[...], w_ref[...], _NT, preferred_element_type=F32).astype(o_ref.dtype)


def dt_projection(xn, wt_bf, tm=1056):
    rows, d = xn.shape
    return pl.pallas_call(
        _mm_kernel,
        grid=(rows // tm,),
        in_specs=[pl.BlockSpec((tm, d), lambda i: (i, 0)), pl.BlockSpec((LANES, d), lambda i: (SRC_DT // LANES, 0))],
        out_specs=pl.BlockSpec((tm, LANES), lambda i: (i, 0)),
        out_shape=jax.ShapeDtypeStruct((rows, LANES), F32),
        compiler_params=_cparams(("parallel",)),
        name="dt_projection",
    )(xn, wt_bf)


def _ssd_chunk(xprev, xcur, dtraw, z, valid, s_ref, cw_ref, cb_ref, dtb_ref, alog_ref, dsk_ref, nw_ref, e_ref,
               state_dot_precision):
    L = xcur.shape[0]
    ext = jnp.concatenate([xprev, xcur], axis=0)
    off = SUBLANES - (SSD_CONV - 1)
    acc = cb_ref[...]
    for k in range(SSD_CONV):
        acc = acc + ext[off + k:off + k + L] * cw_ref[k:k + 1, :]
    act = acc * _sigmoid(acc)
    xs = act[:, :SSD_INNER]
    nb = SSD_GROUPS * SSD_STATE
    bm = act[:, SSD_INNER:SSD_INNER + nb].astype(BF16)
    cm = act[:, SSD_INNER + nb:].astype(BF16)

    dpre = dtraw + dtb_ref[...]
    dt = jnp.maximum(dpre, 0.0) + jnp.log1p(jnp.exp(-jnp.abs(dpre)))
    dt = jnp.where(valid, dt, 0.0)
    da = dt * (-jnp.exp(alog_ref[...]))

    li = lax.broadcasted_iota(I32, (L, L), 0)
    si = lax.broadcasted_iota(I32, (L, L), 1)
    tril = li >= si
    eye = li == si
    a_cs = jnp.dot(tril.astype(F32), da, precision=HI, preferred_element_type=F32)
    a_tot = a_cs[L - 1:L, :]
    e_mat = e_ref[...]

    def expand(v):
        out = None
        for _ in range(3):
            piece = v.astype(BF16)
            term = jnp.dot(piece, e_mat, preferred_element_type=F32)
            out = term if out is None else out + term
            v = v - piece.astype(F32)
        return out

    dt_x = expand(dt)
    eacs_x = expand(jnp.exp(a_cs))
    dte_x = expand(jnp.exp(a_tot - a_cs))
    etot = jnp.exp(a_tot)

    xdt = xs * dt_x
    xdt_b = xdt.astype(BF16)
    xde = xdt * dte_x
    gw = SSD_REP * SSD_HEAD_DIM
    lane = lax.broadcasted_iota(I32, (L, LANES), 1)
    nt = (((1,), (1,)), ((), ()))
    tn = (((0,), (0,)), ((), ()))
    y_groups = []
    for g in range(SSD_GROUPS):
        bg = bm[:, g * SSD_STATE:(g + 1) * SSD_STATE]
        cg = cm[:, g * SSD_STATE:(g + 1) * SSD_STATE]
        cb = lax.dot_general(cg, bg, nt, preferred_element_type=F32)
        s_g = s_ref[g * gw:(g + 1) * gw, :]
        y_off = lax.dot_general(cg, s_g.astype(BF16), nt, preferred_element_type=F32)
        if state_dot_precision is None:
            upd = lax.dot_general(xde[:, g * gw:(g + 1) * gw].astype(BF16), bg, tn, preferred_element_type=F32)
        else:
            upd = lax.dot_general(xde[:, g * gw:(g + 1) * gw], act[:, SSD_INNER + g * SSD_STATE:SSD_INNER + (g + 1) * SSD_STATE],
                                  tn, precision=state_dot_precision, preferred_element_type=F32)
        pairs = []
        decs = []
        for q in range(SSD_REP // 2):
            yd = []
            for r in range(2):
                h = g * SSD_REP + 2 * q + r
                colb = jnp.broadcast_to(a_cs[:, h:h + 1], (L, L))
                rowb = jnp.sum(jnp.where(eye, colb, 0.0), axis=0, keepdims=True)
                lm = jnp.where(tril, jnp.exp(jnp.minimum(colb - rowb, 0.0)), 0.0)
                m = (cb * lm).astype(BF16)
                c0 = (g * SSD_REP + 2 * q) * SSD_HEAD_DIM
                yd.append(jnp.dot(m, xdt_b[:, c0:c0 + LANES], preferred_element_type=F32))
                decs.append(jnp.broadcast_to(etot[:, h:h + 1], (SSD_HEAD_DIM, SSD_STATE)))
            pairs.append(jnp.where(lane < SSD_HEAD_DIM, yd[0], yd[1]))
        y_g = jnp.concatenate(pairs, axis=1) + y_off * eacs_x[:, g * gw:(g + 1) * gw]
        y_groups.append(y_g)
        s_ref[g * gw:(g + 1) * gw, :] = jnp.concatenate(decs, axis=0) * s_g + upd
    y = jnp.concatenate(y_groups, axis=1) + dsk_ref[...] * xs
    yz = y * (z * _sigmoid(z))
    outs = []
    for g in range(SSD_GROUPS):
        yg = yz[:, g * gw:(g + 1) * gw]
        ms = jnp.mean(yg * yg, axis=-1, keepdims=True)
        outs.append(yg * lax.rsqrt(ms + NORM_EPS) * nw_ref[:, g * gw:(g + 1) * gw])
    return jnp.concatenate(outs, axis=1)


def _ssd_prompt_kernel(xprev_ref, xcur_ref, dt_ref, z_ref, cw_ref, cb_ref, dtb_ref, alog_ref, dsk_ref, nw_ref, e_ref,
                       y_ref, s_ref):
    c = pl.program_id(0)

    @pl.when(c == 0)
    def _():
        s_ref[...] = jnp.zeros_like(s_ref)

    xprev = jnp.where(c == 0, 0.0, xprev_ref[...])
    row = c * SSD_CHUNK + lax.broadcasted_iota(I32, (SSD_CHUNK, 1), 0)
    y = _ssd_chunk(xprev, xcur_ref[...], dt_ref[...], z_ref[...], row >= LEAD, s_ref,
                   cw_ref, cb_ref, dtb_ref, alog_ref, dsk_ref, nw_ref, e_ref, None)
    y_ref[...] = y.astype(y_ref.dtype)


def _full(shape):
    return pl.BlockSpec(shape, lambda *_: (0,) * len(shape))


def ssd_prompt(proj, dtraw, params):
    L = SSD_CHUNK
    xb = PCOL_XBC // CONV_DIM
    in_specs = [
        pl.BlockSpec((SUBLANES, CONV_DIM), lambda c: (jnp.maximum(c * (L // SUBLANES) - 1, 0), xb)),
        pl.BlockSpec((L, CONV_DIM), lambda c: (c, xb)),
        pl.BlockSpec((L, LANES), lambda c: (c, 0)),
        pl.BlockSpec((L, SSD_INNER), lambda c: (c, PCOL_Z // SSD_INNER)),
    ] + [_full(p.shape) for p in params]
    return pl.pallas_call(
        _ssd_prompt_kernel,
        grid=(N_CHUNKS,),
        in_specs=in_specs,
        out_specs=[pl.BlockSpec((L, SSD_INNER), lambda c: (c, 0)), _full((SSD_INNER, SSD_STATE))],
        out_shape=[jax.ShapeDtypeStruct((P_ROWS, SSD_INNER), BF16), jax.ShapeDtypeStruct((SSD_INNER, SSD_STATE), F32)],
        compiler_params=_cparams(("arbitrary",)),
        name="ssd_prompt",
    )(proj, proj, dtraw, proj, *params)


SSD_S_NB = 4


def _ssd_sample_kernel(xprev_ref, xcur_ref, dt_ref, z_ref, sin_ref, cw_ref, cb_ref, dtb_ref, alog_ref, dsk_ref, nw_ref,
                       e_ref, y_ref, sout_ref):
    L = SUBLANES
    valid = lax.broadcasted_iota(I32, (L, 1), 0) < DEC_SEQ
    ys = []
    for b in range(SSD_S_NB):
        rows = slice(b * DEC_SEQ, (b + 1) * DEC_SEQ)
        pad = lambda u: jnp.concatenate([u, jnp.zeros((L - DEC_SEQ, u.shape[1]), u.dtype)], axis=0)
        sout_ref[b] = sin_ref[b]
        y = _ssd_chunk(xprev_ref[b], pad(xcur_ref[rows, :]), pad(dt_ref[rows, :]), pad(z_ref[rows, :]), valid,
                       sout_ref.at[b], cw_ref, cb_ref, dtb_ref, alog_ref, dsk_ref, nw_ref, e_ref, HI)
        ys.append(y[:DEC_SEQ])
    y_ref[...] = jnp.concatenate(ys, axis=0).astype(y_ref.dtype)


def ssd_sample(proj, dtraw, conv_prev, state, params):
    nb = SSD_S_NB
    rb = nb * DEC_SEQ
    r0 = P_ROWS // rb
    in_specs = [
        pl.BlockSpec((nb, SUBLANES, CONV_DIM), lambda i: (i, 0, 0)),
        pl.BlockSpec((rb, CONV_DIM), lambda i: (r0 + i, PCOL_XBC // CONV_DIM)),
        pl.BlockSpec((rb, LANES), lambda i: (r0 + i, 0)),
        pl.BlockSpec((rb, SSD_INNER), lambda i: (r0 + i, PCOL_Z // SSD_INNER)),
        pl.BlockSpec((nb, SSD_INNER, SSD_STATE), lambda i: (i, 0, 0)),
    ] + [_full(p.shape) for p in params]
    return pl.pallas_call(
        _ssd_sample_kernel,
        grid=(DEC_BATCH // nb,),
        in_specs=in_specs,
        out_specs=[pl.BlockSpec((rb, SSD_INNER), lambda i: (i, 0)),
                   pl.BlockSpec((nb, SSD_INNER, SSD_STATE), lambda i: (i, 0, 0))],
        out_shape=[jax.ShapeDtypeStruct((S_ROWS, SSD_INNER), BF16),
                   jax.ShapeDtypeStruct((DEC_BATCH, SSD_INNER, SSD_STATE), F32)],
        compiler_params=_cparams(("parallel",)),
        name="ssd_sample",
    )(conv_prev, proj, dtraw, proj, state, *params)


def _lambda(lq1_ref, lk1_ref, lq2_ref, lk2_ref):
    s1 = jnp.sum(lq1_ref[...] * lk1_ref[...], axis=-1, keepdims=True)
    s2 = jnp.sum(lq2_ref[...] * lk2_ref[...], axis=-1, keepdims=True)
    return jnp.exp(s1) - jnp.exp(s2) + LAM_INIT


def _subnorm(o, subw_ref):
    ms = jnp.mean(o * o, axis=-1, keepdims=True)
    return (o * lax.rsqrt(ms + NORM_EPS) * subw_ref[...]) * (1.0 - LAM_INIT)


ATT_TQ = 256
ATT_TK = 768
BF16_SUBLANES = 16
ATT_VT_ROWS = ATT_V_DIM + BF16_SUBLANES
EXP2_SCALE = ATT_SCALE * math.log2(math.e)


def _attn_prompt_kernel(q_ref, k_ref, vt_ref, d_ref, lq1_ref, lk1_ref, lq2_ref, lk2_ref, subw_ref, o_ref, m_s, acc_s):
    i = pl.program_id(1)
    tq, tk, hd = ATT_TQ, ATT_TK, ATT_HEAD_DIM
    q = q_ref[...]
    qc = [jnp.concatenate([q[:, r * 2 * hd + c * hd: r * 2 * hd + (c + 1) * hd] for r in range(ATT_REP)], axis=0).astype(BF16)
          for c in range(2)]
    m_s[...] = jnp.full_like(m_s, NEG)
    acc_s[...] = jnp.zeros_like(acc_s)
    nt = (((1,), (1,)), ((), ()))
    n_blocks = (i * tq + tq - 1) // tk + 1

    def step(j, mask, tkc=tk):
        k0 = pl.multiple_of(j * tk, tk)
        kblk = k_ref[pl.ds(k0, tkc), :]
        vt = vt_ref[:, pl.ds(k0, tkc)]
        sts = [lax.dot_general(kblk[:, c * hd:(c + 1) * hd], qc[c], nt, preferred_element_type=F32) for c in range(2)]
        ps, alphas = [], []
        for c in range(2):
            st = sts[c]
            if "causal" in mask:
                st = jnp.where(d_ref[:tkc, :] <= i * tq - j * tk, st, NEG)
            if "lead" in mask:
                st = st + jnp.where(lax.broadcasted_iota(I32, (tkc, 1), 0) < LEAD, NEG, 0.0)
            m_old = m_s[c]
            m_new = jnp.maximum(m_old, jnp.max(st, axis=0, keepdims=True))
            alphas.append(jnp.exp2((m_old - m_new) * EXP2_SCALE))
            ps.append(jnp.exp2((st - m_new) * EXP2_SCALE).astype(BF16))
            m_s[c] = m_new
        for c in range(2):
            acc_s[c] = alphas[c] * acc_s[c] + jnp.dot(vt, ps[c], preferred_element_type=F32)

    def diagonal_step(j, mask):
        live = (i * tq + tq - 1 - j * tk) // tq + 1
        for n in range(1, tk // tq + 1):
            @pl.when(live == n)
            def _():
                step(j, mask, n * tq)

    @pl.when(n_blocks == 1)
    def _():
        diagonal_step(0, ("causal", "lead"))

    @pl.when(n_blocks > 1)
    def _():
        step(0, ("lead",))

        def body(j, carry):
            step(j, ())
            return carry

        lax.fori_loop(1, n_blocks - 1, body, 0)
        diagonal_step(n_blocks - 1, ("causal",))

    lam = _lambda(lq1_ref, lk1_ref, lq2_ref, lk2_ref)
    for r in range(ATT_REP):
        cols = slice(r * tq, (r + 1) * tq)
        o0 = acc_s[0, :ATT_V_DIM, cols] / acc_s[0, ATT_V_DIM:ATT_V_DIM + 1, cols]
        o1 = acc_s[1, :ATT_V_DIM, cols] / acc_s[1, ATT_V_DIM:ATT_V_DIM + 1, cols]
        ot = o0 - lam * o1
        ms = jnp.mean(ot * ot, axis=0, keepdims=True)
        ot = (ot * lax.rsqrt(ms + NORM_EPS) * subw_ref[...]) * (1.0 - LAM_INIT)
        o_ref[:, r * ATT_V_DIM:(r + 1) * ATT_V_DIM] = ot.T.astype(o_ref.dtype)


def attn_prompt(proj, kb, vt_ext, lam_params, subw_col):
    tq = ATT_TQ
    gq = ATT_REP * 2 * ATT_HEAD_DIM
    vec = _full((1, ATT_HEAD_DIM))
    key_minus_query = (np.arange(ATT_TK)[:, None] - np.arange(ATT_REP * tq)[None, :] % tq).astype(np.int32)
    return pl.pallas_call(
        _attn_prompt_kernel,
        grid=(ATT_KV_HEADS, R // tq),
        in_specs=[pl.BlockSpec((tq, gq), lambda g, i: (i, PCOL_Q // gq + g)),
                  pl.BlockSpec((R, ATT_V_DIM), lambda g, i: (0, g)),
                  pl.BlockSpec((ATT_VT_ROWS, R), lambda g, i: (g, 0)),
                  _full((ATT_TK, ATT_REP * tq)),
                  vec, vec, vec, vec, _full((ATT_V_DIM, 1))],
        out_specs=pl.BlockSpec((tq, gq), lambda g, i: (i, g)),
        out_shape=jax.ShapeDtypeStruct((R, ATT_HEADS * ATT_V_DIM), BF16),
        scratch_shapes=[pltpu.VMEM((2, 1, ATT_REP * tq), F32),
                        pltpu.VMEM((2, ATT_VT_ROWS, ATT_REP * tq), F32)],
        compiler_params=_cparams(("parallel", "arbitrary")),
        name="attn_prompt",
    )(proj, kb, vt_ext, jnp.asarray(key_minus_query), *lam_params, subw_col)


N_PAGES = SEQ // PAGE_SIZE
ATT_S_ROWS = ATT_KV_HEADS * 2 * DEC_SEQ * ATT_REP
ATT_S_GROUP = 2 * DEC_SEQ * ATT_REP
ATT_S_PPS = 16
ATT_S_STEPS = N_PAGES // ATT_S_PPS
KROWS = PAGE_SIZE * ATT_KV_HEADS * 2
V_TILE = ATT_V_DIM
V_LANE_TILES = ATT_V_DIM // V_TILE


def _attn_sample_kernel(pt_ref, q_ref, *refs):
    npg = ATT_S_PPS
    kc_refs = refs[:npg]
    vc_refs = [refs[npg + V_LANE_TILES * p: npg + V_LANE_TILES * (p + 1)] for p in range(npg)]
    base = npg * (1 + V_LANE_TILES)
    kn_ref = refs[base]
    vn_refs = refs[base + 1: base + 1 + V_LANE_TILES]
    x_ref, lq1_ref, lk1_ref, lq2_ref, lk2_ref, subw_ref, o_ref, m_s, l_s, acc_s = refs[base + 1 + V_LANE_TILES:]
    j = pl.program_id(1)
    nt = (((1,), (1,)), ((), ()))
    nhc = ATT_KV_HEADS * 2
    rows_hc = DEC_SEQ * ATT_REP

    @pl.when(j == 0)
    def _():
        m_s[...] = jnp.full_like(m_s, NEG)
        l_s[...] = jnp.zeros_like(l_s)
        acc_s[...] = jnp.zeros_like(acc_s)

    def step(k_pages, v_pages, masked):
        q = q_ref[0]
        cols = []
        for k_src in k_pages:
            ss = []
            for hc in range(nhc):
                khc = k_src[pl.ds(hc, PAGE_SIZE, stride=nhc), :].astype(BF16)
                ss.append(lax.dot_general(q[hc * rows_hc:(hc + 1) * rows_hc, :], khc, nt, preferred_element_type=F32))
            cols.append(jnp.concatenate(ss, axis=0))
        s = jnp.concatenate(cols, axis=1) if len(cols) > 1 else cols[0]
        if masked:
            t_q = (lax.broadcasted_iota(I32, s.shape, 0) % rows_hc) // ATT_REP
            t_k = lax.broadcasted_iota(I32, s.shape, 1)
            s = jnp.where(t_k <= t_q, s, NEG)
        m_old = m_s[...]
        m_new = jnp.maximum(m_old, jnp.max(s, axis=1, keepdims=True))
        alpha = jnp.exp2((m_old - m_new) * EXP2_SCALE)
        p = jnp.exp2((s - m_new[:, :1]) * EXP2_SCALE)
        l_s[...] = alpha * l_s[...] + jnp.sum(p, axis=1, keepdims=True)
        pb = p.astype(BF16)
        head_of_col = lax.broadcasted_iota(I32, (ATT_S_ROWS, PAGE_SIZE * ATT_KV_HEADS), 1) % ATT_KV_HEADS
        head_of_row = lax.broadcasted_iota(I32, (ATT_S_ROWS, PAGE_SIZE * ATT_KV_HEADS), 0) // ATT_S_GROUP
        own_head = head_of_col == head_of_row
        pv = None
        for pi, v_tiles in enumerate(v_pages):
            v2 = jnp.concatenate([v[...].reshape(PAGE_SIZE * ATT_KV_HEADS, V_TILE) for v in v_tiles], axis=1).astype(BF16)
            spread = jnp.dot(pb[:, pi * PAGE_SIZE:(pi + 1) * PAGE_SIZE], x_ref[...], preferred_element_type=F32)
            pe = jnp.where(own_head, spread, 0.0).astype(BF16)
            d = jnp.dot(pe, v2, preferred_element_type=F32)
            pv = d if pv is None else pv + d
        acc_s[...] = alpha[:, :1] * acc_s[...] + pv
        m_s[...] = m_new

    @pl.when(j < ATT_S_STEPS)
    def _():
        step(kc_refs, vc_refs, False)

    @pl.when(j == ATT_S_STEPS)
    def _():
        step([kn_ref], [vn_refs], True)
        lam = _lambda(lq1_ref, lk1_ref, lq2_ref, lk2_ref)
        for g in range(ATT_KV_HEADS):
            r0 = g * ATT_S_GROUP
            o0 = acc_s[r0:r0 + rows_hc, :] / l_s[r0:r0 + rows_hc, :1]
            o1 = acc_s[r0 + rows_hc:r0 + 2 * rows_hc, :] / l_s[r0 + rows_hc:r0 + 2 * rows_hc, :1]
            o_ref[0, g] = _subnorm(o0 - lam * o1, subw_ref)


def attn_sample(page_table, q, kc, vc, knew, vnew, lam_params, subw):
    vec = _full((1, ATT_HEAD_DIM))
    kblk = (None, KROWS, ATT_HEAD_DIM)
    vblk = (None, None, PAGE_SIZE, ATT_KV_HEADS, V_TILE)

    def page(b, j, pt, p):
        return pt[b * N_PAGES + jnp.minimum(j, ATT_S_STEPS - 1) * ATT_S_PPS + p]

    k_specs = [pl.BlockSpec(kblk, functools.partial(lambda b, j, pt, p: (page(b, j, pt, p), 0, 0), p=p))
               for p in range(ATT_S_PPS)]
    v_specs = [pl.BlockSpec(vblk, functools.partial(lambda b, j, pt, p, e: (0, page(b, j, pt, p), 0, 0, e), p=p, e=e))
               for p in range(ATT_S_PPS) for e in range(V_LANE_TILES)]
    spread = np.zeros((PAGE_SIZE, PAGE_SIZE * ATT_KV_HEADS), np.float32)
    for t in range(PAGE_SIZE):
        spread[t, t * ATT_KV_HEADS:(t + 1) * ATT_KV_HEADS] = 1.0
    kn_spec = pl.BlockSpec(kblk, lambda b, j, pt: (b, 0, 0))
    vn_specs = [pl.BlockSpec(vblk, functools.partial(lambda b, j, pt, e: (0, b, 0, 0, e), e=e)) for e in range(V_LANE_TILES)]
    grid_spec = pltpu.PrefetchScalarGridSpec(
        num_scalar_prefetch=1,
        grid=(DEC_BATCH, ATT_S_STEPS + 1),
        in_specs=[pl.BlockSpec((1, ATT_S_ROWS, ATT_HEAD_DIM), lambda b, j, pt: (b, 0, 0)),
                  *k_specs, *v_specs, kn_spec, *vn_specs, _full((PAGE_SIZE, PAGE_SIZE * ATT_KV_HEADS)),
                  vec, vec, vec, vec, _full((1, ATT_V_DIM))],
        out_specs=pl.BlockSpec((1, ATT_KV_HEADS, DEC_SEQ * ATT_REP, ATT_V_DIM), lambda b, j, pt: (b, 0, 0, 0)),
        scratch_shapes=[pltpu.VMEM((ATT_S_ROWS, LANES), F32), pltpu.VMEM((ATT_S_ROWS, LANES), F32),
                        pltpu.VMEM((ATT_S_ROWS, ATT_V_DIM), F32)],
    )
    return pl.pallas_call(
        _attn_sample_kernel,
        grid_spec=grid_spec,
        out_shape=jax.ShapeDtypeStruct((DEC_BATCH, ATT_KV_HEADS, DEC_SEQ * ATT_REP, ATT_V_DIM), F32),
        compiler_params=_cparams(("parallel", "arbitrary")),
        name="attn_sample",
    )(page_table, q, *([kc] * ATT_S_PPS), *([vc] * (ATT_S_PPS * V_LANE_TILES)), knew, *([vnew] * V_LANE_TILES),
      jnp.asarray(spread, BF16), *lam_params, subw)


def _mix_kernel(ya_ref, yb_ref, ga_ref, gb_ref, wa_ref, wb_ref, o_ref):
    oa = jnp.dot(ya_ref[...], wa_ref[...], preferred_element_type=F32)
    ob = jnp.dot(yb_ref[...], wb_ref[...], preferred_element_type=F32)
    o_ref[...] = (_sigmoid(ga_ref[...]) * oa + _sigmoid(gb_ref[...]) * ob).astype(o_ref.dtype)


def mix_branches(yzn, on, proj, wa, wb, tm=256):
    d = D_MODEL
    row = lambda i: (i, 0)
    return pl.pallas_call(
        _mix_kernel,
        grid=(R // tm,),
        in_specs=[pl.BlockSpec((tm, d), row), pl.BlockSpec((tm, d), row),
                  pl.BlockSpec((tm, d), lambda i: (i, PCOL_GA // d)), pl.BlockSpec((tm, d), lambda i: (i, PCOL_GB // d)),
                  _full((d, d)), _full((d, d))],
        out_specs=pl.BlockSpec((tm, d), row),
        out_shape=jax.ShapeDtypeStruct((R, d), BF16),
        compiler_params=_cparams(("parallel",)),
        name="mix_branches",
    )(yzn, on, proj, proj, wa, wb)


def _post_kernel(mix_ref, h_ref, wo_ref, nw_ref, rw_ref, rb_ref, h2_ref, n_ref, ti_ref, tg_ref):
    h2 = h_ref[...] + jnp.dot(mix_ref[...], wo_ref[...], preferred_element_type=F32)
    h2_ref[...] = h2
    ms = jnp.mean(h2 * h2, axis=-1, keepdims=True)
    n = h2 * lax.rsqrt(ms + NORM_EPS) * nw_ref[...]
    n_ref[...] = n
    logits = jnp.dot(n, rw_ref[...], precision=HI, preferred_element_type=F32) + rb_ref[...]
    lane = lax.broadcasted_iota(I32, logits.shape, 1)
    work = logits
    vals, idxs = [], []
    for _ in range(TOP_K):
        mx = jnp.max(work, axis=1, keepdims=True)
        am = jnp.min(jnp.where(work == mx, lane, LANES), axis=1, keepdims=True)
        vals.append(mx)
        idxs.append(am)
        work = jnp.where(lane == am, -jnp.inf, work)
    es = [jnp.exp(v - vals[0]) for v in vals]
    den = es[0] + es[1] + es[2] + es[3]
    ti = jnp.zeros(logits.shape, I32)
    tg = jnp.zeros(logits.shape, F32)
    for k in range(TOP_K):
        ti = jnp.where(lane == k, idxs[k], ti)
        tg = jnp.where(lane == k, es[k] / den, tg)
    ti_ref[...] = ti
    tg_ref[...] = tg


def post_mixer(mix, h, wo, nw, rw, rb, tm=256):
    d = D_MODEL
    row = lambda i: (i, 0)
    return pl.pallas_call(
        _post_kernel,
        grid=(R // tm,),
        in_specs=[pl.BlockSpec((tm, d), row), pl.BlockSpec((tm, d), row), _full((d, d)), _full((1, d)),
                  _full((d, LANES)), _full((1, LANES))],
        out_specs=[pl.BlockSpec((tm, d), row), pl.BlockSpec((tm, d), row),
                   pl.BlockSpec((tm, LANES), row), pl.BlockSpec((tm, LANES), row)],
        out_shape=[jax.ShapeDtypeStruct((R, d), F32), jax.ShapeDtypeStruct((R, d), F32),
                   jax.ShapeDtypeStruct((R, LANES), I32), jax.ShapeDtypeStruct((R, LANES), F32)],
        compiler_params=_cparams(("parallel",)),
        name="post_mixer",
    )(mix, h, wo, nw, rw, rb)


def _rank_kernel(ti_ref, rank_ref, cnt_ref, carry_s):
    i = pl.program_id(0)
    tm = ti_ref.shape[0]

    @pl.when(i == 0)
    def _():
        carry_s[...] = jnp.zeros_like(carry_s)

    ti = ti_ref[...]
    lane = lax.broadcasted_iota(I32, ti.shape, 1)
    row = i * tm + lax.broadcasted_iota(I32, (tm, 1), 0)
    valid = row >= LEAD
    ohs = [jnp.where((lane == ti[:, k:k + 1]) & valid, 1.0, 0.0) for k in range(TOP_K)]
    osum = ohs[0] + ohs[1] + ohs[2] + ohs[3]
    li = lax.broadcasted_iota(I32, (tm, tm), 0)
    si = lax.broadcasted_iota(I32, (tm, tm), 1)
    before = jnp.dot((li > si).astype(BF16), osum.astype(BF16), preferred_element_type=F32) + carry_s[...]
    rank = jnp.zeros(ti.shape, I32)
    for k in range(TOP_K):
        rk = jnp.sum(ohs[k] * before, axis=1, keepdims=True)
        rank = jnp.where(lane == k, rk.astype(I32), rank)
    rank_ref[...] = rank
    carry_s[...] = carry_s[...] + jnp.sum(osum, axis=0, keepdims=True)
    cnt_ref[...] = carry_s[...].astype(I32)


def expert_ranks(ti, tm=256):
    return pl.pallas_call(
        _rank_kernel,
        grid=(R // tm,),
        in_specs=[pl.BlockSpec((tm, LANES), lambda i: (i, 0))],
        out_specs=[pl.BlockSpec((tm, LANES), lambda i: (i, 0)), _full((1, LANES))],
        out_shape=[jax.ShapeDtypeStruct((R, LANES), I32), jax.ShapeDtypeStruct((1, LANES), I32)],
        scratch_shapes=[pltpu.VMEM((1, LANES), F32)],
        compiler_params=_cparams(("arbitrary",)),
        name="expert_ranks",
    )(ti)


DISPATCH_TM = 256


def _dispatch_kernel(dest_ref, tail_ref, nu_ref, n_ref, xs_ref, zbuf, sem, zsem):
    i = pl.program_id(0)
    tm = DISPATCH_TM

    @pl.when(i == 0)
    def _():
        zbuf[...] = jnp.zeros_like(zbuf)

        def zero_block(row0):
            return pltpu.make_async_copy(zbuf, xs_ref.at[pl.ds(pl.multiple_of(row0, MOE_TM), MOE_TM), :], zsem)

        def each_unused_block(fn):
            def body(b, carry):
                fn(zero_block(b * MOE_TM))
                return carry

            lax.fori_loop(nu_ref[0], MOE_ROWS // MOE_TM, body, 0)

        for e in range(N_EXPERTS):
            @pl.when(tail_ref[e] >= 0)
            def _():
                zero_block(tail_ref[e]).start()

        each_unused_block(lambda c: c.start())
        for e in range(N_EXPERTS):
            @pl.when(tail_ref[e] >= 0)
            def _():
                zero_block(tail_ref[e]).wait()

        each_unused_block(lambda c: c.wait())

    def body(r, carry):
        for k in range(TOP_K):
            d = dest_ref[(i * tm + r) * TOP_K + k]
            pltpu.make_async_copy(n_ref.at[pl.ds(r, 1), :], xs_ref.at[pl.ds(d, 1), :], sem).start()
        return carry

    lax.fori_loop(0, tm, body, 0)
    for k in range(TOP_K):
        pltpu.make_async_copy(n_ref, xs_ref.at[pl.ds(0, tm), :], sem).wait()


def moe_dispatch(dest_flat, tail_start, n_used, n):
    tm = DISPATCH_TM
    d = n.shape[1]
    grid_spec = pltpu.PrefetchScalarGridSpec(
        num_scalar_prefetch=3,
        grid=(R // tm,),
        in_specs=[pl.BlockSpec((tm, d), lambda i, ds, ts, nu: (i, 0))],
        out_specs=pl.BlockSpec(memory_space=pl.ANY),
        scratch_shapes=[pltpu.VMEM((MOE_TM, d), n.dtype), pltpu.SemaphoreType.DMA, pltpu.SemaphoreType.DMA],
    )
    return pl.pallas_call(
        _dispatch_kernel,
        grid_spec=grid_spec,
        out_shape=jax.ShapeDtypeStruct((MOE_ROWS, d), n.dtype),
        compiler_params=_cparams(("arbitrary",)),
        name="moe_dispatch",
    )(dest_flat, tail_start, n_used, n)


def _new_expert(be_ref, i):
    return (i == 0) | (be_ref[i] != be_ref[jnp.maximum(i - 1, 0)])


def _stream_expert_weights(i, be_ref, run_ref, nxt_ref, copies, wbuf, wbf):
    slot = run_ref[i] % 2

    @pl.when(i == 0)
    def _():
        for c in copies(be_ref[0], 0):
            c.start()

    @pl.when(_new_expert(be_ref, i))
    def _():
        for c in copies(be_ref[i], slot):
            c.wait()

        @pl.when(nxt_ref[i] >= 0)
        def _():
            for c in copies(nxt_ref[i], 1 - slot):
                c.start()

        for t in range(wbf.shape[0]):
            wbf[t] = wbuf[slot, t].astype(BF16)


def _gmm1_kernel(be_ref, nu_ref, run_ref, nxt_ref, x_ref, w_hbm, bg_ref, bu_ref, h_ref, wbuf, wbf, sem):
    j = pl.program_id(0)
    i = pl.program_id(1)
    tn = MOE_TN_GATE_UP

    def copies(e, slot):
        cols = (pl.multiple_of(j * tn, tn), pl.multiple_of(D_FF + j * tn, tn))
        return [pltpu.make_async_copy(w_hbm.at[e, :, pl.ds(c0, tn)], wbuf.at[slot, t], sem.at[slot, t])
                for t, c0 in enumerate(cols)]

    @pl.when(i < nu_ref[0])
    def _():
        _stream_expert_weights(i, be_ref, run_ref, nxt_ref, copies, wbuf, wbf)
        x = x_ref[...].astype(BF16)
        hg = jnp.dot(x, wbf[0], preferred_element_type=F32) + bg_ref[0]
        hu = jnp.dot(x, wbf[1], preferred_element_type=F32) + bu_ref[0]
        g = jnp.minimum(hg, SWIGLU_LIMIT)
        u = jnp.clip(hu, -SWIGLU_LIMIT, SWIGLU_LIMIT)
        h_ref[...] = (g * _sigmoid(SWIGLU_ALPHA * g) * (u + 1.0)).astype(h_ref.dtype)

    @pl.when(i >= nu_ref[0])
    def _():
        h_ref[...] = jnp.zeros_like(h_ref)


def moe_gate_up(sched, xs, w_gu, b_gu):
    tm, tn = MOE_TM, MOE_TN_GATE_UP
    nj = D_FF // tn
    blk = lambda i, nu: jnp.minimum(i, nu[0] - 1)
    grid_spec = pltpu.PrefetchScalarGridSpec(
        num_scalar_prefetch=4,
        grid=(nj, MOE_NB),
        in_specs=[pl.BlockSpec((tm, D_MODEL), lambda j, i, be, nu, rn, nx: (blk(i, nu), 0)),
                  pl.BlockSpec(memory_space=pl.ANY),
                  pl.BlockSpec((1, 1, tn), lambda j, i, be, nu, rn, nx: (be[blk(i, nu)], 0, j)),
                  pl.BlockSpec((1, 1, tn), lambda j, i, be, nu, rn, nx: (be[blk(i, nu)], 0, nj + j))],
        out_specs=pl.BlockSpec((tm, tn), lambda j, i, be, nu, rn, nx: (i, j)),
        scratch_shapes=[pltpu.VMEM((2, 2, D_MODEL, tn), F32), pltpu.VMEM((2, D_MODEL, tn), BF16),
                        pltpu.SemaphoreType.DMA((2, 2))],
    )
    return pl.pallas_call(
        _gmm1_kernel,
        grid_spec=grid_spec,
        out_shape=jax.ShapeDtypeStruct((MOE_NB * tm, D_FF), BF16),
        compiler_params=_cparams(("arbitrary", "arbitrary")),
        name="moe_gate_up",
    )(*sched, xs, w_gu, b_gu, b_gu)


def _gmm2_kernel(be_ref, nu_ref, run_ref, nxt_ref, h_ref, w_hbm, bd_ref, y_ref, wbuf, wbf, sem):
    i = pl.program_id(0)

    def copies(e, slot):
        return [pltpu.make_async_copy(w_hbm.at[e], wbuf.at[slot, 0], sem.at[slot])]

    @pl.when(i < nu_ref[0])
    def _():
        _stream_expert_weights(i, be_ref, run_ref, nxt_ref, copies, wbuf, wbf)
        y_ref[...] = jnp.dot(h_ref[...], wbf[0], preferred_element_type=F32) + bd_ref[0]

    @pl.when(i >= nu_ref[0])
    def _():
        y_ref[...] = jnp.zeros_like(y_ref)


def moe_down(sched, hidden, w_dn, b_dn):
    tm = MOE_TM
    blk = lambda i, nu: jnp.minimum(i, nu[0] - 1)
    grid_spec = pltpu.PrefetchScalarGridSpec(
        num_scalar_prefetch=4,
        grid=(MOE_NB,),
        in_specs=[pl.BlockSpec((tm, D_FF), lambda i, be, nu, rn, nx: (blk(i, nu), 0)),
                  pl.BlockSpec(memory_space=pl.ANY),
                  pl.BlockSpec((1, 1, D_MODEL), lambda i, be, nu, rn, nx: (be[blk(i, nu)], 0, 0))],
        out_specs=pl.BlockSpec((tm, D_MODEL), lambda i, be, nu, rn, nx: (i, 0)),
        scratch_shapes=[pltpu.VMEM((2, 1, D_FF, D_MODEL), F32), pltpu.VMEM((1, D_FF, D_MODEL), BF16),
                        pltpu.SemaphoreType.DMA((2,))],
    )
    return pl.pallas_call(
        _gmm2_kernel,
        grid_spec=grid_spec,
        out_shape=jax.ShapeDtypeStruct((MOE_NB * tm, D_MODEL), F32),
        compiler_params=_cparams(("arbitrary",)),
        name="moe_down",
    )(*sched, hidden, w_dn, b_dn)


COMBINE_TM = 128
COMBINE_TILES = R // COMBINE_TM
PROMPT_TILE0 = (LEAD + N_META) // COMBINE_TM
SAMPLE_TILE = P_ROWS // COMBINE_TM


def _combine_kernel(src_ref, h2_ref, tg_ref, fw_ref, yb_ref, yp_ref, ys_ref, buf, sem):
    i = pl.program_id(0)
    tm = COMBINE_TM

    def gather(tile, slot):
        def body(r, carry):
            for k in range(TOP_K):
                s = src_ref[(tile * tm + r) * TOP_K + k]
                pltpu.make_async_copy(yb_ref.at[pl.ds(s, 1), :], buf.at[slot, k, pl.ds(r, 1), :], sem.at[slot]).start()
            return carry

        lax.fori_loop(0, tm, body, 0)

    @pl.when(i == 0)
    def _():
        gather(0, 0)

    @pl.when(i + 1 < COMBINE_TILES)
    def _():
        gather(i + 1, (i + 1) % 2)

    slot = i % 2
    for k in range(TOP_K):
        pltpu.make_async_copy(yb_ref.at[pl.ds(0, tm), :], buf.at[slot, k], sem.at[slot]).wait()
    tg = tg_ref[...]
    f = tg[:, 0:1] * buf[slot, 0]
    for k in range(1, TOP_K):
        f = f + tg[:, k:k + 1] * buf[slot, k]
    out = h2_ref[...] + f
    ms = jnp.mean(out * out, axis=-1, keepdims=True)
    y = out * lax.rsqrt(ms + NORM_EPS) * fw_ref[...]

    @pl.when((i >= PROMPT_TILE0) & (i < SAMPLE_TILE))
    def _():
        yp_ref[...] = y

    @pl.when(i == SAMPLE_TILE)
    def _():
        ys_ref[...] = y


def moe_combine(src_flat, h2, tg, fw, yb):
    tm = COMBINE_TM
    d = D_MODEL
    grid_spec = pltpu.PrefetchScalarGridSpec(
        num_scalar_prefetch=1,
        grid=(COMBINE_TILES,),
        in_specs=[pl.BlockSpec((tm, d), lambda i, s: (i, 0)), pl.BlockSpec((tm, LANES), lambda i, s: (i, 0)),
                  pl.BlockSpec((1, d), lambda i, s: (0, 0)), pl.BlockSpec(memory_space=pl.ANY)],
        out_specs=[pl.BlockSpec((tm, d), lambda i, s: (jnp.clip(i - PROMPT_TILE0, 0, SEQ // tm - 1), 0)),
                   pl.BlockSpec((tm, d), lambda i, s: (0, 0))],
        scratch_shapes=[pltpu.VMEM((2, TOP_K, tm, d), F32), pltpu.SemaphoreType.DMA((2,))],
    )
    return pl.pallas_call(
        _combine_kernel,
        grid_spec=grid_spec,
        out_shape=[jax.ShapeDtypeStruct((SEQ, d), F32), jax.ShapeDtypeStruct((S_ROWS, d), F32)],
        compiler_params=_cparams(("arbitrary",)),
        name="moe_combine",
    )(src_flat, h2, tg, fw, yb)


def _rope_tables():
    half = ROT_DIM // 2
    pos_p = jnp.maximum(jnp.arange(P_ROWS) - LEAD, 0)
    pos_s = SEQ + jnp.arange(S_ROWS) % DEC_SEQ
    pos = jnp.concatenate([pos_p, pos_s])
    inv_freq = 1.0 / (ROPE_THETA ** (jnp.arange(half, dtype=F32) * (2.0 / ROT_DIM)))
    ang = pos.astype(F32)[:, None] * inv_freq[None, :]
    cos, sin = jnp.cos(ang), jnp.sin(ang)
    zeros = jnp.zeros((R, LANES - ROT_DIM), F32)
    cos_t = jnp.concatenate([cos, cos, jnp.ones((R, LANES - ROT_DIM), F32)], axis=1)
    sin_a = jnp.concatenate([jnp.zeros_like(sin), sin, zeros], axis=1)
    sin_b = jnp.concatenate([-sin, jnp.zeros_like(sin), zeros], axis=1)
    return cos_t, sin_a, sin_b


def _pad_lanes(v, fill=0.0):
    v = v.reshape(1, -1)
    return jnp.concatenate([v, jnp.full((1, LANES - v.shape[1]), fill, v.dtype)], axis=1)


def kernel(x_prompt, x_sample, cache_k, cache_v, state_ssm, state_conv, page_table, meta_tokens,
           norm_mix_w, w_in, conv_w, conv_b, dt_bias, a_log, d_skip, ssd_norm_w, w_ssd_out,
           lambda_q1, lambda_k1, lambda_q2, lambda_k2, subln_w, w_attn_out, w_o, norm_ffn_w,
           router_w, router_b, w_gate_up, b_gate_up, w_down, b_down, final_norm_w):
    l = 0
    d = D_MODEL
    x_all = jnp.concatenate([jnp.zeros((LEAD, d), F32), meta_tokens, x_prompt[0], x_sample.reshape(S_ROWS, d)], axis=0)
    xn = rmsnorm_rows(x_all, norm_mix_w[l])
    wt_bf = w_in[l].T.astype(BF16)
    cos_t, sin_a, sin_b = _rope_tables()
    proj, kb, vt = in_projection(xn, wt_bf, cos_t, sin_a, sin_b)
    dtraw = dt_projection(xn, wt_bf)

    e_mat = np.zeros((LANES, SSD_INNER), np.float32)
    for h in range(SSD_HEADS):
        e_mat[h, h * SSD_HEAD_DIM:(h + 1) * SSD_HEAD_DIM] = 1.0
    ssd_params = (conv_w[l], conv_b[l].reshape(1, CONV_DIM), _pad_lanes(dt_bias[l]), _pad_lanes(a_log[l]),
                  jnp.repeat(d_skip[l], SSD_HEAD_DIM).reshape(1, SSD_INNER), ssd_norm_w[l].reshape(1, SSD_INNER),
                  jnp.asarray(e_mat, BF16))
    yzn_p, ssm_p = ssd_prompt(proj, dtraw, ssd_params)
    conv_prev = jnp.concatenate([jnp.zeros((DEC_BATCH, SUBLANES - (SSD_CONV - 1), CONV_DIM), F32), state_conv[l]], axis=1)
    yzn_s, ssm_s = ssd_sample(proj, dtraw, conv_prev, state_ssm[l].reshape(DEC_BATCH, SSD_INNER, SSD_STATE), ssd_params)
    yzn = jnp.concatenate([yzn_p, yzn_s], axis=0)

    k_rows = proj[:, PCOL_K:PCOL_V]
    v_rows = proj[:, PCOL_V:P_COLS]
    lam_params = (lambda_q1[l].reshape(1, -1), lambda_k1[l].reshape(1, -1), lambda_q2[l].reshape(1, -1),
                  lambda_k2[l].reshape(1, -1))
    subw = subln_w[l].reshape(1, ATT_V_DIM)
    vt_ext = jnp.concatenate([vt.reshape(ATT_KV_HEADS, ATT_V_DIM, R),
                              jnp.ones((ATT_KV_HEADS, ATT_VT_ROWS - ATT_V_DIM, R), BF16)], axis=1)
    on_p = attn_prompt(proj, kb, vt_ext.reshape(ATT_KV_HEADS * ATT_VT_ROWS, R), lam_params,
                       subln_w[l].reshape(ATT_V_DIM, 1))

    q_s = proj[P_ROWS:, PCOL_Q:PCOL_K].reshape(DEC_BATCH, DEC_SEQ, ATT_KV_HEADS, ATT_REP, 2, ATT_HEAD_DIM)
    q_s = jnp.transpose(q_s, (0, 2, 4, 1, 3, 5)).reshape(DEC_BATCH, ATT_S_ROWS, ATT_HEAD_DIM).astype(BF16)
    n_phys = cache_k.shape[1]
    kc = cache_k[l].reshape(n_phys, KROWS, ATT_HEAD_DIM)
    pad_new = lambda u: jnp.concatenate([u.reshape(DEC_BATCH, DEC_SEQ, -1),
                                         jnp.zeros((DEC_BATCH, PAGE_SIZE - DEC_SEQ, u.shape[-1]), F32)], axis=1)
    k_new = pad_new(k_rows[P_ROWS:]).reshape(DEC_BATCH, KROWS, ATT_HEAD_DIM)
    v_new = pad_new(v_rows[P_ROWS:]).reshape(1, DEC_BATCH, PAGE_SIZE, ATT_KV_HEADS, ATT_V_DIM)
    o_s = attn_sample(page_table.reshape(-1), q_s, kc, cache_v[l:l + 1], k_new, v_new, lam_params, subw)
    o_s = o_s.reshape(DEC_BATCH, ATT_KV_HEADS, DEC_SEQ, ATT_REP, ATT_V_DIM)
    o_s = jnp.transpose(o_s, (0, 2, 1, 3, 4)).reshape(S_ROWS, ATT_HEADS * ATT_V_DIM).astype(BF16)
    on = lax.dynamic_update_slice(on_p, o_s, (P_ROWS, 0))

    mix = mix_branches(yzn, on, proj, w_ssd_out[l].astype(BF16), w_attn_out[l].astype(BF16))
    rw = jnp.concatenate([router_w[l], jnp.zeros((d, LANES - N_EXPERTS), F32)], axis=1)
    rb = _pad_lanes(router_b[l], NEG)
    h2, n_rows, top_i, top_g = post_mixer(mix, x_all, w_o[l].astype(BF16), norm_ffn_w[l].reshape(1, d), rw, rb)

    rank, counts = expert_ranks(top_i)
    counts = counts[0, :N_EXPERTS]
    padded = (counts + MOE_TM - 1) // MOE_TM * MOE_TM
    pad_end = jnp.cumsum(padded)
    pad_start = pad_end - padded
    ex = jnp.arange(N_EXPERTS)
    ti_t = top_i[:, :TOP_K].T
    base_t = jnp.sum(jnp.where(ti_t[None] == ex[:, None, None], pad_start[:, None, None], 0), axis=0)
    dest_t = base_t + rank[:, :TOP_K].T
    tok = jnp.arange(R)[None, :]
    is_tok = tok >= LEAD
    dump_t = MOE_NB * MOE_TM + tok * TOP_K + jnp.arange(TOP_K)[:, None]
    dest_scatter = jnp.where(is_tok, dest_t, dump_t).astype(I32).T.reshape(-1)
    dest_gather = jnp.where(is_tok, dest_t, 0).astype(I32).T.reshape(-1)
    block_start = jnp.arange(MOE_NB) * MOE_TM
    block_e = jnp.minimum(jnp.sum(pad_end[None, :] <= block_start[:, None], axis=1), N_EXPERTS - 1).astype(I32)
    n_used = (pad_end[-1:] // MOE_TM).astype(I32)
    nonempty = padded > 0
    run_of_expert = jnp.cumsum(nonempty) - 1
    later = (ex[None, :] > ex[:, None]) & nonempty[None, :]
    next_expert = jnp.min(jnp.where(later, ex[None, :], N_EXPERTS), axis=1)
    next_expert = jnp.where(next_expert < N_EXPERTS, next_expert, -1)
    of_block = lambda per_expert: jnp.sum(jnp.where(block_e[:, None] == ex[None, :], per_expert[None, :], 0), axis=1)
    sched = (block_e, n_used, of_block(run_of_expert).astype(I32), of_block(next_expert).astype(I32))
    tail_start = jnp.where(nonempty, pad_end - MOE_TM, -1).astype(I32)
    xs = moe_dispatch(dest_scatter, tail_start, n_used, n_rows)
    hidden = moe_gate_up(sched, xs, w_gate_up[l], b_gate_up[l].reshape(N_EXPERTS, 1, 2 * D_FF))
    yb = moe_down(sched, hidden, w_down[l], b_down[l].reshape(N_EXPERTS, 1, d))
    y_p, y_s = moe_combine(dest_gather, h2, top_g, final_norm_w.reshape(1, d), yb)

    tok0 = LEAD
    y_prompt = y_p[None]
    y_sample = y_s.reshape(DEC_BATCH, DEC_SEQ, d)
    k_prompt = k_rows[tok0:P_ROWS].reshape(1, 1, SEQ + N_META, ATT_KV_HEADS, 2, ATT_HEAD_DIM)
    v_prompt = v_rows[tok0:P_ROWS].reshape(1, 1, SEQ + N_META, ATT_KV_HEADS, ATT_V_DIM)
    k_sample = k_rows[P_ROWS:].reshape(1, DEC_BATCH, DEC_SEQ, ATT_KV_HEADS, 2, ATT_HEAD_DIM)
    v_sample = v_rows[P_ROWS:].reshape(1, DEC_BATCH, DEC_SEQ, ATT_KV_HEADS, ATT_V_DIM)
    ssm_prompt = ssm_p.reshape(1, 1, SSD_HEADS, SSD_HEAD_DIM, SSD_STATE)
    ssm_sample = ssm_s.reshape(1, DEC_BATCH, SSD_HEADS, SSD_HEAD_DIM, SSD_STATE)
    xbc = proj[:, PCOL_XBC:PCOL_XBC + CONV_DIM]
    conv_prompt = xbc[P_ROWS - (SSD_CONV - 1):P_ROWS][None, None]
    xbc_s = xbc[P_ROWS:].reshape(DEC_BATCH, DEC_SEQ, CONV_DIM)
    conv_sample = jnp.concatenate([state_conv[l], xbc_s], axis=1)[:, -(SSD_CONV - 1):][None]
    return (y_prompt, y_sample, k_prompt, v_prompt, k_sample, v_sample, ssm_prompt, ssm_sample, conv_prompt, conv_sample)
```

```python
import functools
import math

import jax
import jax.numpy as jnp
import numpy as np
from jax import lax
from jax.experimental import pallas as pl
from jax.experimental.pallas import tpu as pltpu

F32 = jnp.float32
BF16 = jnp.bfloat16
I32 = jnp.int32
HI = lax.Precision.HIGHEST

D_MODEL = 2048
SEQ = 8192
DEC_BATCH = 32
DEC_SEQ = 4
PAGE_SIZE = 128
N_META = 16
NORM_EPS = 1e-5
SSD_INNER = D_MODEL
SSD_HEAD_DIM = 64
SSD_HEADS = SSD_INNER // SSD_HEAD_DIM
SSD_GROUPS = 4
SSD_REP = SSD_HEADS // SSD_GROUPS
SSD_STATE = 128
SSD_CONV = 4
SSD_CHUNK = 128
CONV_DIM = SSD_INNER + 2 * SSD_GROUPS * SSD_STATE
ATT_HEADS = 8
ATT_KV_HEADS = 4
ATT_REP = ATT_HEADS // ATT_KV_HEADS
ATT_HEAD_DIM = D_MODEL // ATT_HEADS // 2
ATT_V_DIM = 2 * ATT_HEAD_DIM
ATT_SCALE = ATT_HEAD_DIM ** -0.5
ROT_DIM = ATT_HEAD_DIM // 4
ROPE_THETA = 500000.0
N_EXPERTS = 32
TOP_K = 4
D_FF = D_MODEL
SWIGLU_LIMIT = 7.0
SWIGLU_ALPHA = 1.702
LAM_INIT = 0.8 - 0.6 * math.exp(-0.3 * 0)

LEAD = (-N_META) % SSD_CHUNK
P_ROWS = LEAD + N_META + SEQ
S_ROWS = DEC_BATCH * DEC_SEQ
R = P_ROWS + S_ROWS
N_CHUNKS = P_ROWS // SSD_CHUNK

LANES = 128
SUBLANES = 8
VMEM_LIMIT = 56 * 1024 * 1024

PCOL_Z, PCOL_GA, PCOL_GB, PCOL_XBC, PCOL_Q, PCOL_K, PCOL_V = 0, 2048, 4096, 6144, 9216, 11264, 12288
P_COLS = 13312
PROJ_TN = 1024
ROPE_BLOCKS = (PCOL_Q // PROJ_TN, PCOL_V // PROJ_TN)
SRC_Z, SRC_XBC, SRC_DT = 0, SSD_INNER, SSD_INNER + CONV_DIM
SRC_Q = SRC_DT + SSD_HEADS
SRC_GA = SRC_Q + 2 * D_MODEL
PROJ_SRC_ROW = tuple(base + PROJ_TN * b for base, n in
                     ((SRC_Z, 2), (SRC_GA, 4), (SRC_XBC, 3), (SRC_Q, 4)) for b in range(n))

MOE_TM = 256
MOE_NB = -(-(R * TOP_K) // MOE_TM) + N_EXPERTS
MOE_DUMP_BLOCKS = -(-(LEAD * TOP_K) // MOE_TM)
MOE_ROWS = (MOE_NB + MOE_DUMP_BLOCKS) * MOE_TM
MOE_TN_GATE_UP = 1024
NEG = -1e30


def _cparams(sem, vmem=VMEM_LIMIT):
    return pltpu.CompilerParams(dimension_semantics=sem, vmem_limit_bytes=vmem)


_NT = (((1,), (1,)), ((), ()))


def _sigmoid(x):
    return 1.0 / (1.0 + jnp.exp(-x))


def _rmsnorm_kernel(x_ref, w_ref, o_ref):
    x = x_ref[...]
    ms = jnp.mean(x * x, axis=-1, keepdims=True)
    o_ref[...] = (x * lax.rsqrt(ms + NORM_EPS) * w_ref[...]).astype(o_ref.dtype)


def rmsnorm_rows(x, w, tm=768):
    rows, d = x.shape
    return pl.pallas_call(
        _rmsnorm_kernel,
        grid=(rows // tm,),
        in_specs=[pl.BlockSpec((tm, d), lambda i: (i, 0)), pl.BlockSpec((1, d), lambda i: (0, 0))],
        out_specs=pl.BlockSpec((tm, d), lambda i: (i, 0)),
        out_shape=jax.ShapeDtypeStruct((rows, d), BF16),
        compiler_params=_cparams(("parallel",)),
        name="rmsnorm_rows",
    )(x, w.reshape(1, d))


def _rope(blk, c, sa, sb):
    return blk * c + pltpu.roll(blk, ROT_DIM // 2, axis=1) * sa + pltpu.roll(blk, LANES - ROT_DIM // 2, axis=1) * sb


def _proj_kernel(x_ref, w_ref, c_ref, sa_ref, sb_ref, o_ref, kb_ref, vt_ref):
    j = pl.program_id(1)
    acc = lax.dot_general(x_ref[...], w_ref[...], _NT, preferred_element_type=F32)
    is_rope = (j >= ROPE_BLOCKS[0]) & (j < ROPE_BLOCKS[1])

    @pl.when(is_rope)
    def _():
        c, sa, sb = c_ref[...], sa_ref[...], sb_ref[...]
        for g in range(acc.shape[1] // LANES):
            sl = slice(g * LANES, (g + 1) * LANES)
            o_ref[:, sl] = _rope(acc[:, sl], c, sa, sb)

    @pl.when(jnp.logical_not(is_rope))
    def _():
        o_ref[...] = acc

    @pl.when(j == PCOL_K // PROJ_TN)
    def _():
        kb_ref[...] = o_ref[...].astype(kb_ref.dtype)

    @pl.when(j == PCOL_V // PROJ_TN)
    def _():
        vt_ref[...] = acc.T.astype(vt_ref.dtype)


def in_projection(xn, wt_bf, cos_t, sin_a, sin_b, tm=1408):
    rows, d = xn.shape
    tn = PROJ_TN
    tab = pl.BlockSpec((tm, LANES), lambda i, j: (i, 0))

    def src_row(j):
        unit = math.gcd(*PROJ_SRC_ROW[1:])
        off = jnp.int32(0)
        for jj, r0 in enumerate(PROJ_SRC_ROW):
            off = jnp.where(j == jj, r0 // unit, off)
        return off * unit

    return pl.pallas_call(
        _proj_kernel,
        grid=(rows // tm, P_COLS // tn),
        in_specs=[pl.BlockSpec((tm, d), lambda i, j: (i, 0)),
                  pl.BlockSpec((pl.Element(tn), pl.Element(d)), lambda i, j: (src_row(j), 0)), tab, tab, tab],
        out_specs=[pl.BlockSpec((tm, tn), lambda i, j: (i, j)),
                   pl.BlockSpec((tm, tn), lambda i, j: (i, 0)),
                   pl.BlockSpec((tn, tm), lambda i, j: (0, i))],
        out_shape=[jax.ShapeDtypeStruct((rows, P_COLS), F32),
                   jax.ShapeDtypeStruct((rows, tn), BF16),
                   jax.ShapeDtypeStruct((tn, rows), BF16)],
        compiler_params=_cparams(("parallel", "arbitrary")),
        name="in_projection",
    )(xn, wt_bf, cos_t, sin_a, sin_b)


def _mm_kernel(x_ref, w_ref, o_ref):
    o_ref[...] = lax.dot_general(x_ref[...], w_ref[...], _NT, preferred_element_type=F32).astype(o_ref.dtype)


def dt_projection(xn, wt_bf, tm=1056):
    rows, d = xn.shape
    return pl.pallas_call(
        _mm_kernel,
        grid=(rows // tm,),
        in_specs=[pl.BlockSpec((tm, d), lambda i: (i, 0)), pl.BlockSpec((LANES, d), lambda i: (SRC_DT // LANES, 0))],
        out_specs=pl.BlockSpec((tm, LANES), lambda i: (i, 0)),
        out_shape=jax.ShapeDtypeStruct((rows, LANES), F32),
        compiler_params=_cparams(("parallel",)),
        name="dt_projection",
    )(xn, wt_bf)


def _ssd_chunk(xprev, xcur, dtraw, z, valid, s_ref, cw_ref, cb_ref, dtb_ref, alog_ref, dsk_ref, nw_ref, e_ref,
               state_dot_precision):
    L = xcur.shape[0]
    ext = jnp.concatenate([xprev, xcur], axis=0)
    off = SUBLANES - (SSD_CONV - 1)
    acc = cb_ref[...]
    for k in range(SSD_CONV):
        acc = acc + ext[off + k:off + k + L] * cw_ref[k:k + 1, :]
    act = acc * _sigmoid(acc)
    xs = act[:, :SSD_INNER]
    nb = SSD_GROUPS * SSD_STATE
    bm = act[:, SSD_INNER:SSD_INNER + nb].astype(BF16)
    cm = act[:, SSD_INNER + nb:].astype(BF16)

    dpre = dtraw + dtb_ref[...]
    dt = jnp.maximum(dpre, 0.0) + jnp.log1p(jnp.exp(-jnp.abs(dpre)))
    dt = jnp.where(valid, dt, 0.0)
    da = dt * (-jnp.exp(alog_ref[...]))

    li = lax.broadcasted_iota(I32, (L, L), 0)
    si = lax.broadcasted_iota(I32, (L, L), 1)
    tril = li >= si
    eye = li == si
    a_cs = jnp.dot(tril.astype(F32), da, precision=HI, preferred_element_type=F32)
    a_tot = a_cs[L - 1:L, :]
    e_mat = e_ref[...]

    def expand(v):
        out = None
        for _ in range(3):
            piece = v.astype(BF16)
            term = jnp.dot(piece, e_mat, preferred_element_type=F32)
            out = term if out is None else out + term
            v = v - piece.astype(F32)
        return out

    dt_x = expand(dt)
    eacs_x = expand(jnp.exp(a_cs))
    dte_x = expand(jnp.exp(a_tot - a_cs))
    etot = jnp.exp(a_tot)

    xdt = xs * dt_x
    xdt_b = xdt.astype(BF16)
    xde = xdt * dte_x
    gw = SSD_REP * SSD_HEAD_DIM
    lane = lax.broadcasted_iota(I32, (L, LANES), 1)
    nt = (((1,), (1,)), ((), ()))
    tn = (((0,), (0,)), ((), ()))
    y_groups = []
    for g in range(SSD_GROUPS):
        bg = bm[:, g * SSD_STATE:(g + 1) * SSD_STATE]
        cg = cm[:, g * SSD_STATE:(g + 1) * SSD_STATE]
        cb = lax.dot_general(cg, bg, nt, preferred_element_type=F32)
        s_g = s_ref[g * gw:(g + 1) * gw, :]
        y_off = lax.dot_general(cg, s_g.astype(BF16), nt, preferred_element_type=F32)
        if state_dot_precision is None:
            upd = lax.dot_general(xde[:, g * gw:(g + 1) * gw].astype(BF16), bg, tn, preferred_element_type=F32)
        else:
            upd = lax.dot_general(xde[:, g * gw:(g + 1) * gw], act[:, SSD_INNER + g * SSD_STATE:SSD_INNER + (g + 1) * SSD_STATE],
                                  tn, precision=state_dot_precision, preferred_element_type=F32)
        pairs = []
        decs = []
        for q in range(SSD_REP // 2):
            yd = []
            for r in range(2):
                h = g * SSD_REP + 2 * q + r
                colb = jnp.broadcast_to(a_cs[:, h:h + 1], (L, L))
                rowb = jnp.sum(jnp.where(eye, colb, 0.0), axis=0, keepdims=True)
                lm = jnp.where(tril, jnp.exp(jnp.minimum(colb - rowb, 0.0)), 0.0)
                m = (cb * lm).astype(BF16)
                c0 = (g * SSD_REP + 2 * q) * SSD_HEAD_DIM
                yd.append(jnp.dot(m, xdt_b[:, c0:c0 + LANES], preferred_element_type=F32))
                decs.append(jnp.broadcast_to(etot[:, h:h + 1], (SSD_HEAD_DIM, SSD_STATE)))
            pairs.append(jnp.where(lane < SSD_HEAD_DIM, yd[0], yd[1]))
        y_g = jnp.concatenate(pairs, axis=1) + y_off * eacs_x[:, g * gw:(g + 1) * gw]
        y_groups.append(y_g)
        s_ref[g * gw:(g + 1) * gw, :] = jnp.concatenate(decs, axis=0) * s_g + upd
    y = jnp.concatenate(y_groups, axis=1) + dsk_ref[...] * xs
    yz = y * (z * _sigmoid(z))
    outs = []
    for g in range(SSD_GROUPS):
        yg = yz[:, g * gw:(g + 1) * gw]
        ms = jnp.mean(yg * yg, axis=-1, keepdims=True)
        outs.append(yg * lax.rsqrt(ms + NORM_EPS) * nw_ref[:, g * gw:(g + 1) * gw])
    return jnp.concatenate(outs, axis=1)


def _ssd_prompt_kernel(xprev_ref, xcur_ref, dt_ref, z_ref, cw_ref, cb_ref, dtb_ref, alog_ref, dsk_ref, nw_ref, e_ref,
                       y_ref, s_ref):
    c = pl.program_id(0)

    @pl.when(c == 0)
    def _():
        s_ref[...] = jnp.zeros_like(s_ref)

    xprev = jnp.where(c == 0, 0.0, xprev_ref[...])
    row = c * SSD_CHUNK + lax.broadcasted_iota(I32, (SSD_CHUNK, 1), 0)
    y = _ssd_chunk(xprev, xcur_ref[...], dt_ref[...], z_ref[...], row >= LEAD, s_ref,
                   cw_ref, cb_ref, dtb_ref, alog_ref, dsk_ref, nw_ref, e_ref, None)
    y_ref[...] = y.astype(y_ref.dtype)


def _full(shape):
    return pl.BlockSpec(shape, lambda *_: (0,) * len(shape))


def ssd_prompt(proj, dtraw, params):
    L = SSD_CHUNK
    xb = PCOL_XBC // CONV_DIM
    in_specs = [
        pl.BlockSpec((SUBLANES, CONV_DIM), lambda c: (jnp.maximum(c * (L // SUBLANES) - 1, 0), xb)),
        pl.BlockSpec((L, CONV_DIM), lambda c: (c, xb)),
        pl.BlockSpec((L, LANES), lambda c: (c, 0)),
        pl.BlockSpec((L, SSD_INNER), lambda c: (c, PCOL_Z // SSD_INNER)),
    ] + [_full(p.shape) for p in params]
    return pl.pallas_call(
        _ssd_prompt_kernel,
        grid=(N_CHUNKS,),
        in_specs=in_specs,
        out_specs=[pl.BlockSpec((L, SSD_INNER), lambda c: (c, 0)), _full((SSD_INNER, SSD_STATE))],
        out_shape=[jax.ShapeDtypeStruct((P_ROWS, SSD_INNER), BF16), jax.ShapeDtypeStruct((SSD_INNER, SSD_STATE), F32)],
        compiler_params=_cparams(("arbitrary",)),
        name="ssd_prompt",
    )(proj, proj, dtraw, proj, *params)


SSD_S_NB = 4


def _ssd_sample_kernel(xprev_ref, xcur_ref, dt_ref, z_ref, sin_ref, cw_ref, cb_ref, dtb_ref, alog_ref, dsk_ref, nw_ref,
                       e_ref, y_ref, sout_ref):
    L = SUBLANES
    valid = lax.broadcasted_iota(I32, (L, 1), 0) < DEC_SEQ
    ys = []
    for b in range(SSD_S_NB):
        rows = slice(b * DEC_SEQ, (b + 1) * DEC_SEQ)
        pad = lambda u: jnp.concatenate([u, jnp.zeros((L - DEC_SEQ, u.shape[1]), u.dtype)], axis=0)
        sout_ref[b] = sin_ref[b]
        y = _ssd_chunk(xprev_ref[b], pad(xcur_ref[rows, :]), pad(dt_ref[rows, :]), pad(z_ref[rows, :]), valid,
                       sout_ref.at[b], cw_ref, cb_ref, dtb_ref, alog_ref, dsk_ref, nw_ref, e_ref, HI)
        ys.append(y[:DEC_SEQ])
    y_ref[...] = jnp.concatenate(ys, axis=0).astype(y_ref.dtype)


def ssd_sample(proj, dtraw, conv_prev, state, params):
    nb = SSD_S_NB
    rb = nb * DEC_SEQ
    r0 = P_ROWS // rb
    in_specs = [
        pl.BlockSpec((nb, SUBLANES, CONV_DIM), lambda i: (i, 0, 0)),
        pl.BlockSpec((rb, CONV_DIM), lambda i: (r0 + i, PCOL_XBC // CONV_DIM)),
        pl.BlockSpec((rb, LANES), lambda i: (r0 + i, 0)),
        pl.BlockSpec((rb, SSD_INNER), lambda i: (r0 + i, PCOL_Z // SSD_INNER)),
        pl.BlockSpec((nb, SSD_INNER, SSD_STATE), lambda i: (i, 0, 0)),
    ] + [_full(p.shape) for p in params]
    return pl.pallas_call(
        _ssd_sample_kernel,
        grid=(DEC_BATCH // nb,),
        in_specs=in_specs,
        out_specs=[pl.BlockSpec((rb, SSD_INNER), lambda i: (i, 0)),
                   pl.BlockSpec((nb, SSD_INNER, SSD_STATE), lambda i: (i, 0, 0))],
        out_shape=[jax.ShapeDtypeStruct((S_ROWS, SSD_INNER), BF16),
                   jax.ShapeDtypeStruct((DEC_BATCH, SSD_INNER, SSD_STATE), F32)],
        compiler_params=_cparams(("parallel",)),
        name="ssd_sample",
    )(conv_prev, proj, dtraw, proj, state, *params)


def _lambda(lq1_ref, lk1_ref, lq2_ref, lk2_ref):
    s1 = jnp.sum(lq1_ref[...] * lk1_ref[...], axis=-1, keepdims=True)
    s2 = jnp.sum(lq2_ref[...] * lk2_ref[...], axis=-1, keepdims=True)
    return jnp.exp(s1) - jnp.exp(s2) + LAM_INIT


def _subnorm(o, subw_ref):
    ms = jnp.mean(o * o, axis=-1, keepdims=True)
    return (o * lax.rsqrt(ms + NORM_EPS) * subw_ref[...]) * (1.0 - LAM_INIT)


ATT_TQ = 256
ATT_TK = 768
BF16_SUBLANES = 16
ATT_VT_ROWS = ATT_V_DIM + BF16_SUBLANES
EXP2_SCALE = ATT_SCALE * math.log2(math.e)


def _attn_prompt_kernel(q_ref, k_ref, vt_ref, d_ref, lq1_ref, lk1_ref, lq2_ref, lk2_ref, subw_ref, o_ref, m_s, acc_s):
    i = pl.program_id(1)
    tq, tk, hd = ATT_TQ, ATT_TK, ATT_HEAD_DIM
    q = q_ref[...]
    qc = [jnp.concatenate([q[:, r * 2 * hd + c * hd: r * 2 * hd + (c + 1) * hd] for r in range(ATT_REP)], axis=0).astype(BF16)
          for c in range(2)]
    m_s[...] = jnp.full_like(m_s, NEG)
    acc_s[...] = jnp.zeros_like(acc_s)
    nt = (((1,), (1,)), ((), ()))
    n_blocks = (i * tq + tq - 1) // tk + 1

    def step(j, mask, tkc=tk):
        k0 = pl.multiple_of(j * tk, tk)
        kblk = k_ref[pl.ds(k0, tkc), :]
        vt = vt_ref[:, pl.ds(k0, tkc)]
        sts = [lax.dot_general(kblk[:, c * hd:(c + 1) * hd], qc[c], nt, preferred_element_type=F32) for c in range(2)]
        ps, alphas = [], []
        for c in range(2):
            st = sts[c]
            if "causal" in mask:
                st = jnp.where(d_ref[:tkc, :] <= i * tq - j * tk, st, NEG)
            if "lead" in mask:
                st = st + jnp.where(lax.broadcasted_iota(I32, (tkc, 1), 0) < LEAD, NEG, 0.0)
            m_old = m_s[c]
            m_new = jnp.maximum(m_old, jnp.max(st, axis=0, keepdims=True))
            alphas.append(jnp.exp2((m_old - m_new) * EXP2_SCALE))
            ps.append(jnp.exp2((st - m_new) * EXP2_SCALE).astype(BF16))
            m_s[c] = m_new
        for c in range(2):
            acc_s[c] = alphas[c] * acc_s[c] + jnp.dot(vt, ps[c], preferred_element_type=F32)

    def diagonal_step(j, mask):
        live = (i * tq + tq - 1 - j * tk) // tq + 1
        for n in range(1, tk // tq + 1):
            @pl.when(live == n)
            def _():
                step(j, mask, n * tq)

    @pl.when(n_blocks == 1)
    def _():
        diagonal_step(0, ("causal", "lead"))

    @pl.when(n_blocks > 1)
    def _():
        step(0, ("lead",))

        def body(j, carry):
            step(j, ())
            return carry

        lax.fori_loop(1, n_blocks - 1, body, 0)
        diagonal_step(n_blocks - 1, ("causal",))

    lam = _lambda(lq1_ref, lk1_ref, lq2_ref, lk2_ref)
    for r in range(ATT_REP):
        cols = slice(r * tq, (r + 1) * tq)
        o0 = acc_s[0, :ATT_V_DIM, cols] / acc_s[0, ATT_V_DIM:ATT_V_DIM + 1, cols]
        o1 = acc_s[1, :ATT_V_DIM, cols] / acc_s[1, ATT_V_DIM:ATT_V_DIM + 1, cols]
        ot = o0 - lam * o1
        ms = jnp.mean(ot * ot, axis=0, keepdims=True)
        ot = (ot * lax.rsqrt(ms + NORM_EPS) * subw_ref[...]) * (1.0 - LAM_INIT)
        o_ref[:, r * ATT_V_DIM:(r + 1) * ATT_V_DIM] = ot.T.astype(o_ref.dtype)


def attn_prompt(proj, kb, vt_ext, lam_params, subw_col):
    tq = ATT_TQ
    gq = ATT_REP * 2 * ATT_HEAD_DIM
    vec = _full((1, ATT_HEAD_DIM))
    key_minus_query = (np.arange(ATT_TK)[:, None] - np.arange(ATT_REP * tq)[None, :] % tq).astype(np.int32)
    return pl.pallas_call(
        _attn_prompt_kernel,
        grid=(ATT_KV_HEADS, R // tq),
        in_specs=[pl.BlockSpec((tq, gq), lambda g, i: (i, PCOL_Q // gq + g)),
                  pl.BlockSpec((R, ATT_V_DIM), lambda g, i: (0, g)),
                  pl.BlockSpec((ATT_VT_ROWS, R), lambda g, i: (g, 0)),
                  _full((ATT_TK, ATT_REP * tq)),
                  vec, vec, vec, vec, _full((ATT_V_DIM, 1))],
        out_specs=pl.BlockSpec((tq, gq), lambda g, i: (i, g)),
        out_shape=jax.ShapeDtypeStruct((R, ATT_HEADS * ATT_V_DIM), BF16),
        scratch_shapes=[pltpu.VMEM((2, 1, ATT_REP * tq), F32),
                        pltpu.VMEM((2, ATT_VT_ROWS, ATT_REP * tq), F32)],
        compiler_params=_cparams(("parallel", "arbitrary")),
        name="attn_prompt",
    )(proj, kb, vt_ext, jnp.asarray(key_minus_query), *lam_params, subw_col)


N_PAGES = SEQ // PAGE_SIZE
ATT_S_ROWS = ATT_KV_HEADS * 2 * DEC_SEQ * ATT_REP
ATT_S_GROUP = 2 * DEC_SEQ * ATT_REP
ATT_S_PPS = 16
ATT_S_STEPS = N_PAGES // ATT_S_PPS
KROWS = PAGE_SIZE * ATT_KV_HEADS * 2
V_TILE = ATT_V_DIM
V_LANE_TILES = ATT_V_DIM // V_TILE


def _attn_sample_kernel(pt_ref, q_ref, *refs):
    npg = ATT_S_PPS
    kc_refs = refs[:npg]
    vc_refs = [refs[npg + V_LANE_TILES * p: npg + V_LANE_TILES * (p + 1)] for p in range(npg)]
    base = npg * (1 + V_LANE_TILES)
    kn_ref = refs[base]
    vn_refs = refs[base + 1: base + 1 + V_LANE_TILES]
    x_ref, lq1_ref, lk1_ref, lq2_ref, lk2_ref, subw_ref, o_ref, m_s, l_s, acc_s = refs[base + 1 + V_LANE_TILES:]
    j = pl.program_id(1)
    nt = (((1,), (1,)), ((), ()))
    nhc = ATT_KV_HEADS * 2
    rows_hc = DEC_SEQ * ATT_REP

    @pl.when(j == 0)
    def _():
        m_s[...] = jnp.full_like(m_s, NEG)
        l_s[...] = jnp.zeros_like(l_s)
        acc_s[...] = jnp.zeros_like(acc_s)

    def step(k_pages, v_pages, masked):
        q = q_ref[0]
        cols = []
        for k_src in k_pages:
            ss = []
            for hc in range(nhc):
                khc = k_src[pl.ds(hc, PAGE_SIZE, stride=nhc), :].astype(BF16)
                ss.append(lax.dot_general(q[hc * rows_hc:(hc + 1) * rows_hc, :], khc, nt, preferred_element_type=F32))
            cols.append(jnp.concatenate(ss, axis=0))
        s = jnp.concatenate(cols, axis=1) if len(cols) > 1 else cols[0]
        if masked:
            t_q = (lax.broadcasted_iota(I32, s.shape, 0) % rows_hc) // ATT_REP
            t_k = lax.broadcasted_iota(I32, s.shape, 1)
            s = jnp.where(t_k <= t_q, s, NEG)
        m_old = m_s[...]
        m_new = jnp.maximum(m_old, jnp.max(s, axis=1, keepdims=True))
        alpha = jnp.exp2((m_old - m_new) * EXP2_SCALE)
        p = jnp.exp2((s - m_new[:, :1]) * EXP2_SCALE)
        l_s[...] = alpha * l_s[...] + jnp.sum(p, axis=1, keepdims=True)
        pb = p.astype(BF16)
        head_of_col = lax.broadcasted_iota(I32, (ATT_S_ROWS, PAGE_SIZE * ATT_KV_HEADS), 1) % ATT_KV_HEADS
        head_of_row = lax.broadcasted_iota(I32, (ATT_S_ROWS, PAGE_SIZE * ATT_KV_HEADS), 0) // ATT_S_GROUP
        own_head = head_of_col == head_of_row
        pv = None
        for pi, v_tiles in enumerate(v_pages):
            v2 = jnp.concatenate([v[...].reshape(PAGE_SIZE * ATT_KV_HEADS, V_TILE) for v in v_tiles], axis=1).astype(BF16)
            spread = jnp.dot(pb[:, pi * PAGE_SIZE:(pi + 1) * PAGE_SIZE], x_ref[...], preferred_element_type=F32)
            pe = jnp.where(own_head, spread, 0.0).astype(BF16)
            d = jnp.dot(pe, v2, preferred_element_type=F32)
            pv = d if pv is None else pv + d
        acc_s[...] = alpha[:, :1] * acc_s[...] + pv
        m_s[...] = m_new

    @pl.when(j < ATT_S_STEPS)
    def _():
        step(kc_refs, vc_refs, False)

    @pl.when(j == ATT_S_STEPS)
    def _():
        step([kn_ref], [vn_refs], True)
        lam = _lambda(lq1_ref, lk1_ref, lq2_ref, lk2_ref)
        for g in range(ATT_KV_HEADS):
            r0 = g * ATT_S_GROUP
            o0 = acc_s[r0:r0 + rows_hc, :] / l_s[r0:r0 + rows_hc, :1]
            o1 = acc_s[r0 + rows_hc:r0 + 2 * rows_hc, :] / l_s[r0 + rows_hc:r0 + 2 * rows_hc, :1]
            o_ref[0, g] = _subnorm(o0 - lam * o1, subw_ref)


def attn_sample(page_table, q, kc, vc, knew, vnew, lam_params, subw):
    vec = _full((1, ATT_HEAD_DIM))
    kblk = (None, KROWS, ATT_HEAD_DIM)
    vblk = (None, None, PAGE_SIZE, ATT_KV_HEADS, V_TILE)

    def page(b, j, pt, p):
        return pt[b * N_PAGES + jnp.minimum(j, ATT_S_STEPS - 1) * ATT_S_PPS + p]

    k_specs = [pl.BlockSpec(kblk, functools.partial(lambda b, j, pt, p: (page(b, j, pt, p), 0, 0), p=p))
               for p in range(ATT_S_PPS)]
    v_specs = [pl.BlockSpec(vblk, functools.partial(lambda b, j, pt, p, e: (0, page(b, j, pt, p), 0, 0, e), p=p, e=e))
               for p in range(ATT_S_PPS) for e in range(V_LANE_TILES)]
    spread = np.zeros((PAGE_SIZE, PAGE_SIZE * ATT_KV_HEADS), np.float32)
    for t in range(PAGE_SIZE):
        spread[t, t * ATT_KV_HEADS:(t + 1) * ATT_KV_HEADS] = 1.0
    kn_spec = pl.BlockSpec(kblk, lambda b, j, pt: (b, 0, 0))
    vn_specs = [pl.BlockSpec(vblk, functools.partial(lambda b, j, pt, e: (0, b, 0, 0, e), e=e)) for e in range(V_LANE_TILES)]
    grid_spec = pltpu.PrefetchScalarGridSpec(
        num_scalar_prefetch=1,
        grid=(DEC_BATCH, ATT_S_STEPS + 1),
        in_specs=[pl.BlockSpec((1, ATT_S_ROWS, ATT_HEAD_DIM), lambda b, j, pt: (b, 0, 0)),
                  *k_specs, *v_specs, kn_spec, *vn_specs, _full((PAGE_SIZE, PAGE_SIZE * ATT_KV_HEADS)),
                  vec, vec, vec, vec, _full((1, ATT_V_DIM))],
        out_specs=pl.BlockSpec((1, ATT_KV_HEADS, DEC_SEQ * ATT_REP, ATT_V_DIM), lambda b, j, pt: (b, 0, 0, 0)),
        scratch_shapes=[pltpu.VMEM((ATT_S_ROWS, LANES), F32), pltpu.VMEM((ATT_S_ROWS, LANES), F32),
                        pltpu.VMEM((ATT_S_ROWS, ATT_V_DIM), F32)],
    )
    return pl.pallas_call(
        _attn_sample_kernel,
        grid_spec=grid_spec,
        out_shape=jax.ShapeDtypeStruct((DEC_BATCH, ATT_KV_HEADS, DEC_SEQ * ATT_REP, ATT_V_DIM), F32),
        compiler_params=_cparams(("parallel", "arbitrary")),
        name="attn_sample",
    )(page_table, q, *([kc] * ATT_S_PPS), *([vc] * (ATT_S_PPS * V_LANE_TILES)), knew, *([vnew] * V_LANE_TILES),
      jnp.asarray(spread, BF16), *lam_params, subw)


def _mix_kernel(ya_ref, yb_ref, ga_ref, gb_ref, wa_ref, wb_ref, o_ref):
    oa = jnp.dot(ya_ref[...], wa_ref[...], preferred_element_type=F32)
    ob = jnp.dot(yb_ref[...], wb_ref[...], preferred_element_type=F32)
    o_ref[...] = (_sigmoid(ga_ref[...]) * oa + _sigmoid(gb_ref[...]) * ob).astype(o_ref.dtype)


def mix_branches(yzn, on, proj, wa, wb, tm=256):
    d = D_MODEL
    row = lambda i: (i, 0)
    return pl.pallas_call(
        _mix_kernel,
        grid=(R // tm,),
        in_specs=[pl.BlockSpec((tm, d), row), pl.BlockSpec((tm, d), row),
                  pl.BlockSpec((tm, d), lambda i: (i, PCOL_GA // d)), pl.BlockSpec((tm, d), lambda i: (i, PCOL_GB // d)),
                  _full((d, d)), _full((d, d))],
        out_specs=pl.BlockSpec((tm, d), row),
        out_shape=jax.ShapeDtypeStruct((R, d), BF16),
        compiler_params=_cparams(("parallel",)),
        name="mix_branches",
    )(yzn, on, proj, proj, wa, wb)


def _post_kernel(mix_ref, h_ref, wo_ref, nw_ref, rw_ref, rb_ref, h2_ref, n_ref, ti_ref, tg_ref):
    h2 = h_ref[...] + jnp.dot(mix_ref[...], wo_ref[...], preferred_element_type=F32)
    h2_ref[...] = h2
    ms = jnp.mean(h2 * h2, axis=-1, keepdims=True)
    n = h2 * lax.rsqrt(ms + NORM_EPS) * nw_ref[...]
    n_ref[...] = n
    logits = jnp.dot(n, rw_ref[...], precision=HI, preferred_element_type=F32) + rb_ref[...]
    lane = lax.broadcasted_iota(I32, logits.shape, 1)
    work = logits
    vals, idxs = [], []
    for _ in range(TOP_K):
        mx = jnp.max(work, axis=1, keepdims=True)
        am = jnp.min(jnp.where(work == mx, lane, LANES), axis=1, keepdims=True)
        vals.append(mx)
        idxs.append(am)
        work = jnp.where(lane == am, -jnp.inf, work)
    es = [jnp.exp(v - vals[0]) for v in vals]
    den = es[0] + es[1] + es[2] + es[3]
    ti = jnp.zeros(logits.shape, I32)
    tg = jnp.zeros(logits.shape, F32)
    for k in range(TOP_K):
        ti = jnp.where(lane == k, idxs[k], ti)
        tg = jnp.where(lane == k, es[k] / den, tg)
    ti_ref[...] = ti
    tg_ref[...] = tg


def post_mixer(mix, h, wo, nw, rw, rb, tm=256):
    d = D_MODEL
    row = lambda i: (i, 0)
    return pl.pallas_call(
        _post_kernel,
        grid=(R // tm,),
        in_specs=[pl.BlockSpec((tm, d), row), pl.BlockSpec((tm, d), row), _full((d, d)), _full((1, d)),
                  _full((d, LANES)), _full((1, LANES))],
        out_specs=[pl.BlockSpec((tm, d), row), pl.BlockSpec((tm, d), row),
                   pl.BlockSpec((tm, LANES), row), pl.BlockSpec((tm, LANES), row)],
        out_shape=[jax.ShapeDtypeStruct((R, d), F32), jax.ShapeDtypeStruct((R, d), F32),
                   jax.ShapeDtypeStruct((R, LANES), I32), jax.ShapeDtypeStruct((R, LANES), F32)],
        compiler_params=_cparams(("parallel",)),
        name="post_mixer",
    )(mix, h, wo, nw, rw, rb)


def _rank_kernel(ti_ref, rank_ref, cnt_ref, carry_s):
    i = pl.program_id(0)
    tm = ti_ref.shape[0]

    @pl.when(i == 0)
    def _():
        carry_s[...] = jnp.zeros_like(carry_s)

    ti = ti_ref[...]
    lane = lax.broadcasted_iota(I32, ti.shape, 1)
    row = i * tm + lax.broadcasted_iota(I32, (tm, 1), 0)
    valid = row >= LEAD
    ohs = [jnp.where((lane == ti[:, k:k + 1]) & valid, 1.0, 0.0) for k in range(TOP_K)]
    osum = ohs[0] + ohs[1] + ohs[2] + ohs[3]
    li = lax.broadcasted_iota(I32, (tm, tm), 0)
    si = lax.broadcasted_iota(I32, (tm, tm), 1)
    before = jnp.dot((li > si).astype(BF16), osum.astype(BF16), preferred_element_type=F32) + carry_s[...]
    rank = jnp.zeros(ti.shape, I32)
    for k in range(TOP_K):
        rk = jnp.sum(ohs[k] * before, axis=1, keepdims=True)
        rank = jnp.where(lane == k, rk.astype(I32), rank)
    rank_ref[...] = rank
    carry_s[...] = carry_s[...] + jnp.sum(osum, axis=0, keepdims=True)
    cnt_ref[...] = carry_s[...].astype(I32)


def expert_ranks(ti, tm=256):
    return pl.pallas_call(
        _rank_kernel,
        grid=(R // tm,),
        in_specs=[pl.BlockSpec((tm, LANES), lambda i: (i, 0))],
        out_specs=[pl.BlockSpec((tm, LANES), lambda i: (i, 0)), _full((1, LANES))],
        out_shape=[jax.ShapeDtypeStruct((R, LANES), I32), jax.ShapeDtypeStruct((1, LANES), I32)],
        scratch_shapes=[pltpu.VMEM((1, LANES), F32)],
        compiler_params=_cparams(("arbitrary",)),
        name="expert_ranks",
    )(ti)


DISPATCH_TM = 256


def _dispatch_kernel(dest_ref, tail_ref, nu_ref, n_ref, xs_ref, zbuf, sem, zsem):
    i = pl.program_id(0)
    tm = DISPATCH_TM

    @pl.when(i == 0)
    def _():
        zbuf[...] = jnp.zeros_like(zbuf)

        def zero_block(row0):
            return pltpu.make_async_copy(zbuf, xs_ref.at[pl.ds(pl.multiple_of(row0, MOE_TM), MOE_TM), :], zsem)

        def each_unused_block(fn):
            def body(b, carry):
                fn(zero_block(b * MOE_TM))
                return carry

            lax.fori_loop(nu_ref[0], MOE_ROWS // MOE_TM, body, 0)

        for e in range(N_EXPERTS):
            @pl.when(tail_ref[e] >= 0)
            def _():
                zero_block(tail_ref[e]).start()

        each_unused_block(lambda c: c.start())
        for e in range(N_EXPERTS):
            @pl.when(tail_ref[e] >= 0)
            def _():
                zero_block(tail_ref[e]).wait()

        each_unused_block(lambda c: c.wait())

    def body(r, carry):
        for k in range(TOP_K):
            d = dest_ref[(i * tm + r) * TOP_K + k]
            pltpu.make_async_copy(n_ref.at[pl.ds(r, 1), :], xs_ref.at[pl.ds(d, 1), :], sem).start(priority=k % 2)
        return carry

    lax.fori_loop(0, tm, body, 0)
    for k in range(TOP_K):
        pltpu.make_async_copy(n_ref, xs_ref.at[pl.ds(0, tm), :], sem).wait()


def moe_dispatch(dest_flat, tail_start, n_used, n):
    tm = DISPATCH_TM
    d = n.shape[1]
    grid_spec = pltpu.PrefetchScalarGridSpec(
        num_scalar_prefetch=3,
        grid=(R // tm,),
        in_specs=[pl.BlockSpec((tm, d), lambda i, ds, ts, nu: (i, 0))],
        out_specs=pl.BlockSpec(memory_space=pl.ANY),
        scratch_shapes=[pltpu.VMEM((MOE_TM, d), n.dtype), pltpu.SemaphoreType.DMA, pltpu.SemaphoreType.DMA],
    )
    return pl.pallas_call(
        _dispatch_kernel,
        grid_spec=grid_spec,
        out_shape=jax.ShapeDtypeStruct((MOE_ROWS, d), n.dtype),
        compiler_params=_cparams(("arbitrary",)),
        name="moe_dispatch",
    )(dest_flat, tail_start, n_used, n)


def _new_expert(be_ref, i):
    return (i == 0) | (be_ref[i] != be_ref[jnp.maximum(i - 1, 0)])


def _stream_expert_weights(i, be_ref, run_ref, nxt_ref, copies, wbuf, wbf):
    slot = run_ref[i] % 2

    @pl.when(i == 0)
    def _():
        for c in copies(be_ref[0], 0):
            c.start()

    @pl.when(_new_expert(be_ref, i))
    def _():
        for c in copies(be_ref[i], slot):
            c.wait()

        @pl.when(nxt_ref[i] >= 0)
        def _():
            for c in copies(nxt_ref[i], 1 - slot):
                c.start()

        for t in range(wbf.shape[0]):
            wbf[t] = wbuf[slot, t].astype(BF16)


def _gmm1_kernel(be_ref, nu_ref, run_ref, nxt_ref, x_ref, w_hbm, bg_ref, bu_ref, h_ref, wbuf, wbf, sem):
    j = pl.program_id(0)
    i = pl.program_id(1)
    tn = MOE_TN_GATE_UP

    def copies(e, slot):
        cols = (pl.multiple_of(j * tn, tn), pl.multiple_of(D_FF + j * tn, tn))
        return [pltpu.make_async_copy(w_hbm.at[e, :, pl.ds(c0, tn)], wbuf.at[slot, t], sem.at[slot, t])
                for t, c0 in enumerate(cols)]

    @pl.when(i < nu_ref[0])
    def _():
        _stream_expert_weights(i, be_ref, run_ref, nxt_ref, copies, wbuf, wbf)
        x = x_ref[...].astype(BF16)
        hg = jnp.dot(x, wbf[0], preferred_element_type=F32) + bg_ref[0]
        hu = jnp.dot(x, wbf[1], preferred_element_type=F32) + bu_ref[0]
        g = jnp.minimum(hg, SWIGLU_LIMIT)
        u = jnp.clip(hu, -SWIGLU_LIMIT, SWIGLU_LIMIT)
        h_ref[...] = (g * _sigmoid(SWIGLU_ALPHA * g) * (u + 1.0)).astype(h_ref.dtype)

    @pl.when(i >= nu_ref[0])
    def _():
        h_ref[...] = jnp.zeros_like(h_ref)


def moe_gate_up(sched, xs, w_gu, b_gu):
    tm, tn = MOE_TM, MOE_TN_GATE_UP
    nj = D_FF // tn
    blk = lambda i, nu: jnp.minimum(i, nu[0] - 1)
    grid_spec = pltpu.PrefetchScalarGridSpec(
        num_scalar_prefetch=4,
        grid=(nj, MOE_NB),
        in_specs=[pl.BlockSpec((tm, D_MODEL), lambda j, i, be, nu, rn, nx: (blk(i, nu), 0)),
                  pl.BlockSpec(memory_space=pl.ANY),
                  pl.BlockSpec((1, 1, tn), lambda j, i, be, nu, rn, nx: (be[blk(i, nu)], 0, j)),
                  pl.BlockSpec((1, 1, tn), lambda j, i, be, nu, rn, nx: (be[blk(i, nu)], 0, nj + j))],
        out_specs=pl.BlockSpec((tm, tn), lambda j, i, be, nu, rn, nx: (i, j)),
        scratch_shapes=[pltpu.VMEM((2, 2, D_MODEL, tn), F32), pltpu.VMEM((2, D_MODEL, tn), BF16),
                        pltpu.SemaphoreType.DMA((2, 2))],
    )
    return pl.pallas_call(
        _gmm1_kernel,
        grid_spec=grid_spec,
        out_shape=jax.ShapeDtypeStruct((MOE_NB * tm, D_FF), BF16),
        compiler_params=_cparams(("arbitrary", "arbitrary")),
        name="moe_gate_up",
    )(*sched, xs, w_gu, b_gu, b_gu)


def _gmm2_kernel(be_ref, nu_ref, run_ref, nxt_ref, h_ref, w_hbm, bd_ref, y_ref, wbuf, wbf, sem):
    i = pl.program_id(0)

    def copies(e, slot):
        return [pltpu.make_async_copy(w_hbm.at[e], wbuf.at[slot, 0], sem.at[slot])]

    @pl.when(i < nu_ref[0])
    def _():
        _stream_expert_weights(i, be_ref, run_ref, nxt_ref, copies, wbuf, wbf)
        y_ref[...] = jnp.dot(h_ref[...], wbf[0], preferred_element_type=F32) + bd_ref[0]

    @pl.when(i >= nu_ref[0])
    def _():
        y_ref[...] = jnp.zeros_like(y_ref)


def moe_down(sched, hidden, w_dn, b_dn):
    tm = MOE_TM
    blk = lambda i, nu: jnp.minimum(i, nu[0] - 1)
    grid_spec = pltpu.PrefetchScalarGridSpec(
        num_scalar_prefetch=4,
        grid=(MOE_NB,),
        in_specs=[pl.BlockSpec((tm, D_FF), lambda i, be, nu, rn, nx: (blk(i, nu), 0)),
                  pl.BlockSpec(memory_space=pl.ANY),
                  pl.BlockSpec((1, 1, D_MODEL), lambda i, be, nu, rn, nx: (be[blk(i, nu)], 0, 0))],
        out_specs=pl.BlockSpec((tm, D_MODEL), lambda i, be, nu, rn, nx: (i, 0)),
        scratch_shapes=[pltpu.VMEM((2, 1, D_FF, D_MODEL), F32), pltpu.VMEM((1, D_FF, D_MODEL), BF16),
                        pltpu.SemaphoreType.DMA((2,))],
    )
    return pl.pallas_call(
        _gmm2_kernel,
        grid_spec=grid_spec,
        out_shape=jax.ShapeDtypeStruct((MOE_NB * tm, D_MODEL), F32),
        compiler_params=_cparams(("arbitrary",)),
        name="moe_down",
    )(*sched, hidden, w_dn, b_dn)


COMBINE_TM = 128
COMBINE_TILES = R // COMBINE_TM
PROMPT_TILE0 = (LEAD + N_META) // COMBINE_TM
SAMPLE_TILE = P_ROWS // COMBINE_TM


def _combine_kernel(src_ref, h2_ref, tg_ref, fw_ref, yb_ref, yp_ref, ys_ref, buf, sem):
    i = pl.program_id(0)
    tm = COMBINE_TM

    def gather(tile, slot):
        def body(r, carry):
            for k in range(TOP_K):
                s = src_ref[(tile * tm + r) * TOP_K + k]
                pltpu.make_async_copy(yb_ref.at[pl.ds(s, 1), :], buf.at[slot, k, pl.ds(r, 1), :],
                                      sem.at[slot]).start(priority=k % 2)
            return carry

        lax.fori_loop(0, tm, body, 0)

    @pl.when(i == 0)
    def _():
        gather(0, 0)

    @pl.when(i + 1 < COMBINE_TILES)
    def _():
        gather(i + 1, (i + 1) % 2)

    slot = i % 2
    for k in range(TOP_K):
        pltpu.make_async_copy(yb_ref.at[pl.ds(0, tm), :], buf.at[slot, k], sem.at[slot]).wait()
    tg = tg_ref[...]
    f = tg[:, 0:1] * buf[slot, 0]
    for k in range(1, TOP_K):
        f = f + tg[:, k:k + 1] * buf[slot, k]
    out = h2_ref[...] + f
    ms = jnp.mean(out * out, axis=-1, keepdims=True)
    y = out * lax.rsqrt(ms + NORM_EPS) * fw_ref[...]

    @pl.when((i >= PROMPT_TILE0) & (i < SAMPLE_TILE))
    def _():
        yp_ref[...] = y

    @pl.when(i == SAMPLE_TILE)
    def _():
        ys_ref[...] = y


def moe_combine(src_flat, h2, tg, fw, yb):
    tm = COMBINE_TM
    d = D_MODEL
    grid_spec = pltpu.PrefetchScalarGridSpec(
        num_scalar_prefetch=1,
        grid=(COMBINE_TILES,),
        in_specs=[pl.BlockSpec((tm, d), lambda i, s: (i, 0)), pl.BlockSpec((tm, LANES), lambda i, s: (i, 0)),
                  pl.BlockSpec((1, d), lambda i, s: (0, 0)), pl.BlockSpec(memory_space=pl.ANY)],
        out_specs=[pl.BlockSpec((tm, d), lambda i, s: (jnp.clip(i - PROMPT_TILE0, 0, SEQ // tm - 1), 0)),
                   pl.BlockSpec((tm, d), lambda i, s: (0, 0))],
        scratch_shapes=[pltpu.VMEM((2, TOP_K, tm, d), F32), pltpu.SemaphoreType.DMA((2,))],
    )
    return pl.pallas_call(
        _combine_kernel,
        grid_spec=grid_spec,
        out_shape=[jax.ShapeDtypeStruct((SEQ, d), F32), jax.ShapeDtypeStruct((S_ROWS, d), F32)],
        compiler_params=_cparams(("arbitrary",)),
        name="moe_combine",
    )(src_flat, h2, tg, fw, yb)


def _rope_tables():
    half = ROT_DIM // 2
    pos_p = jnp.maximum(jnp.arange(P_ROWS) - LEAD, 0)
    pos_s = SEQ + jnp.arange(S_ROWS) % DEC_SEQ
    pos = jnp.concatenate([pos_p, pos_s])
    inv_freq = 1.0 / (ROPE_THETA ** (jnp.arange(half, dtype=F32) * (2.0 / ROT_DIM)))
    ang = pos.astype(F32)[:, None] * inv_freq[None, :]
    cos, sin = jnp.cos(ang), jnp.sin(ang)
    zeros = jnp.zeros((R, LANES - ROT_DIM), F32)
    cos_t = jnp.concatenate([cos, cos, jnp.ones((R, LANES - ROT_DIM), F32)], axis=1)
    sin_a = jnp.concatenate([jnp.zeros_like(sin), sin, zeros], axis=1)
    sin_b = jnp.concatenate([-sin, jnp.zeros_like(sin), zeros], axis=1)
    return cos_t, sin_a, sin_b


def _pad_lanes(v, fill=0.0):
    v = v.reshape(1, -1)
    return jnp.concatenate([v, jnp.full((1, LANES - v.shape[1]), fill, v.dtype)], axis=1)


def kernel(x_prompt, x_sample, cache_k, cache_v, state_ssm, state_conv, page_table, meta_tokens,
           norm_mix_w, w_in, conv_w, conv_b, dt_bias, a_log, d_skip, ssd_norm_w, w_ssd_out,
           lambda_q1, lambda_k1, lambda_q2, lambda_k2, subln_w, w_attn_out, w_o, norm_ffn_w,
           router_w, router_b, w_gate_up, b_gate_up, w_down, b_down, final_norm_w):
    l = 0
    d = D_MODEL
    x_all = jnp.concatenate([jnp.zeros((LEAD, d), F32), meta_tokens, x_prompt[0], x_sample.reshape(S_ROWS, d)], axis=0)
    xn = rmsnorm_rows(x_all, norm_mix_w[l])
    wt_bf = w_in[l].T.astype(BF16)
    cos_t, sin_a, sin_b = _rope_tables()
    proj, kb, vt = in_projection(xn, wt_bf, cos_t, sin_a, sin_b)
    dtraw = dt_projection(xn, wt_bf)

    e_mat = np.zeros((LANES, SSD_INNER), np.float32)
    for h in range(SSD_HEADS):
        e_mat[h, h * SSD_HEAD_DIM:(h + 1) * SSD_HEAD_DIM] = 1.0
    ssd_params = (conv_w[l], conv_b[l].reshape(1, CONV_DIM), _pad_lanes(dt_bias[l]), _pad_lanes(a_log[l]),
                  jnp.repeat(d_skip[l], SSD_HEAD_DIM).reshape(1, SSD_INNER), ssd_norm_w[l].reshape(1, SSD_INNER),
                  jnp.asarray(e_mat, BF16))
    yzn_p, ssm_p = ssd_prompt(proj, dtraw, ssd_params)
    conv_prev = jnp.concatenate([jnp.zeros((DEC_BATCH, SUBLANES - (SSD_CONV - 1), CONV_DIM), F32), state_conv[l]], axis=1)
    yzn_s, ssm_s = ssd_sample(proj, dtraw, conv_prev, state_ssm[l].reshape(DEC_BATCH, SSD_INNER, SSD_STATE), ssd_params)
    yzn = jnp.concatenate([yzn_p, yzn_s], axis=0)

    k_rows = proj[:, PCOL_K:PCOL_V]
    v_rows = proj[:, PCOL_V:P_COLS]
    lam_params = (lambda_q1[l].reshape(1, -1), lambda_k1[l].reshape(1, -1), lambda_q2[l].reshape(1, -1),
                  lambda_k2[l].reshape(1, -1))
    subw = subln_w[l].reshape(1, ATT_V_DIM)
    vt_ext = jnp.concatenate([vt.reshape(ATT_KV_HEADS, ATT_V_DIM, R),
                              jnp.ones((ATT_KV_HEADS, ATT_VT_ROWS - ATT_V_DIM, R), BF16)], axis=1)
    on_p = attn_prompt(proj, kb, vt_ext.reshape(ATT_KV_HEADS * ATT_VT_ROWS, R), lam_params,
                       subln_w[l].reshape(ATT_V_DIM, 1))

    q_s = proj[P_ROWS:, PCOL_Q:PCOL_K].reshape(DEC_BATCH, DEC_SEQ, ATT_KV_HEADS, ATT_REP, 2, ATT_HEAD_DIM)
    q_s = jnp.transpose(q_s, (0, 2, 4, 1, 3, 5)).reshape(DEC_BATCH, ATT_S_ROWS, ATT_HEAD_DIM).astype(BF16)
    n_phys = cache_k.shape[1]
    kc = cache_k[l].reshape(n_phys, KROWS, ATT_HEAD_DIM)
    pad_new = lambda u: jnp.concatenate([u.reshape(DEC_BATCH, DEC_SEQ, -1),
                                         jnp.zeros((DEC_BATCH, PAGE_SIZE - DEC_SEQ, u.shape[-1]), F32)], axis=1)
    k_new = pad_new(k_rows[P_ROWS:]).reshape(DEC_BATCH, KROWS, ATT_HEAD_DIM)
    v_new = pad_new(v_rows[P_ROWS:]).reshape(1, DEC_BATCH, PAGE_SIZE, ATT_KV_HEADS, ATT_V_DIM)
    o_s = attn_sample(page_table.reshape(-1), q_s, kc, cache_v[l:l + 1], k_new, v_new, lam_params, subw)
    o_s = o_s.reshape(DEC_BATCH, ATT_KV_HEADS, DEC_SEQ, ATT_REP, ATT_V_DIM)
    o_s = jnp.transpose(o_s, (0, 2, 1, 3, 4)).reshape(S_ROWS, ATT_HEADS * ATT_V_DIM).astype(BF16)
    on = lax.dynamic_update_slice(on_p, o_s, (P_ROWS, 0))

    mix = mix_branches(yzn, on, proj, w_ssd_out[l].astype(BF16), w_attn_out[l].astype(BF16))
    rw = jnp.concatenate([router_w[l], jnp.zeros((d, LANES - N_EXPERTS), F32)], axis=1)
    rb = _pad_lanes(router_b[l], NEG)
    h2, n_rows, top_i, top_g = post_mixer(mix, x_all, w_o[l].astype(BF16), norm_ffn_w[l].reshape(1, d), rw, rb)

    rank, counts = expert_ranks(top_i)
    counts = counts[0, :N_EXPERTS]
    padded = (counts + MOE_TM - 1) // MOE_TM * MOE_TM
    pad_end = jnp.cumsum(padded)
    pad_start = pad_end - padded
    ex = jnp.arange(N_EXPERTS)
    ti_t = top_i[:, :TOP_K].T
    base_t = jnp.sum(jnp.where(ti_t[None] == ex[:, None, None], pad_start[:, None, None], 0), axis=0)
    dest_t = base_t + rank[:, :TOP_K].T
    tok = jnp.arange(R)[None, :]
    is_tok = tok >= LEAD
    dump_t = MOE_NB * MOE_TM + tok * TOP_K + jnp.arange(TOP_K)[:, None]
    dest_scatter = jnp.where(is_tok, dest_t, dump_t).astype(I32).T.reshape(-1)
    dest_gather = jnp.where(is_tok, dest_t, 0).astype(I32).T.reshape(-1)
    block_start = jnp.arange(MOE_NB) * MOE_TM
    block_e = jnp.minimum(jnp.sum(pad_end[None, :] <= block_start[:, None], axis=1), N_EXPERTS - 1).astype(I32)
    n_used = (pad_end[-1:] // MOE_TM).astype(I32)
    nonempty = padded > 0
    run_of_expert = jnp.cumsum(nonempty) - 1
    later = (ex[None, :] > ex[:, None]) & nonempty[None, :]
    next_expert = jnp.min(jnp.where(later, ex[None, :], N_EXPERTS), axis=1)
    next_expert = jnp.where(next_expert < N_EXPERTS, next_expert, -1)
    of_block = lambda per_expert: jnp.sum(jnp.where(block_e[:, None] == ex[None, :], per_expert[None, :], 0), axis=1)
    sched = (block_e, n_used, of_block(run_of_expert).astype(I32), of_block(next_expert).astype(I32))
    tail_start = jnp.where(nonempty, pad_end - MOE_TM, -1).astype(I32)
    xs = moe_dispatch(dest_scatter, tail_start, n_used, n_rows)
    hidden = moe_gate_up(sched, xs, w_gate_up[l], b_gate_up[l].reshape(N_EXPERTS, 1, 2 * D_FF))
    yb = moe_down(sched, hidden, w_down[l], b_down[l].reshape(N_EXPERTS, 1, d))
    y_p, y_s = moe_combine(dest_gather, h2, top_g, final_norm_w.reshape(1, d), yb)

    tok0 = LEAD
    y_prompt = y_p[None]
    y_sample = y_s.reshape(DEC_BATCH, DEC_SEQ, d)
    k_prompt = k_rows[tok0:P_ROWS].reshape(1, 1, SEQ + N_META, ATT_KV_HEADS, 2, ATT_HEAD_DIM)
    v_prompt = v_rows[tok0:P_ROWS].reshape(1, 1, SEQ + N_META, ATT_KV_HEADS, ATT_V_DIM)
    k_sample = k_rows[P_ROWS:].reshape(1, DEC_BATCH, DEC_SEQ, ATT_KV_HEADS, 2, ATT_HEAD_DIM)
    v_sample = v_rows[P_ROWS:].reshape(1, DEC_BATCH, DEC_SEQ, ATT_KV_HEADS, ATT_V_DIM)
    ssm_prompt = ssm_p.reshape(1, 1, SSD_HEADS, SSD_HEAD_DIM, SSD_STATE)
    ssm_sample = ssm_s.reshape(1, DEC_BATCH, SSD_HEADS, SSD_HEAD_DIM, SSD_STATE)
    xbc = proj[:, PCOL_XBC:PCOL_XBC + CONV_DIM]
    conv_prompt = xbc[P_ROWS - (SSD_CONV - 1):P_ROWS][None, None]
    xbc_s = xbc[P_ROWS:].reshape(DEC_BATCH, DEC_SEQ, CONV_DIM)
    conv_sample = jnp.concatenate([state_conv[l], xbc_s], axis=1)[:, -(SSD_CONV - 1):][None]
    return (y_prompt, y_sample, k_prompt, v_prompt, k_sample, v_sample, ssm_prompt, ssm_sample, conv_prompt, conv_sample)
```
